```python
import functools
import jax, jax.numpy as jnp
from jax import lax
import numpy as np

D_MODEL = 1024
BATCH = 8
SEQ = 2048
DEPTH = 1
DEC_BATCH = 32
DEC_SEQ = 1
PAST_LEN = 16384
PAGE_SIZE = 128

N_META = 16
HEAD_DIM = 64
R_HEADS = (D_MODEL // 2) // HEAD_DIM
F_HEADS = (D_MODEL // 2) // HEAD_DIM
R_WIDTH = R_HEADS * HEAD_DIM
F_WIDTH = F_HEADS * HEAD_DIM
DECAY_LORA = max(32, int(round(D_MODEL ** 0.5 * 1.8 / 32)) * 32)
AAA_LORA = max(32, int(round(D_MODEL ** 0.5 * 1.8 / 32)) * 32)
GATE_LORA = max(32, int(round(D_MODEL ** 0.8 * 0.6 / 32)) * 32)
FFN_DIM = ((8 * D_MODEL // 3) + 127) // 128 * 128
Q_BLOCK = 128
NORM_EPS = 1e-6
GN_EPS = HEAD_DIM * 1e-5
NEG_INF = -1e30
R_SPLITS = [R_WIDTH, 2 * R_WIDTH, 3 * R_WIDTH, 3 * R_WIDTH + DECAY_LORA, 3 * R_WIDTH + DECAY_LORA + AAA_LORA]
R_SHIFT_COLS = 3 * R_WIDTH + DECAY_LORA + AAA_LORA + GATE_LORA
F_SPLITS = [F_WIDTH, 2 * F_WIDTH, 3 * F_WIDTH]
F_COLS = 3 * F_WIDTH + F_HEADS
GATE_COLS = 2 * D_MODEL
IN_COLS = R_SHIFT_COLS + F_COLS + GATE_COLS

kernel_name = "rwkv7_fox_gated_macaron_step"


def rms_norm(x, g):
    xf = x.astype(jnp.float32)
    y = xf * lax.rsqrt(jnp.mean(xf * xf, axis=-1, keepdims=True) + NORM_EPS)
    return (y * g.astype(jnp.float32)).astype(x.dtype)


def swiglu(x, w_gate, w_up, w_down):
    return (jax.nn.silu(x @ w_gate) * (x @ w_up)) @ w_down


def rwkv7_branch(zr, shift_prev, state0, mu_shift, w0, w_decay_up, a0, w_aaa_up, w_gate_up,
                 k_k, k_a, r_k, gn_g, gn_b):
    f32 = jnp.float32
    B, T, _ = zr.shape
    z_prev = jnp.concatenate([shift_prev[:, None, :].astype(zr.dtype), zr[:, :-1]], axis=1)
    zs = zr + (z_prev - zr) * mu_shift
    r, k, v, d_lo, a_lo, g_lo = jnp.split(zs, R_SPLITS, axis=-1)
    w_log = -jax.nn.softplus(-(w0 + jnp.tanh(d_lo) @ w_decay_up).astype(f32)) - 0.5
    decay = jnp.exp(-jnp.exp(w_log))
    a = jax.nn.sigmoid(a0 + a_lo @ w_aaa_up)
    g = jax.nn.sigmoid(g_lo) @ w_gate_up
    kk = k * k_k
    k = k * (1 + (a - 1) * k_a)
    hd = lambda t: t.astype(f32).reshape(B, T, R_HEADS, HEAD_DIM)
    rh, kh, vh, ah, dh, kkh = (hd(t) for t in (r, k, v, a, decay, kk))
    kkh = kkh * lax.rsqrt(jnp.sum(kkh * kkh, axis=-1, keepdims=True) + 1e-12)
    bh = kkh * ah

    def step(S, inp):
        r_t, w_t, k_t, v_t, kk_t, b_t = inp
        sa = -jnp.einsum('bhvk,bhk->bhv', S, kk_t)
        S = S * w_t[:, :, None, :] + sa[..., None] * b_t[:, :, None, :] + v_t[..., None] * k_t[:, :, None, :]
        return S, jnp.einsum('bhvk,bhk->bhv', S, r_t)

    seq_major = lambda t: jnp.swapaxes(t, 0, 1)
    S_T, ys = lax.scan(step, state0.astype(f32), tuple(seq_major(t) for t in (rh, dh, kh, vh, kkh, bh)))
    y = seq_major(ys)
    mean = jnp.mean(y, axis=-1, keepdims=True)
    var = jnp.mean(jnp.square(y - mean), axis=-1, keepdims=True)
    yn = ((y - mean) * lax.rsqrt(var + GN_EPS) * gn_g.astype(f32).reshape(R_HEADS, HEAD_DIM)
          + gn_b.astype(f32).reshape(R_HEADS, HEAD_DIM))
    bonus = jnp.sum(rh * kh * r_k.astype(f32), axis=-1, keepdims=True) * vh
    out = (yn + bonus).reshape(B, T, R_WIDTH).astype(zr.dtype) * g
    return out, S_T, zr[:, -1]


def fox_project(zf, q_norm, k_norm, b_f):
    B, T, _ = zf.shape
    q, k, v, f_logit = jnp.split(zf, F_SPLITS, axis=-1)
    hd = lambda t: t.reshape(B, T, F_HEADS, HEAD_DIM)
    q = rms_norm(hd(q), q_norm)
    k = rms_norm(hd(k), k_norm)
    logf = jax.nn.log_sigmoid((f_logit + b_f).astype(jnp.float32))
    return q, k, hd(v), logf


def fox_scores(q, k, cq, ck):
    s = jnp.einsum('bqhd,bkhd->bhqk', q, k, preferred_element_type=jnp.float32) * (HEAD_DIM ** -0.5)
    return s + (jnp.swapaxes(cq, 1, 2)[..., :, None] - jnp.swapaxes(ck, 1, 2)[..., None, :])


def fox_prompt(q, k, v, logf):
    B, L, H, D = q.shape
    pad = (-L) % Q_BLOCK
    padf = lambda t: jnp.pad(t, [(0, 0), (pad, 0)] + [(0, 0)] * (t.ndim - 2))
    qp, kp, vp = padf(q), padf(k), padf(v)
    c = jnp.cumsum(padf(logf), axis=1)
    Lp = L + pad
    outs = []
    for blk in range(Lp // Q_BLOCK):
        q0, q1 = blk * Q_BLOCK, (blk + 1) * Q_BLOCK
        s = fox_scores(qp[:, q0:q1], kp[:, :q1], c[:, q0:q1], c[:, :q1])
        kpos = np.arange(q1)
        qpos = np.arange(q0, q1)
        mask = (kpos[None, :] <= qpos[:, None]) & (kpos[None, :] >= pad)
        p = jax.nn.softmax(jnp.where(mask, s, NEG_INF), axis=-1)
        outs.append(jnp.einsum('bhqk,bkhd->bqhd', p.astype(vp.dtype), vp[:, :q1]))
    o = jnp.concatenate(outs, axis=1)[:, pad:]
    return o.reshape(B, L, H * D)


def fox_sample(q, k_new, v_new, logf_new, k_past, v_past, logf_past):
    B, T, H, D = q.shape
    P = k_past.shape[1]
    c_past = jnp.cumsum(logf_past.astype(jnp.float32), axis=1)
    c_new = c_past[:, -1:] + jnp.cumsum(logf_new, axis=1)
    s_past = fox_scores(q, k_past, c_new, c_past)
    causal = np.arange(T)[None, :] <= np.arange(T)[:, None]
    s_new = jnp.where(causal, fox_scores(q, k_new, c_new, c_new), NEG_INF)
    p = jax.nn.softmax(jnp.concatenate([s_past, s_new], axis=-1), axis=-1).astype(v_new.dtype)
    o = (jnp.einsum('bhqk,bkhd->bqhd', p[..., :P], v_past)
         + jnp.einsum('bhqk,bkhd->bqhd', p[..., P:], v_new))
    return o.reshape(B, T, H * D)


def trunk_layer(x, shift_prev, wkv0, attend, p):
    x = x + 0.5 * swiglu(rms_norm(x, p['ffn1_norm']), p['ffn1_w_gate'], p['ffn1_w_up'], p['ffn1_w_down'])
    z = rms_norm(x, p['mix_norm']) @ p['w_in']
    zr, zf, zg = jnp.split(z, [R_SHIFT_COLS, R_SHIFT_COLS + F_COLS], axis=-1)
    rwkv_out, wkv_T, shift_T = rwkv7_branch(zr, shift_prev, wkv0, p['mu_shift'], p['w0'], p['w_decay_up'],
                                            p['a0'], p['w_aaa_up'], p['w_gate_up'], p['k_k'], p['k_a'],
                                            p['r_k'], p['gn_g'], p['gn_b'])
    q, k, v, logf = fox_project(zf, p['q_norm'], p['k_norm'], p['b_f'])
    fox_out = attend(q, k, v, logf)
    gate_a, gate_b = jnp.split(jax.nn.sigmoid(zg), 2, axis=-1)
    merged = gate_a * (rwkv_out @ p['w_a']) + gate_b * (fox_out @ p['w_b'])
    x = x + merged @ p['w_o']
    x = x + 0.5 * swiglu(rms_norm(x, p['ffn2_norm']), p['ffn2_w_gate'], p['ffn2_w_up'], p['ffn2_w_down'])
    return x, (k, v, logf.astype(x.dtype), wkv_T.astype(x.dtype), shift_T)


def setup_inputs(seed: int = 0) -> dict:
    key = jax.random.key(seed)
    keys = iter(jax.random.split(key, 64))
    f32 = jnp.float32
    nrm = lambda shape, scale: scale * jax.random.normal(next(keys), shape, f32)
    uni = lambda shape, lo, hi: jax.random.uniform(next(keys), shape, f32, lo, hi)
    Ld = DEPTH
    n_pages = PAST_LEN // PAGE_SIZE
    n_used = DEC_BATCH * n_pages
    n_phys = n_used + n_used // 4
    perm = jax.random.permutation(next(keys), n_phys)
    page_table = perm[:n_used].reshape(DEC_BATCH, n_pages).astype(jnp.int32)
    return {
        'x_prompt': nrm((BATCH, SEQ, D_MODEL), 1.0),
        'x_sample': nrm((DEC_BATCH, DEC_SEQ, D_MODEL), 1.0),
        'cache_k': nrm((Ld, n_phys, PAGE_SIZE, F_HEADS, HEAD_DIM), 1.0),
        'cache_v': nrm((Ld, n_phys, PAGE_SIZE, F_HEADS, HEAD_DIM), 1.0),
        'cache_logf': jax.nn.log_sigmoid(3.5 + nrm((Ld, n_phys, PAGE_SIZE, F_HEADS), 1.0)),
        'state_wkv': nrm((Ld, DEC_BATCH, R_HEADS, HEAD_DIM, HEAD_DIM), 0.3),
        'state_shift': nrm((Ld, DEC_BATCH, R_SHIFT_COLS), 1.0),
        'page_table': page_table,
        'meta_tokens': nrm((N_META, D_MODEL), 1.0),
        'ffn1_norm': 1.0 + nrm((Ld, D_MODEL), 0.02),
        'ffn1_w_gate': nrm((Ld, D_MODEL, FFN_DIM), D_MODEL ** -0.5),
        'ffn1_w_up': nrm((Ld, D_MODEL, FFN_DIM), D_MODEL ** -0.5),
        'ffn1_w_down': nrm((Ld, FFN_DIM, D_MODEL), FFN_DIM ** -0.5),
        'mix_norm': 1.0 + nrm((Ld, D_MODEL), 0.02),
        'w_in': nrm((Ld, D_MODEL, IN_COLS), D_MODEL ** -0.5),
        'mu_shift': uni((Ld, R_SHIFT_COLS), 0.0, 1.0),
        'w0': -1.0 + nrm((Ld, R_WIDTH), 0.5),
        'w_decay_up': nrm((Ld, DECAY_LORA, R_WIDTH), DECAY_LORA ** -0.5),
        'a0': nrm((Ld, R_WIDTH), 0.1),
        'w_aaa_up': nrm((Ld, AAA_LORA, R_WIDTH), 0.5 * AAA_LORA ** -0.5),
        'w_gate_up': nrm((Ld, GATE_LORA, R_WIDTH), GATE_LORA ** -0.5),
        'k_k': 0.85 + nrm((Ld, R_WIDTH), 0.02),
        'k_a': 1.0 + nrm((Ld, R_WIDTH), 0.02),
        'r_k': nrm((Ld, R_HEADS, HEAD_DIM), 0.1),
        'gn_g': 1.0 + nrm((Ld, R_WIDTH), 0.02),
        'gn_b': nrm((Ld, R_WIDTH), 0.02),
        'q_norm': 1.0 + nrm((Ld, HEAD_DIM), 0.02),
        'k_norm': 1.0 + nrm((Ld, HEAD_DIM), 0.02),
        'b_f': uni((Ld, F_HEADS), 1.0, 6.0),
        'w_a': nrm((Ld, R_WIDTH, D_MODEL), R_WIDTH ** -0.5),
        'w_b': nrm((Ld, F_WIDTH, D_MODEL), F_WIDTH ** -0.5),
        'w_o': nrm((Ld, D_MODEL, D_MODEL), D_MODEL ** -0.5),
        'ffn2_norm': 1.0 + nrm((Ld, D_MODEL), 0.02),
        'ffn2_w_gate': nrm((Ld, D_MODEL, FFN_DIM), D_MODEL ** -0.5),
        'ffn2_w_up': nrm((Ld, D_MODEL, FFN_DIM), D_MODEL ** -0.5),
        'ffn2_w_down': nrm((Ld, FFN_DIM, D_MODEL), FFN_DIM ** -0.5),
    }


def reference(x_prompt, x_sample, cache_k, cache_v, cache_logf, state_wkv, state_shift, page_table,
              meta_tokens, ffn1_norm, ffn1_w_gate, ffn1_w_up, ffn1_w_down, mix_norm, w_in, mu_shift,
              w0, w_decay_up, a0, w_aaa_up, w_gate_up, k_k, k_a, r_k, gn_g, gn_b, q_norm, k_norm, b_f,
              w_a, w_b, w_o, ffn2_norm, ffn2_w_gate, ffn2_w_up, ffn2_w_down):
    bp = x_prompt.shape[0]
    bs = x_sample.shape[0]
    n_pages = PAST_LEN // PAGE_SIZE
    meta = jnp.broadcast_to(meta_tokens[None].astype(x_prompt.dtype), (bp, N_META, D_MODEL))
    h_p = jnp.concatenate([meta, x_prompt], axis=1)
    h_s = x_sample
    rows_p, rows_s = [], []
    for l in range(DEPTH):
        p = {
            'ffn1_norm': ffn1_norm[l], 'ffn1_w_gate': ffn1_w_gate[l], 'ffn1_w_up': ffn1_w_up[l],
            'ffn1_w_down': ffn1_w_down[l], 'mix_norm': mix_norm[l], 'w_in': w_in[l], 'mu_shift': mu_shift[l],
            'w0': w0[l], 'w_decay_up': w_decay_up[l], 'a0': a0[l], 'w_aaa_up': w_aaa_up[l],
            'w_gate_up': w_gate_up[l], 'k_k': k_k[l], 'k_a': k_a[l], 'r_k': r_k[l], 'gn_g': gn_g[l],
            'gn_b': gn_b[l], 'q_norm': q_norm[l], 'k_norm': k_norm[l], 'b_f': b_f[l], 'w_a': w_a[l],
            'w_b': w_b[l], 'w_o': w_o[l], 'ffn2_norm': ffn2_norm[l], 'ffn2_w_gate': ffn2_w_gate[l],
            'ffn2_w_up': ffn2_w_up[l], 'ffn2_w_down': ffn2_w_down[l],
        }
        h_p, st_p = trunk_layer(h_p, jnp.zeros((bp, R_SHIFT_COLS), h_p.dtype),
                                jnp.zeros((bp, R_HEADS, HEAD_DIM, HEAD_DIM), jnp.float32), fox_prompt, p)
        k_past = cache_k[l][page_table].reshape(bs, n_pages * PAGE_SIZE, F_HEADS, HEAD_DIM)
        v_past = cache_v[l][page_table].reshape(bs, n_pages * PAGE_SIZE, F_HEADS, HEAD_DIM)
        lf_past = cache_logf[l][page_table].reshape(bs, n_pages * PAGE_SIZE, F_HEADS)
        attend_s = functools.partial(fox_sample, k_past=k_past, v_past=v_past, logf_past=lf_past)
        h_s, st_s = trunk_layer(h_s, state_shift[l], state_wkv[l], attend_s, p)
        rows_p.append(st_p)
        rows_s.append(st_s)
    y_prompt = h_p[:, N_META:]
    y_sample = h_s
    stk = lambda rows, i: jnp.stack([r[i] for r in rows], axis=0)
    return (y_prompt, y_sample,
            stk(rows_p, 0), stk(rows_p, 1), stk(rows_p, 2), stk(rows_p, 3), stk(rows_p, 4),
            stk(rows_s, 0), stk(rows_s, 1), stk(rows_s, 2), stk(rows_s, 3), stk(rows_s, 4))
```

```python
import functools

import numpy as np
import jax
import jax.numpy as jnp
from jax import lax
from jax.experimental import pallas as pl
from jax.experimental.pallas import tpu as pltpu

F32 = jnp.float32
BF16 = jnp.bfloat16

D_MODEL = 1024
HEAD_DIM = 64
N_HEADS = 8
WIDTH = N_HEADS * HEAD_DIM
N_META = 16
DECAY_LORA = 64
AAA_LORA = 64
GATE_LORA = 160
FFN_DIM = 2816
R_COLS = 3 * WIDTH + DECAY_LORA + AAA_LORA + GATE_LORA
QKV_COLS = 3 * WIDTH
NORM_EPS = 1e-6
GN_EPS = HEAD_DIM * 1e-5
NEG_INF = -1e30
PAGE = 128
LANES = 128
QUAD = 4 * HEAD_DIM
FFN_CHUNK = 256
VMEM_LIMIT = 56 * 1024 * 1024


def _dot(a, b):
    return jnp.dot(a.astype(BF16), b.astype(BF16), preferred_element_type=F32)


def _dot_nt(a, b):
    return lax.dot_general(a.astype(BF16), b.astype(BF16), (((1,), (1,)), ((), ())),
                           preferred_element_type=F32)


def _dot_tn(a, b):
    return lax.dot_general(a.astype(BF16), b.astype(BF16), (((0,), (0,)), ((), ())),
                           preferred_element_type=F32)


def _split(x, n):
    parts = []
    r = x
    for i in range(n):
        p = r.astype(BF16)
        parts.append(p)
        if i + 1 < n:
            r = r - p.astype(F32)
    return parts


def _dot_x(a, b, dot=_dot, na=1, nb=1):
    pa = _split(a, na) if na > 1 else [a]
    pb = _split(b, nb) if nb > 1 else [b]
    out = None
    for i, x in enumerate(pa):
        for j, y in enumerate(pb):
            if i + j >= max(na, nb):
                continue
            t = dot(x, y)
            out = t if out is None else out + t
    return out


def _sigmoid(x):
    return 1.0 / (1.0 + jnp.exp(-x))


def _rms(x, g):
    ms = jnp.mean(x * x, axis=-1, keepdims=True)
    return x * lax.rsqrt(ms + NORM_EPS) * g


def _head_sum(x, ones_bd):
    return _dot_x(x, ones_bd, na=2)


def _params(sem, vmem=VMEM_LIMIT):
    return pltpu.CompilerParams(dimension_semantics=sem, vmem_limit_bytes=vmem)


def _resident(shape):
    nd = len(shape)
    return pl.BlockSpec(shape, lambda *_: (0,) * nd, pipeline_mode=pl.Buffered(1))


def _np_ones_bd(n, blk):
    i = np.arange(n)
    return (i[:, None] // blk == i[None, :] // blk).astype(np.float32)


def _ffn_kernel(x_ref, g_ref, wg_ref, wu_ref, wd_ref, o_ref, acc_ref):
    x = x_ref[...]
    xn = _rms(x, g_ref[...]).astype(BF16)
    for c in range(FFN_DIM // FFN_CHUNK):
        sl = slice(c * FFN_CHUNK, (c + 1) * FFN_CHUNK)
        gate = jnp.dot(xn, wg_ref[:, sl], preferred_element_type=F32)
        up = jnp.dot(xn, wu_ref[:, sl], preferred_element_type=F32)
        h = (gate * _sigmoid(gate) * up).astype(BF16)
        part = jnp.dot(h, wd_ref[sl, :], preferred_element_type=F32)
        if c == 0:
            acc_ref[...] = part
        else:
            acc_ref[...] += part
    o_ref[...] = x + 0.5 * acc_ref[...]


def _ffn(x, norm_g, w_gate, w_up, w_down, tm):
    m = x.shape[0]
    return pl.pallas_call(
        _ffn_kernel,
        grid=(m // tm,),
        in_specs=[
            pl.BlockSpec((tm, D_MODEL), lambda i: (i, 0)),
            _resident((1, D_MODEL)),
            _resident((D_MODEL, FFN_DIM)),
            _resident((D_MODEL, FFN_DIM)),
            _resident((FFN_DIM, D_MODEL)),
        ],
        out_specs=pl.BlockSpec((tm, D_MODEL), lambda i: (i, 0)),
        out_shape=jax.ShapeDtypeStruct((m, D_MODEL), F32),
        scratch_shapes=[pltpu.VMEM((tm, D_MODEL), F32)],
        compiler_params=_params(("parallel",)),
        name="ffn",
    )(x, norm_g, w_gate, w_up, w_down)


def _proj_kernel(x_ref, g_ref, wr_ref, wq_ref, wf_ref, wg_ref, zr_ref, zq_ref, zf_ref, zg_ref):
    xn = _rms(x_ref[...], g_ref[...]).astype(BF16)
    zr_ref[...] = jnp.dot(xn, wr_ref[...], preferred_element_type=F32)
    zq_ref[...] = jnp.dot(xn, wq_ref[...], preferred_element_type=F32)
    zf_ref[...] = jnp.dot(xn, wf_ref[...], preferred_element_type=F32)
    zg_ref[...] = jnp.dot(xn, wg_ref[...], preferred_element_type=F32)


def _proj(x, norm_g, w_r, w_q, w_f, w_g, tm):
    m = x.shape[0]
    row = lambda n: pl.BlockSpec((tm, n), lambda i: (i, 0))
    return pl.pallas_call(
        _proj_kernel,
        grid=(m // tm,),
        in_specs=[row(D_MODEL), _resident((1, D_MODEL)), _resident((D_MODEL, R_COLS)),
                  _resident((D_MODEL, QKV_COLS)), _resident((D_MODEL, LANES)),
                  _resident((D_MODEL, 2 * D_MODEL))],
        out_specs=[row(R_COLS), row(QKV_COLS), row(LANES), row(2 * D_MODEL)],
        out_shape=[jax.ShapeDtypeStruct((m, R_COLS), F32),
                   jax.ShapeDtypeStruct((m, QKV_COLS), F32),
                   jax.ShapeDtypeStruct((m, LANES), F32),
                   jax.ShapeDtypeStruct((m, 2 * D_MODEL), F32)],
        compiler_params=_params(("parallel",)),
        name="in_proj",
    )(x, norm_g, w_r, w_q, w_f, w_g)


def _prep_math(z, zprev, mu, w0, wdu, a0, wau, wgu, k_k, k_a, r_k, ones_bd, outs):
    r_ref, lw_ref, k_ref, v_ref, a_ref, b_ref, g_ref, bonus_ref = outs
    zs = z + (zprev - z) * mu
    r = zs[:, 0:WIDTH]
    k = zs[:, WIDTH:2 * WIDTH]
    v = zs[:, 2 * WIDTH:3 * WIDTH]
    o = 3 * WIDTH
    d_lo = zs[:, o:o + DECAY_LORA]
    a_lo = zs[:, o + DECAY_LORA:o + DECAY_LORA + AAA_LORA]
    g_lo = zs[:, o + DECAY_LORA + AAA_LORA:R_COLS]
    wpre = w0 + _dot(jnp.tanh(d_lo), wdu)
    y = -wpre
    softplus = jnp.maximum(y, 0.0) + jnp.log(1.0 + jnp.exp(-jnp.abs(y)))
    w_log = -softplus - 0.5
    lw_ref[...] = -jnp.exp(w_log)
    a = _sigmoid(a0 + _dot(a_lo, wau))
    g_ref[...] = _dot(_sigmoid(g_lo), wgu)
    kk = k * k_k
    k2 = k * (1.0 + (a - 1.0) * k_a)
    kk = kk * lax.rsqrt(_head_sum(kk * kk, ones_bd) + 1e-12)
    r_ref[...] = r
    k_ref[...] = k2
    v_ref[...] = v
    a_ref[...] = -kk
    b_ref[...] = kk * a
    bonus_ref[...] = _head_sum(r * k2 * r_k, ones_bd) * v


def _prep_prompt_kernel(seq_len, z_ref, zp_ref, mu, w0, wdu, a0, wau, wgu, k_k, k_a, r_k, ones_bd, *outs):
    z = z_ref[...]
    tm = z.shape[0]
    rolled = pltpu.roll(z, 1, 0)
    row = lax.broadcasted_iota(jnp.int32, (tm, 1), 0)
    zprev = jnp.where(row == 0, zp_ref[7:8, :], rolled)
    grow = row + pl.program_id(0) * tm
    zprev = jnp.where(grow % seq_len == 0, 0.0, zprev)
    _prep_math(z, zprev, mu[...], w0[...], wdu[...], a0[...], wau[...], wgu[...], k_k[...], k_a[...],
               r_k[...], ones_bd[...], outs)


def _prep_sample_kernel(z_ref, zp_ref, mu, w0, wdu, a0, wau, wgu, k_k, k_a, r_k, ones_bd, *outs):
    _prep_math(z_ref[...], zp_ref[...], mu[...], w0[...], wdu[...], a0[...], wau[...], wgu[...],
               k_k[...], k_a[...], r_k[...], ones_bd[...], outs)


def _prep(zr, zprev, pw, tm, seq_len):
    m = zr.shape[0]
    row = lambda n: pl.BlockSpec((tm, n), lambda i: (i, 0))
    if zprev is None:
        body = functools.partial(_prep_prompt_kernel, seq_len)
        prev_spec = pl.BlockSpec((8, R_COLS), lambda i: (jnp.maximum(i * (tm // 8) - 1, 0), 0))
        zprev = zr
    else:
        body = _prep_sample_kernel
        prev_spec = row(R_COLS)
    consts = [pw["mu"], pw["w0"], pw["wdu"], pw["a0"], pw["wau"], pw["wgu"], pw["k_k"], pw["k_a"],
              pw["r_k"], pw["ones_bd"]]
    return pl.pallas_call(
        body,
        grid=(m // tm,),
        in_specs=[row(R_COLS), prev_spec] + [_resident(c.shape) for c in consts],
        out_specs=[row(WIDTH)] * 8,
        out_shape=[jax.ShapeDtypeStruct((m, WIDTH), F32)] * 8,
        compiler_params=_params(("parallel",)),
        name="rwkv_prep",
    )(zr, zprev, *consts)


def _rec_masks(c):
    t = np.arange(c)[:, None]
    hs = np.arange(4 * c)[None, :]
    s = hs % c
    strict = (s < t).astype(np.float32)
    incl = (s <= t).astype(np.float32)
    eye = (s == t).astype(np.float32)
    rows = np.arange(4 * c)[:, None]
    bd = (rows // c == hs // c).astype(np.float32)
    tri = (np.arange(c)[None, :] <= np.arange(c)[:, None]).astype(np.float32)
    lane = np.arange(QUAD)[None, :] // HEAD_DIM
    head = (lane == np.arange(4)[:, None]).astype(np.float32)
    sbd = _np_ones_bd(QUAD, HEAD_DIM)
    return strict, incl, eye, bd, tri, head, sbd


def _stack_heads(x, head):
    return jnp.concatenate([x * head[h:h + 1, :] for h in range(4)], axis=0)


def _rec_kernel(c, first_chunk, n_doubling, r_ref, lw_ref, k_ref, v_ref, a_ref, b_ref, s0_ref,
                strict_ref, incl_ref, eye_ref, bd_ref, tri_ref, head_ref, sbd_ref,
                y_ref, sT_ref, s_ref):
    t_len = r_ref.shape[0]
    s_ref[...] = s0_ref[...]
    if first_chunk > 0:
        y_ref[0:first_chunk * c, :] = jnp.zeros((first_chunk * c, QUAD), F32)

    def to_bd(p_row):
        return jnp.concatenate([p_row] * 4, axis=0) * bd_ref[...]

    def chunk(ci, carry):
        strict = strict_ref[...]
        incl = incl_ref[...]
        head = head_ref[...]
        rows = pl.ds(pl.multiple_of(ci * c, c), c)
        r = r_ref[rows, :]
        lw = lw_ref[rows, :]
        k = k_ref[rows, :]
        v = v_ref[rows, :]
        a = a_ref[rows, :]
        b = b_ref[rows, :]
        s_old = s_ref[...]

        cum = _dot_x(tri_ref[...], lw, nb=3)
        cum_last = cum[c - 1:c, :]
        w_inc = jnp.exp(cum)
        e_inv = jnp.exp(-cum)
        at = a * jnp.exp(cum - lw)
        rt = r * w_inc
        bi = b * e_inv
        ki = k * e_inv
        e_last = jnp.exp(cum_last - cum)
        bc = b * e_last
        kc = k * e_last

        lhs = jnp.concatenate([at, rt], axis=0)
        rhs = jnp.concatenate([_stack_heads(bi, head), _stack_heads(ki, head)], axis=0)
        gmat = _dot_x(lhs, rhs, dot=_dot_nt, na=2, nb=2)
        n_row = gmat[0:c, 0:4 * c] * strict
        m_ak = gmat[0:c, 4 * c:8 * c] * strict
        n_rb = gmat[c:2 * c, 0:4 * c] * incl
        n_rk = gmat[c:2 * c, 4 * c:8 * c] * incl

        t_row = eye_ref[...] + n_row
        p_row = n_row
        for _ in range(n_doubling):
            p_row = _dot_x(p_row, to_bd(p_row), na=2, nb=2)
            t_row = t_row + _dot_x(t_row, to_bd(p_row), na=2, nb=2)

        v_st = _stack_heads(v, head)
        x = _dot_x(at, s_old, dot=_dot_nt, na=2, nb=2) + _dot_x(m_ak, v_st, na=2, nb=2)
        u = _dot_x(t_row, _stack_heads(x, head), na=2, nb=2)
        u_st = _stack_heads(u, head)
        y = (_dot_x(rt, s_old, dot=_dot_nt, na=2, nb=2)
             + _dot_x(n_rb, u_st, na=2, nb=2) + _dot_x(n_rk, v_st, na=2, nb=2))
        y_ref[rows, :] = y
        upd = _dot_x(jnp.concatenate([u, v], axis=0), jnp.concatenate([bc, kc], axis=0),
                     dot=_dot_tn, na=2, nb=2)
        s_ref[...] = s_old * jnp.exp(cum_last) + upd * sbd_ref[...]
        return carry

    lax.fori_loop(first_chunk, t_len // c, chunk, 0)
    sT_ref[...] = s_ref[...]


def _rec(arrs, s0, c, first_chunk):
    bsz, t_len, _ = arrs[0].shape
    masks = [jnp.asarray(m) for m in _rec_masks(c)]
    n_doubling = max(int(np.ceil(np.log2(c))) - 1, 0)
    seq = pl.BlockSpec((None, t_len, QUAD), lambda b, q: (b, 0, q))
    st = pl.BlockSpec((None, None, QUAD, QUAD), lambda b, q: (b, q, 0, 0))
    return pl.pallas_call(
        functools.partial(_rec_kernel, c, first_chunk, n_doubling),
        grid=(bsz, 2),
        in_specs=[seq] * 6 + [st] + [_resident(m.shape) for m in masks],
        out_specs=[seq, st],
        out_shape=[jax.ShapeDtypeStruct((bsz, t_len, WIDTH), F32),
                   jax.ShapeDtypeStruct((bsz, 2, QUAD, QUAD), F32)],
        scratch_shapes=[pltpu.VMEM((QUAD, QUAD), F32)],
        compiler_params=_params(("parallel", "parallel")),
        name="rwkv_rec",
    )(*arrs, s0, *masks)


def _foxprep_kernel(n_pad, with_cumsum, q_ref, k_ref, f_ref, qn_ref, kn_ref, bf_ref, ones_bd, tri_ref,
                    qo_ref, ko_ref, lf_ref, *cum_refs):
    q = q_ref[...]
    k = k_ref[...]
    inv = 1.0 / HEAD_DIM
    qo_ref[...] = (q * lax.rsqrt(_head_sum(q * q, ones_bd[...]) * inv + NORM_EPS) * qn_ref[...]
                   * (HEAD_DIM ** -0.5))
    ko_ref[...] = k * lax.rsqrt(_head_sum(k * k, ones_bd[...]) * inv + NORM_EPS) * kn_ref[...]
    x = f_ref[...] + bf_ref[...]
    lf = jnp.minimum(x, 0.0) - jnp.log(1.0 + jnp.exp(-jnp.abs(x)))
    lane = lax.broadcasted_iota(jnp.int32, lf.shape, 1)
    lf = jnp.where(lane < N_HEADS, lf, 0.0)
    if with_cumsum:
        row = lax.broadcasted_iota(jnp.int32, lf.shape, 0)
        lf = jnp.where(row >= n_pad, lf, 0.0)
    lf_ref[...] = lf
    if with_cumsum:
        c_ref, ct_ref = cum_refs
        t_len = lf.shape[0]
        carry = jnp.zeros((1, LANES), F32)
        for blk in range(t_len // LANES):
            rows = slice(blk * LANES, (blk + 1) * LANES)
            cb = _dot_x(tri_ref[...], lf[rows, :], nb=3) + carry
            c_ref[rows, :] = cb
            ct_ref[:, rows] = cb.T[0:N_HEADS, :]
            carry = cb[LANES - 1:LANES, :]


def _foxprep(zq, zf, qn_w, kn_w, bf_row, ones_bd, rows_per_step, n_steps, n_pad, with_cumsum):
    m = zq.shape[0]
    tm = rows_per_step
    tri = jnp.asarray(np.tril(np.ones((LANES, LANES), np.float32)))
    col = lambda j: pl.BlockSpec((tm, WIDTH), lambda i: (i, j))
    row = lambda n: pl.BlockSpec((tm, n), lambda i: (i, 0))
    out_specs = [row(WIDTH), row(WIDTH), row(LANES)]
    out_shape = [jax.ShapeDtypeStruct((m, WIDTH), F32), jax.ShapeDtypeStruct((m, WIDTH), F32),
                 jax.ShapeDtypeStruct((m, LANES), F32)]
    if with_cumsum:
        out_specs += [row(LANES), pl.BlockSpec((None, N_HEADS, tm), lambda i: (i, 0, 0))]
        out_shape += [jax.ShapeDtypeStruct((m, LANES), F32),
                      jax.ShapeDtypeStruct((n_steps, N_HEADS, tm), F32)]
    return pl.pallas_call(
        functools.partial(_foxprep_kernel, n_pad, with_cumsum),
        grid=(n_steps,),
        in_specs=[col(0), col(1), row(LANES), _resident((1, WIDTH)), _resident((1, WIDTH)),
                  _resident((1, LANES)), _resident((WIDTH, WIDTH)), _resident((LANES, LANES))],
        out_specs=out_specs,
        out_shape=out_shape,
        compiler_params=_params(("parallel",)),
        name="fox_prep",
    )(zq, zq, zf, qn_w, kn_w, bf_row, ones_bd, tri)


def _foxattn_kernel(n_pad, n_qtiles, q_ref, k_ref, v_ref, c_ref, ct_ref, o_ref):
    t_len = q_ref.shape[0]
    tq = t_len // n_qtiles
    pair = pl.program_id(1)
    lane = lax.broadcasted_iota(jnp.int32, (1, LANES), 1)
    for qi in range(n_qtiles):
        nk = (qi + 1) * tq
        qrows = slice(qi * tq, nk)
        q = q_ref[qrows, :]
        k = k_ref[0:nk, :].astype(BF16)
        v = v_ref[0:nk, :].astype(BF16)
        qpos = lax.broadcasted_iota(jnp.int32, (tq, nk), 0) + qi * tq
        kpos = lax.broadcasted_iota(jnp.int32, (tq, nk), 1)
        visible = (kpos <= qpos) & (kpos >= n_pad)
        out = jnp.zeros((tq, LANES), F32)
        for h in range(2):
            hmask = (lane // HEAD_DIM == h).astype(F32)
            head = 2 * pair + h
            sel = (lane == head).astype(F32)
            cq = jnp.sum(c_ref[qrows, :] * sel, axis=-1, keepdims=True)
            ck = ct_ref[pl.ds(head, 1), 0:nk]
            s = _dot_nt(q * hmask, k) + (cq - ck)
            s = jnp.where(visible, s, NEG_INF)
            m = jnp.max(s, axis=-1, keepdims=True)
            p = jnp.exp(s - m)
            l = jnp.sum(p, axis=-1, keepdims=True)
            p = p / l
            out = out + jnp.dot(p.astype(BF16), v, preferred_element_type=F32) * hmask
        o_ref[qrows, :] = out


def _foxattn(qn, kn, zq, cum, cum_t, bsz, t_len, n_pad, n_qtiles):
    seq = lambda j0: pl.BlockSpec((t_len, LANES), lambda b, p: (b, j0 + p))
    return pl.pallas_call(
        functools.partial(_foxattn_kernel, n_pad, n_qtiles),
        grid=(bsz, N_HEADS // 2),
        in_specs=[seq(0), seq(0), seq(2 * WIDTH // LANES),
                  pl.BlockSpec((t_len, LANES), lambda b, p: (b, 0)),
                  pl.BlockSpec((None, N_HEADS, t_len), lambda b, p: (b, 0, 0))],
        out_specs=seq(0),
        out_shape=jax.ShapeDtypeStruct((bsz * t_len, WIDTH), F32),
        compiler_params=_params(("parallel", "parallel")),
        name="fox_attn",
    )(qn, kn, zq, cum, cum_t)


def _dec_consts():
    c = np.arange(PAGE)
    rep = (np.arange(8)[None, :] == c[:, None] // 16).astype(np.float32)
    rep = np.pad(rep, ((0, 0), (0, LANES - 8)))
    lmask = (np.arange(LANES)[None, :] // 8 == c[:, None] % 16).astype(np.float32)
    fold_t = (np.arange(LANES)[None, :] % 8 == np.arange(8)[:, None]).astype(np.float32)
    suf = (c[:, None] > c[None, :]).astype(np.float32)
    bdm = (np.arange(WIDTH)[None, :] // HEAD_DIM == np.arange(8)[:, None]).astype(np.float32)
    eye = (np.arange(LANES)[None, :] == np.arange(8)[:, None]).astype(np.float32)
    return rep, lmask, fold_t, suf, bdm, eye


def _dec_kernel(n_group, pt_ref, q_ref, kn_ref, vn_ref, lfn_ref, rep_ref, lmask_ref, foldt_ref, suf_ref,
                bdm_ref, eye_ref, *rest):
    k_refs = rest[0:n_group]
    v_refs = rest[n_group:2 * n_group]
    lf_refs = rest[2 * n_group:3 * n_group]
    o_ref = rest[3 * n_group]
    m_ref, l_ref, acc_ref, carry_ref = rest[3 * n_group + 1:]
    j = pl.program_id(1)
    bdm = bdm_ref[...]
    qbd = q_ref[...] * bdm
    lf_new = jnp.sum(eye_ref[...] * lfn_ref[...], axis=-1, keepdims=True)

    @pl.when(j == 0)
    def _():
        m_ref[...] = jnp.broadcast_to(jnp.sum(qbd * kn_ref[...], axis=-1, keepdims=True), (8, LANES))
        l_ref[...] = jnp.ones((8, LANES), F32)
        acc_ref[...] = vn_ref[...] * bdm
        carry_ref[...] = jnp.zeros((8, LANES), F32)

    carry = carry_ref[:, 0:1]
    qb = qbd.astype(BF16)
    scores = []
    for g in range(n_group):
        dense = lf_refs[g][...]
        dense = jnp.concatenate([dense, jnp.zeros((LANES - 8, LANES), F32)], axis=0)
        expanded = _dot_x(rep_ref[...], dense, nb=3) * lmask_ref[...]
        lf_t = _dot_x(foldt_ref[...], expanded, dot=_dot_nt, nb=3)
        suffix = _dot_x(lf_t, suf_ref[...], na=3)
        s = _dot_nt(qb, k_refs[g][...]) + (lf_new + carry + suffix)
        carry = carry + jnp.sum(lf_t, axis=-1, keepdims=True)
        scores.append(s)
    carry_ref[...] = jnp.broadcast_to(carry, (8, LANES))
    s_all = jnp.concatenate(scores, axis=1)
    m_old = m_ref[:, 0:1]
    m_new = jnp.maximum(m_old, jnp.max(s_all, axis=-1, keepdims=True))
    alpha = jnp.exp(m_old - m_new)
    p = jnp.exp(s_all - m_new)
    l_ref[...] = jnp.broadcast_to(alpha * l_ref[:, 0:1] + jnp.sum(p, axis=-1, keepdims=True), (8, LANES))
    m_ref[...] = jnp.broadcast_to(m_new, (8, LANES))
    acc = alpha * acc_ref[...]
    pb = p.astype(BF16)
    for g in range(n_group):
        acc = acc + jnp.dot(pb[:, g * PAGE:(g + 1) * PAGE], v_refs[g][...].astype(BF16),
                            preferred_element_type=F32)
    acc_ref[...] = acc

    @pl.when(j == pl.num_programs(1) - 1)
    def _():
        o_ref[...] = jnp.sum(acc_ref[...] * bdm / l_ref[:, 0:1], axis=0, keepdims=True)


def _dec(page_table, qn, kn, vn, lfn, cache_k, cache_v, cache_lf, n_group):
    bsz, n_pages = page_table.shape
    n_phys = cache_k.shape[0]
    ck = cache_k.reshape(n_phys, PAGE, WIDTH)
    cv = cache_v.reshape(n_phys, PAGE, WIDTH)
    clf = cache_lf.reshape(n_phys, 8, LANES)
    consts = [jnp.asarray(c) for c in _dec_consts()]
    n_steps = n_pages // n_group
    tok = lambda n: pl.BlockSpec((None, 1, n), lambda b, j, pt: (b, 0, 0))
    const = lambda c: pl.BlockSpec(c.shape, lambda b, j, pt: (0,) * c.ndim, pipeline_mode=pl.Buffered(1))

    def page(shape, g):
        def index(b, j, pt):
            return (pt[b * n_pages + (n_pages - 1 - (j * n_group + g))], 0, 0)
        return pl.BlockSpec((None,) + shape, index)

    grid_spec = pltpu.PrefetchScalarGridSpec(
        num_scalar_prefetch=1,
        grid=(bsz, n_steps),
        in_specs=([tok(WIDTH), tok(WIDTH), tok(WIDTH), tok(LANES)] + [const(c) for c in consts]
                  + [page((PAGE, WIDTH), g) for g in range(n_group)]
                  + [page((PAGE, WIDTH), g) for g in range(n_group)]
                  + [page((8, LANES), g) for g in range(n_group)]),
        out_specs=tok(WIDTH),
        scratch_shapes=[pltpu.VMEM((8, LANES), F32), pltpu.VMEM((8, LANES), F32),
                        pltpu.VMEM((8, WIDTH), F32), pltpu.VMEM((8, LANES), F32)],
    )
    r3 = lambda x: x.reshape(bsz, 1, x.shape[-1])
    return pl.pallas_call(
        functools.partial(_dec_kernel, n_group),
        grid_spec=grid_spec,
        out_shape=jax.ShapeDtypeStruct((bsz, 1, WIDTH), F32),
        compiler_params=_params(("parallel", "arbitrary")),
        name="fox_decode",
    )(page_table.reshape(-1), r3(qn), r3(kn), r3(vn), r3(lfn), *consts,
      *([ck] * n_group), *([cv] * n_group), *([clf] * n_group)).reshape(bsz, WIDTH)


def _merge_kernel(x_ref, y_ref, bonus_ref, g_ref, fox_ref, zg_ref, gng_ref, gnb_ref, ones_bd,
                  wa_ref, wb_ref, wo_ref, o_ref):
    y = y_ref[...]
    inv = 1.0 / HEAD_DIM
    mean = _head_sum(y, ones_bd[...]) * inv
    d = y - mean
    var = _head_sum(d * d, ones_bd[...]) * inv
    yn = d * lax.rsqrt(var + GN_EPS) * gng_ref[...] + gnb_ref[...]
    rw = (yn + bonus_ref[...]) * g_ref[...]
    zg = zg_ref[...]
    merged = (_sigmoid(zg[:, 0:D_MODEL]) * _dot(rw, wa_ref[...])
              + _sigmoid(zg[:, D_MODEL:]) * _dot(fox_ref[...], wb_ref[...]))
    o_ref[...] = x_ref[...] + _dot(merged, wo_ref[...])


def _merge(x1, y, bonus, g, fox, zg, gn_g, gn_b, ones_bd, w_a, w_b, w_o, tm):
    m = x1.shape[0]
    row = lambda n: pl.BlockSpec((tm, n), lambda i: (i, 0))
    return pl.pallas_call(
        _merge_kernel,
        grid=(m // tm,),
        in_specs=[row(D_MODEL), row(WIDTH), row(WIDTH), row(WIDTH), row(WIDTH), row(2 * D_MODEL),
                  _resident((1, WIDTH)), _resident((1, WIDTH)), _resident((WIDTH, WIDTH)),
                  _resident((WIDTH, D_MODEL)), _resident((WIDTH, D_MODEL)), _resident((D_MODEL, D_MODEL))],
        out_specs=row(D_MODEL),
        out_shape=jax.ShapeDtypeStruct((m, D_MODEL), F32),
        compiler_params=_params(("parallel",)),
        name="merge",
    )(x1, y, bonus, g, fox, zg, gn_g, gn_b, ones_bd, w_a, w_b, w_o)


def _to_blockdiag(state):
    bsz = state.shape[0]
    s = state.reshape(bsz, 2, 4, HEAD_DIM, HEAD_DIM)
    eye = jnp.eye(4, dtype=state.dtype)
    out = jnp.einsum("bqhvk,hg->bqhvgk", s, eye)
    return out.reshape(bsz, 2, QUAD, QUAD)


def _from_blockdiag(sbd):
    bsz = sbd.shape[0]
    s = sbd.reshape(bsz, 2, 4, HEAD_DIM, 4, HEAD_DIM)
    idx = jnp.arange(4)
    s = s[:, :, idx, :, idx, :]
    return jnp.moveaxis(s, 0, 2).reshape(bsz, N_HEADS, HEAD_DIM, HEAD_DIM)


def _layer_weights(l, ffn1_norm, ffn1_w_gate, ffn1_w_up, ffn1_w_down, mix_norm, w_in, mu_shift, w0,
                   w_decay_up, a0, w_aaa_up, w_gate_up, k_k, k_a, r_k, gn_g, gn_b, q_norm, k_norm, b_f,
                   w_a, w_b, w_o, ffn2_norm, ffn2_w_gate, ffn2_w_up, ffn2_w_down):
    row = lambda x: x.reshape(1, -1).astype(F32)
    bf = lambda x: x.astype(BF16)
    wi = w_in[l]
    w_f = jnp.pad(wi[:, R_COLS + QKV_COLS:R_COLS + QKV_COLS + N_HEADS], ((0, 0), (0, LANES - N_HEADS)))
    return dict(
        ffn1=(row(ffn1_norm[l]), bf(ffn1_w_gate[l]), bf(ffn1_w_up[l]), bf(ffn1_w_down[l])),
        ffn2=(row(ffn2_norm[l]), bf(ffn2_w_gate[l]), bf(ffn2_w_up[l]), bf(ffn2_w_down[l])),
        mix_norm=row(mix_norm[l]),
        w_r=bf(wi[:, 0:R_COLS]), w_q=bf(wi[:, R_COLS:R_COLS + QKV_COLS]), w_f=bf(w_f),
        w_g=bf(wi[:, R_COLS + QKV_COLS + N_HEADS:]),
        prep=dict(mu=row(mu_shift[l]), w0=row(w0[l]), wdu=bf(w_decay_up[l]), a0=row(a0[l]),
                  wau=bf(w_aaa_up[l]), wgu=bf(w_gate_up[l]), k_k=row(k_k[l]), k_a=row(k_a[l]),
                  r_k=row(r_k[l]), ones_bd=jnp.asarray(_np_ones_bd(WIDTH, HEAD_DIM), BF16)),
        gn_g=row(gn_g[l]), gn_b=row(gn_b[l]),
        q_norm=row(jnp.tile(q_norm[l], N_HEADS)), k_norm=row(jnp.tile(k_norm[l], N_HEADS)),
        b_f=jnp.pad(row(b_f[l]), ((0, 0), (0, LANES - N_HEADS))),
        w_a=bf(w_a[l]), w_b=bf(w_b[l]), w_o=bf(w_o[l]),
    )


def _prompt_layer(x, w, bsz, t_len, n_pad):
    ones_bd = w["prep"]["ones_bd"]
    x1 = _ffn(x, *w["ffn1"], tm=544)
    zr, zq, zf, zg = _proj(x1, w["mix_norm"], w["w_r"], w["w_q"], w["w_f"], w["w_g"], tm=272)
    r, lw, k, v, a, b, g, bonus = _prep(zr, None, w["prep"], tm=544, seq_len=t_len)
    seq = lambda arr: arr.reshape(bsz, t_len, WIDTH)
    s0 = jnp.zeros((bsz, 2, QUAD, QUAD), F32)
    y, s_t = _rec([seq(r), seq(lw), seq(k), seq(v), seq(a), seq(b)], s0, c=64, first_chunk=n_pad // 64)
    qn, kn, lf, cum, cum_t = _foxprep(zq, zf, w["q_norm"], w["k_norm"], w["b_f"], ones_bd,
                                      rows_per_step=t_len, n_steps=bsz, n_pad=n_pad, with_cumsum=True)
    fox = _foxattn(qn, kn, zq, cum, cum_t, bsz, t_len, n_pad, n_qtiles=4)
    x2 = _merge(x1, y.reshape(bsz * t_len, WIDTH), bonus, g, fox, zg, w["gn_g"], w["gn_b"], ones_bd,
                w["w_a"], w["w_b"], w["w_o"], tm=272)
    x3 = _ffn(x2, *w["ffn2"], tm=544)
    return x3, zr, zq, kn, lf, s_t


def _sample_layer(x, w, shift_prev, wkv0, page_table, cache_k, cache_v, cache_lf):
    bsz = x.shape[0]
    ones_bd = w["prep"]["ones_bd"]
    x1 = _ffn(x, *w["ffn1"], tm=bsz)
    zr, zq, zf, zg = _proj(x1, w["mix_norm"], w["w_r"], w["w_q"], w["w_f"], w["w_g"], tm=bsz)
    prepped = _prep(zr, shift_prev, w["prep"], tm=bsz, seq_len=1)
    g, bonus = prepped[6], prepped[7]
    c = 32
    pad = lambda arr: jnp.pad(arr.reshape(bsz, 1, WIDTH), ((0, 0), (0, c - 1), (0, 0)))
    y, s_t = _rec([pad(arr) for arr in prepped[:6]], _to_blockdiag(wkv0.astype(F32)), c=c, first_chunk=0)
    y = y[:, 0, :]
    qn, kn, lf = _foxprep(zq, zf, w["q_norm"], w["k_norm"], w["b_f"], ones_bd,
                          rows_per_step=bsz, n_steps=1, n_pad=0, with_cumsum=False)
    vn = zq[:, 2 * WIDTH:]
    fox = _dec(page_table, qn, kn, vn, lf, cache_k, cache_v, cache_lf, n_group=8)
    x2 = _merge(x1, y, bonus, g, fox, zg, w["gn_g"], w["gn_b"], ones_bd, w["w_a"], w["w_b"], w["w_o"],
                tm=bsz)
    x3 = _ffn(x2, *w["ffn2"], tm=bsz)
    return x3, zr, kn, vn, lf, s_t


def kernel(x_prompt, x_sample, cache_k, cache_v, cache_logf, state_wkv, state_shift, page_table, meta_tokens, ffn1_norm, ffn1_w_gate, ffn1_w_up, ffn1_w_down, mix_norm, w_in, mu_shift, w0, w_decay_up, a0, w_aaa_up, w_gate_up, k_k, k_a, r_k, gn_g, gn_b, q_norm, k_norm, b_f, w_a, w_b, w_o, ffn2_norm, ffn2_w_gate, ffn2_w_up, ffn2_w_down):
    depth = w_in.shape[0]
    bp, seq, _ = x_prompt.shape
    bs = x_sample.shape[0]
    l_tok = seq + N_META
    n_pad = (-l_tok) % LANES
    t_len = l_tok + n_pad
    meta = jnp.broadcast_to(meta_tokens[None].astype(F32), (bp, N_META, D_MODEL))
    h_p = jnp.concatenate([jnp.zeros((bp, n_pad, D_MODEL), F32), meta, x_prompt.astype(F32)], axis=1)
    h_p = h_p.reshape(bp * t_len, D_MODEL)
    h_s = x_sample.reshape(bs, D_MODEL).astype(F32)
    rows_p, rows_s = [], []
    for l in range(depth):
        w = _layer_weights(l, ffn1_norm, ffn1_w_gate, ffn1_w_up, ffn1_w_down, mix_norm, w_in, mu_shift, w0,
                           w_decay_up, a0, w_aaa_up, w_gate_up, k_k, k_a, r_k, gn_g, gn_b, q_norm, k_norm,
                           b_f, w_a, w_b, w_o, ffn2_norm, ffn2_w_gate, ffn2_w_up, ffn2_w_down)
        h_p, zr, zq, kn, lf, s_t = _prompt_layer(h_p, w, bp, t_len, n_pad)
        real = lambda arr, n: arr.reshape(bp, t_len, n)[:, n_pad:]
        rows_p.append((real(kn, WIDTH).reshape(bp, l_tok, N_HEADS, HEAD_DIM),
                       real(zq, QKV_COLS)[:, :, 2 * WIDTH:].reshape(bp, l_tok, N_HEADS, HEAD_DIM),
                       real(lf, LANES)[:, :, :N_HEADS],
                       _from_blockdiag(s_t),
                       zr.reshape(bp, t_len, R_COLS)[:, -1]))
        h_s, zr_s, kn_s, vn_s, lf_s, s_ts = _sample_layer(
            h_s, w, state_shift[l].astype(F32), state_wkv[l], page_table, cache_k[l], cache_v[l],
            cache_logf[l])
        rows_s.append((kn_s.reshape(bs, 1, N_HEADS, HEAD_DIM), vn_s.reshape(bs, 1, N_HEADS, HEAD_DIM),
                       lf_s[:, :N_HEADS].reshape(bs, 1, N_HEADS), _from_blockdiag(s_ts), zr_s))
    y_prompt = h_p.reshape(bp, t_len, D_MODEL)[:, n_pad + N_META:]
    y_sample = h_s.reshape(bs, 1, D_MODEL)
    stk = lambda rows, i: jnp.stack([r[i] for r in rows], axis=0)
    return (y_prompt, y_sample,
            stk(rows_p, 0), stk(rows_p, 1), stk(rows_p, 2), stk(rows_p, 3), stk(rows_p, 4),
            stk(rows_s, 0), stk(rows_s, 1), stk(rows_s, 2), stk(rows_s, 3), stk(rows_s, 4))
```

```python
import functools

import numpy as np
import jax
import jax.numpy as jnp
from jax import lax
from jax.experimental import pallas as pl
from jax.experimental.pallas import tpu as pltpu

F32 = jnp.float32
BF16 = jnp.bfloat16

D_MODEL = 1024
HEAD_DIM = 64
N_HEADS = 8
WIDTH = N_HEADS * HEAD_DIM
N_META = 16
DECAY_LORA = 64
AAA_LORA = 64
GATE_LORA = 160
FFN_DIM = 2816
R_COLS = 3 * WIDTH + DECAY_LORA + AAA_LORA + GATE_LORA
QKV_COLS = 3 * WIDTH
NORM_EPS = 1e-6
GN_EPS = HEAD_DIM * 1e-5
NEG_INF = -1e30
PAGE = 128
LANES = 128
QUAD = 4 * HEAD_DIM
FFN_CHUNK = 256
VMEM_LIMIT = 56 * 1024 * 1024


def _dot(a, b):
    return jnp.dot(a.astype(BF16), b.astype(BF16), preferred_element_type=F32)


def _dot_nt(a, b):
    return lax.dot_general(a.astype(BF16), b.astype(BF16), (((1,), (1,)), ((), ())),
                           preferred_element_type=F32)


def _dot_tn(a, b):
    return lax.dot_general(a.astype(BF16), b.astype(BF16), (((0,), (0,)), ((), ())),
                           preferred_element_type=F32)


def _split(x, n):
    parts = []
    r = x
    for i in range(n):
        p = r.astype(BF16)
        parts.append(p)
        if i + 1 < n:
            r = r - p.astype(F32)
    return parts


def _dot_x(a, b, dot=_dot, na=1, nb=1):
    pa = _split(a, na) if na > 1 else [a]
    pb = _split(b, nb) if nb > 1 else [b]
    out = None
    for i, x in enumerate(pa):
        for j, y in enumerate(pb):
            if i + j >= max(na, nb):
                continue
            t = dot(x, y)
            out = t if out is None else out + t
    return out


def _sigmoid(x):
    return 1.0 / (1.0 + jnp.exp(-x))


def _rms(x, g):
    ms = jnp.mean(x * x, axis=-1, keepdims=True)
    return x * lax.rsqrt(ms + NORM_EPS) * g


def _head_sum(x, ones_bd):
    return _dot_x(x, ones_bd, na=2)


def _params(sem, vmem=VMEM_LIMIT):
    return pltpu.CompilerParams(dimension_semantics=sem, vmem_limit_bytes=vmem)


def _resident(shape):
    nd = len(shape)
    return pl.BlockSpec(shape, lambda *_: (0,) * nd, pipeline_mode=pl.Buffered(1))


def _np_ones_bd(n, blk):
    i = np.arange(n)
    return (i[:, None] // blk == i[None, :] // blk).astype(np.float32)


def _ffn_kernel(x_ref, g_ref, wg_ref, wu_ref, wd_ref, o_ref, acc_ref):
    x = x_ref[...]
    xn = _rms(x, g_ref[...]).astype(BF16)
    for c in range(FFN_DIM // FFN_CHUNK):
        sl = slice(c * FFN_CHUNK, (c + 1) * FFN_CHUNK)
        gate = jnp.dot(xn, wg_ref[:, sl], preferred_element_type=F32)
        up = jnp.dot(xn, wu_ref[:, sl], preferred_element_type=F32)
        h = (gate * _sigmoid(gate) * up).astype(BF16)
        part = jnp.dot(h, wd_ref[sl, :], preferred_element_type=F32)
        if c == 0:
            acc_ref[...] = part
        else:
            acc_ref[...] += part
    o_ref[...] = x + 0.5 * acc_ref[...]


def _ffn(x, norm_g, w_gate, w_up, w_down, tm):
    m = x.shape[0]
    return pl.pallas_call(
        _ffn_kernel,
        grid=(m // tm,),
        in_specs=[
            pl.BlockSpec((tm, D_MODEL), lambda i: (i, 0)),
            _resident((1, D_MODEL)),
            _resident((D_MODEL, FFN_DIM)),
            _resident((D_MODEL, FFN_DIM)),
            _resident((FFN_DIM, D_MODEL)),
        ],
        out_specs=pl.BlockSpec((tm, D_MODEL), lambda i: (i, 0)),
        out_shape=jax.ShapeDtypeStruct((m, D_MODEL), F32),
        scratch_shapes=[pltpu.VMEM((tm, D_MODEL), F32)],
        compiler_params=_params(("parallel",)),
        name="ffn",
    )(x, norm_g, w_gate, w_up, w_down)


def _proj_kernel(x_ref, g_ref, wr_ref, wq_ref, wf_ref, wg_ref, zr_ref, zq_ref, zf_ref, zg_ref):
    xn = _rms(x_ref[...], g_ref[...]).astype(BF16)
    zr_ref[...] = jnp.dot(xn, wr_ref[...], preferred_element_type=F32)
    zq_ref[...] = jnp.dot(xn, wq_ref[...], preferred_element_type=F32)
    zf_ref[...] = jnp.dot(xn, wf_ref[...], preferred_element_type=F32)
    zg_ref[...] = jnp.dot(xn, wg_ref[...], preferred_element_type=F32)


def _proj(x, norm_g, w_r, w_q, w_f, w_g, tm):
    m = x.shape[0]
    row = lambda n: pl.BlockSpec((tm, n), lambda i: (i, 0))
    return pl.pallas_call(
        _proj_kernel,
        grid=(m // tm,),
        in_specs=[row(D_MODEL), _resident((1, D_MODEL)), _resident((D_MODEL, R_COLS)),
                  _resident((D_MODEL, QKV_COLS)), _resident((D_MODEL, LANES)),
                  _resident((D_MODEL, 2 * D_MODEL))],
        out_specs=[row(R_COLS), row(QKV_COLS), row(LANES), row(2 * D_MODEL)],
        out_shape=[jax.ShapeDtypeStruct((m, R_COLS), F32),
                   jax.ShapeDtypeStruct((m, QKV_COLS), F32),
                   jax.ShapeDtypeStruct((m, LANES), F32),
                   jax.ShapeDtypeStruct((m, 2 * D_MODEL), F32)],
        compiler_params=_params(("parallel",)),
        name="in_proj",
    )(x, norm_g, w_r, w_q, w_f, w_g)


def _prep_math(z, zprev, mu, w0, wdu, a0, wau, wgu, k_k, k_a, r_k, ones_bd, outs):
    r_ref, lw_ref, k_ref, v_ref, a_ref, b_ref, g_ref, bonus_ref = outs
    zs = z + (zprev - z) * mu
    r = zs[:, 0:WIDTH]
    k = zs[:, WIDTH:2 * WIDTH]
    v = zs[:, 2 * WIDTH:3 * WIDTH]
    o = 3 * WIDTH
    d_lo = zs[:, o:o + DECAY_LORA]
    a_lo = zs[:, o + DECAY_LORA:o + DECAY_LORA + AAA_LORA]
    g_lo = zs[:, o + DECAY_LORA + AAA_LORA:R_COLS]
    wpre = w0 + _dot(jnp.tanh(d_lo), wdu)
    y = -wpre
    softplus = jnp.maximum(y, 0.0) + jnp.log(1.0 + jnp.exp(-jnp.abs(y)))
    w_log = -softplus - 0.5
    lw_ref[...] = -jnp.exp(w_log)
    a = _sigmoid(a0 + _dot(a_lo, wau))
    g_ref[...] = _dot(_sigmoid(g_lo), wgu)
    kk = k * k_k
    k2 = k * (1.0 + (a - 1.0) * k_a)
    kk = kk * lax.rsqrt(_head_sum(kk * kk, ones_bd) + 1e-12)
    r_ref[...] = r
    k_ref[...] = k2
    v_ref[...] = v
    a_ref[...] = -kk
    b_ref[...] = kk * a
    bonus_ref[...] = _head_sum(r * k2 * r_k, ones_bd) * v


def _prep_prompt_kernel(seq_len, z_ref, zp_ref, mu, w0, wdu, a0, wau, wgu, k_k, k_a, r_k, ones_bd, *outs):
    z = z_ref[...]
    tm = z.shape[0]
    rolled = pltpu.roll(z, 1, 0)
    row = lax.broadcasted_iota(jnp.int32, (tm, 1), 0)
    zprev = jnp.where(row == 0, zp_ref[7:8, :], rolled)
    grow = row + pl.program_id(0) * tm
    zprev = jnp.where(grow % seq_len == 0, 0.0, zprev)
    _prep_math(z, zprev, mu[...], w0[...], wdu[...], a0[...], wau[...], wgu[...], k_k[...], k_a[...],
               r_k[...], ones_bd[...], outs)


def _prep_sample_kernel(z_ref, zp_ref, mu, w0, wdu, a0, wau, wgu, k_k, k_a, r_k, ones_bd, *outs):
    _prep_math(z_ref[...], zp_ref[...], mu[...], w0[...], wdu[...], a0[...], wau[...], wgu[...],
               k_k[...], k_a[...], r_k[...], ones_bd[...], outs)


def _prep(zr, zprev, pw, tm, seq_len):
    m = zr.shape[0]
    row = lambda n: pl.BlockSpec((tm, n), lambda i: (i, 0))
    if zprev is None:
        body = functools.partial(_prep_prompt_kernel, seq_len)
        prev_spec = pl.BlockSpec((8, R_COLS), lambda i: (jnp.maximum(i * (tm // 8) - 1, 0), 0))
        zprev = zr
    else:
        body = _prep_sample_kernel
        prev_spec = row(R_COLS)
    consts = [pw["mu"], pw["w0"], pw["wdu"], pw["a0"], pw["wau"], pw["wgu"], pw["k_k"], pw["k_a"],
              pw["r_k"], pw["ones_bd"]]
    return pl.pallas_call(
        body,
        grid=(m // tm,),
        in_specs=[row(R_COLS), prev_spec] + [_resident(c.shape) for c in consts],
        out_specs=[row(WIDTH)] * 8,
        out_shape=[jax.ShapeDtypeStruct((m, WIDTH), F32)] * 8,
        compiler_params=_params(("parallel",)),
        name="rwkv_prep",
    )(zr, zprev, *consts)


def _rec_masks(c):
    t = np.arange(c)[:, None]
    hs = np.arange(4 * c)[None, :]
    s = hs % c
    strict = (s < t).astype(np.float32)
    incl = (s <= t).astype(np.float32)
    eye = (s == t).astype(np.float32)
    rows = np.arange(4 * c)[:, None]
    bd = (rows // c == hs // c).astype(np.float32)
    tri = (np.arange(c)[None, :] <= np.arange(c)[:, None]).astype(np.float32)
    lane = np.arange(QUAD)[None, :] // HEAD_DIM
    head = (lane == rows // c).astype(np.float32)
    sbd = _np_ones_bd(QUAD, HEAD_DIM)
    return strict, incl, eye, bd, tri, head, sbd


def _stack_heads(x, head):
    return jnp.concatenate([x.astype(BF16)] * 4, axis=0) * head


def _rec_kernel(c, first_chunk, n_doubling, group, r_ref, lw_ref, k_ref, v_ref, a_ref, b_ref, s0_ref,
                strict_ref, incl_ref, eye_ref, bd_ref, tri_ref, head_ref, sbd_ref,
                y_ref, sT_ref, s_ref, ry_scr, yc_scr, tr_scr, cs_scr, wl_scr):
    t_len = r_ref.shape[0]
    n_iter = t_len // c - first_chunk
    if first_chunk > 0:
        y_ref[0:first_chunk * c, :] = jnp.zeros((first_chunk * c, QUAD), F32)

    def to_bd(p_row):
        return jnp.concatenate([p_row.astype(BF16)] * 4, axis=0) * bd_ref[...]

    def chunk_rows(ci):
        return pl.ds(pl.multiple_of(ci * c, c), c)

    def prepare(gi, carry):
        strict = strict_ref[...]
        incl2 = jnp.concatenate([incl_ref[...]] * 2, axis=1)
        head = head_ref[...]
        slots = [gi * group + j for j in range(group)]
        loaded = []
        for slot in slots:
            rows = chunk_rows(slot + first_chunk)
            loaded.append((r_ref[rows, :], lw_ref[rows, :], k_ref[rows, :], v_ref[rows, :],
                           a_ref[rows, :], b_ref[rows, :]))
        cums = [_dot_x(tri_ref[...], ld[1], nb=3) for ld in loaded]
        work = []
        for slot, (r, lw, k, v, a, b), cum in zip(slots, loaded, cums):
            cum_last = cum[c - 1:c, :]
            e_inv = jnp.exp(-cum)
            at = (a * jnp.exp(cum - lw)).astype(BF16)
            rt = r * jnp.exp(cum)
            e_last = jnp.exp(cum_last - cum)
            wl_scr[slot] = jnp.broadcast_to(jnp.exp(cum_last), (8, QUAD))
            rhs = jnp.concatenate([_stack_heads(b * e_inv, head), _stack_heads(k * e_inv, head)], axis=0)
            gmat = _dot_nt(jnp.concatenate([at, rt.astype(BF16)], axis=0), rhs)
            work.append(dict(at=at, rt=rt, v=v, bc=(b * e_last).astype(BF16), kc=(k * e_last).astype(BF16),
                             gmat=gmat))
        p_rows, t_rows = [], []
        for w in work:
            gmat = w.pop("gmat")
            n_row = gmat[0:c, 0:4 * c] * strict
            w["v_st"] = _stack_heads(w["v"], head)
            w["xv"] = _dot(gmat[0:c, 4 * c:8 * c] * strict, w["v_st"])
            w["n_r"] = (gmat[c:2 * c, :] * incl2).astype(BF16)
            p_rows.append(n_row)
            t_rows.append(eye_ref[...] + n_row)
        for _ in range(n_doubling):
            t_bds = [to_bd(t) for t in t_rows]
            p_rows = [_dot(p, to_bd(p)) for p in p_rows]
            t_rows = [t + _dot(p, t_bd) for t, p, t_bd in zip(t_rows, p_rows, t_bds)]
        for slot, w, t_row in zip(slots, work, t_rows):
            both = jnp.concatenate([_stack_heads(w["at"], head), _stack_heads(w["xv"], head)], axis=1)
            t_both = _dot(t_row, both)
            ta, txv = t_both[:, 0:QUAD], t_both[:, QUAD:]
            n_rb = w["n_r"][:, 0:4 * c]
            ry_scr[slot] = (w["rt"] + _dot(n_rb, _stack_heads(ta, head))).astype(BF16)
            yc_scr[slot] = _dot(w["n_r"], jnp.concatenate([_stack_heads(txv, head), w["v_st"]],
                                                          axis=0))
            tr_scr[slot] = (_dot_tn(ta, w["bc"]) * sbd_ref[...]).astype(BF16)
            cs_scr[slot] = _dot_tn(jnp.concatenate([txv, w["v"]], axis=0),
                                   jnp.concatenate([w["bc"], w["kc"]], axis=0)) * sbd_ref[...]
        return carry

    lax.fori_loop(0, n_iter // group, prepare, 0)

    s_ref[...] = s0_ref[...]

    def advance(slot, carry):
        rows = chunk_rows(slot + first_chunk)
        s_old = s_ref[...]
        s_bf = s_old.astype(BF16)
        y_ref[rows, :] = _dot_nt(ry_scr[slot], s_bf) + yc_scr[slot]
        s_ref[...] = s_old * wl_scr[slot][0:1, :] + _dot(s_bf, tr_scr[slot]) + cs_scr[slot]
        return carry

    lax.fori_loop(0, n_iter, advance, 0)
    sT_ref[...] = s_ref[...]


def _rec(arrs, s0, c, first_chunk):
    bsz, t_len, _ = arrs[0].shape
    strict, incl, eye, bd, tri, head, sbd = (jnp.asarray(m) for m in _rec_masks(c))
    masks = [strict, incl, eye, bd.astype(BF16), tri, head.astype(BF16), sbd]
    n_doubling = max(int(np.ceil(np.log2(c))) - 1, 0)
    n_iter = t_len // c - first_chunk
    group = next(g for g in (11, 3, 1) if n_iter % g == 0)
    seq = pl.BlockSpec((None, t_len, QUAD), lambda b, q: (b, 0, q))
    st = pl.BlockSpec((None, None, QUAD, QUAD), lambda b, q: (b, q, 0, 0))
    return pl.pallas_call(
        functools.partial(_rec_kernel, c, first_chunk, n_doubling, group),
        grid=(bsz, 2),
        in_specs=[seq] * 6 + [st] + [_resident(m.shape) for m in masks],
        out_specs=[seq, st],
        out_shape=[jax.ShapeDtypeStruct((bsz, t_len, WIDTH), F32),
                   jax.ShapeDtypeStruct((bsz, 2, QUAD, QUAD), F32)],
        scratch_shapes=[pltpu.VMEM((QUAD, QUAD), F32),
                        pltpu.VMEM((n_iter, c, QUAD), BF16), pltpu.VMEM((n_iter, c, QUAD), F32),
                        pltpu.VMEM((n_iter, QUAD, QUAD), BF16), pltpu.VMEM((n_iter, QUAD, QUAD), F32),
                        pltpu.VMEM((n_iter, 8, QUAD), F32)],
        compiler_params=_params(("parallel", "parallel")),
        name="rwkv_rec",
    )(*arrs, s0, *masks)


def _foxprep_kernel(n_pad, with_cumsum, q_ref, k_ref, f_ref, qn_ref, kn_ref, bf_ref, ones_bd, tri_ref,
                    qo_ref, ko_ref, lf_ref, *cum_refs):
    q = q_ref[...]
    k = k_ref[...]
    inv = 1.0 / HEAD_DIM
    qo_ref[...] = (q * lax.rsqrt(_head_sum(q * q, ones_bd[...]) * inv + NORM_EPS) * qn_ref[...]
                   * (HEAD_DIM ** -0.5))
    ko_ref[...] = k * lax.rsqrt(_head_sum(k * k, ones_bd[...]) * inv + NORM_EPS) * kn_ref[...]
    x = f_ref[...] + bf_ref[...]
    lf = jnp.minimum(x, 0.0) - jnp.log(1.0 + jnp.exp(-jnp.abs(x)))
    lane = lax.broadcasted_iota(jnp.int32, lf.shape, 1)
    lf = jnp.where(lane < N_HEADS, lf, 0.0)
    if with_cumsum:
        row = lax.broadcasted_iota(jnp.int32, lf.shape, 0)
        lf = jnp.where(row >= n_pad, lf, 0.0)
    lf_ref[...] = lf
    if with_cumsum:
        c_ref, ct_ref = cum_refs
        t_len = lf.shape[0]
        carry = jnp.zeros((1, LANES), F32)
        for blk in range(t_len // LANES):
            rows = slice(blk * LANES, (blk + 1) * LANES)
            cb = _dot_x(tri_ref[...], lf[rows, :], nb=3) + carry
            c_ref[rows, :] = cb
            ct_ref[:, rows] = cb.T[0:N_HEADS, :]
            carry = cb[LANES - 1:LANES, :]


def _foxprep(zq, zf, qn_w, kn_w, bf_row, ones_bd, rows_per_step, n_steps, n_pad, with_cumsum):
    m = zq.shape[0]
    tm = rows_per_step
    tri = jnp.asarray(np.tril(np.ones((LANES, LANES), np.float32)))
    col = lambda j: pl.BlockSpec((tm, WIDTH), lambda i: (i, j))
    row = lambda n: pl.BlockSpec((tm, n), lambda i: (i, 0))
    out_specs = [row(WIDTH), row(WIDTH), row(LANES)]
    out_shape = [jax.ShapeDtypeStruct((m, WIDTH), F32), jax.ShapeDtypeStruct((m, WIDTH), F32),
                 jax.ShapeDtypeStruct((m, LANES), F32)]
    if with_cumsum:
        out_specs += [row(LANES), pl.BlockSpec((None, N_HEADS, tm), lambda i: (i, 0, 0))]
        out_shape += [jax.ShapeDtypeStruct((m, LANES), F32),
                      jax.ShapeDtypeStruct((n_steps, N_HEADS, tm), F32)]
    return pl.pallas_call(
        functools.partial(_foxprep_kernel, n_pad, with_cumsum),
        grid=(n_steps,),
        in_specs=[col(0), col(1), row(LANES), _resident((1, WIDTH)), _resident((1, WIDTH)),
                  _resident((1, LANES)), _resident((WIDTH, WIDTH)), _resident((LANES, LANES))],
        out_specs=out_specs,
        out_shape=out_shape,
        compiler_params=_params(("parallel",)),
        name="fox_prep",
    )(zq, zq, zf, qn_w, kn_w, bf_row, ones_bd, tri)


def _foxattn_kernel(n_pad, n_qtiles, q_ref, k_ref, v_ref, c_ref, ct_ref, o_ref):
    t_len = q_ref.shape[0]
    tq = t_len // n_qtiles
    pair = pl.program_id(1)
    lane = lax.broadcasted_iota(jnp.int32, (1, LANES), 1)
    for qi in range(n_qtiles):
        nk = (qi + 1) * tq
        qrows = slice(qi * tq, nk)
        q = q_ref[qrows, :]
        k = k_ref[0:nk, :].astype(BF16)
        v = v_ref[0:nk, :].astype(BF16)
        qpos = lax.broadcasted_iota(jnp.int32, (tq, nk), 0) + qi * tq
        kpos = lax.broadcasted_iota(jnp.int32, (tq, nk), 1)
        visible = (kpos <= qpos) & (kpos >= n_pad)
        out = jnp.zeros((tq, LANES), F32)
        for h in range(2):
            hmask = (lane // HEAD_DIM == h).astype(F32)
            head = 2 * pair + h
            sel = (lane == head).astype(F32)
            cq = jnp.sum(c_ref[qrows, :] * sel, axis=-1, keepdims=True)
            ck = ct_ref[pl.ds(head, 1), 0:nk]
            s = _dot_nt(q * hmask, k) + (cq - ck)
            s = jnp.where(visible, s, NEG_INF)
            m = jnp.max(s, axis=-1, keepdims=True)
            p = jnp.exp(s - m)
            l = jnp.sum(p, axis=-1, keepdims=True)
            p = p / l
            out = out + jnp.dot(p.astype(BF16), v, preferred_element_type=F32) * hmask
        o_ref[qrows, :] = out


def _foxattn(qn, kn, zq, cum, cum_t, bsz, t_len, n_pad, n_qtiles):
    seq = lambda j0: pl.BlockSpec((t_len, LANES), lambda b, p: (b, j0 + p))
    return pl.pallas_call(
        functools.partial(_foxattn_kernel, n_pad, n_qtiles),
        grid=(bsz, N_HEADS // 2),
        in_specs=[seq(0), seq(0), seq(2 * WIDTH // LANES),
                  pl.BlockSpec((t_len, LANES), lambda b, p: (b, 0)),
                  pl.BlockSpec((None, N_HEADS, t_len), lambda b, p: (b, 0, 0))],
        out_specs=seq(0),
        out_shape=jax.ShapeDtypeStruct((bsz * t_len, WIDTH), F32),
        compiler_params=_params(("parallel", "parallel")),
        name="fox_attn",
    )(qn, kn, zq, cum, cum_t)


def _dec_consts():
    lane = np.arange(LANES)
    j, h = lane // N_HEADS, lane % N_HEADS
    same_head = h[:, None] == h[None, :]
    msuf = (same_head & (j[:, None] > j[None, :])).astype(np.float32)
    mtot = same_head.astype(np.float32)
    tile = ((lane[:, None] == h[None, :]) & (lane[:, None] < N_HEADS)).astype(np.float32)
    rowsuf = (lane[None, :] > np.arange(8)[:, None]) & (lane[None, :] < 8)
    hmask = (np.arange(PAGE * N_HEADS)[None, :] % N_HEADS == np.arange(8)[:, None])
    return msuf, mtot, tile, rowsuf.astype(np.float32), hmask.astype(np.float32)


def _dec_kernel(n_group, pt_ref, q_ref, kn_ref, vn_ref, lfn_ref, msuf_ref, mtot_ref, tile_ref, rowsuf_ref,
                hmask_ref, *rest):
    k_refs = rest[0:n_group]
    v_refs = rest[n_group:2 * n_group]
    lf_refs = rest[2 * n_group:3 * n_group]
    o_ref = rest[3 * n_group]
    m_ref, l_ref, acc_ref, carry_ref, lfl_ref = rest[3 * n_group + 1:]
    j = pl.program_id(1)
    q = q_ref[...]

    @pl.when(j == 0)
    def _():
        m_ref[...] = jnp.broadcast_to(jnp.sum(q * kn_ref[...], axis=-1, keepdims=True), (8, LANES))
        l_ref[...] = jnp.ones((8, LANES), F32)
        acc_ref[...] = vn_ref[...]
        carry_ref[...] = jnp.zeros((8, LANES), F32)
        lfl_ref[...] = _dot_x(jnp.broadcast_to(lfn_ref[...], (8, LANES)), tile_ref[...], na=3)

    carry = carry_ref[...]
    base = lfl_ref[...]
    qb = q.astype(BF16)
    hmask = hmask_ref[...] > 0.5
    zeros_pad = jnp.zeros((LANES - 8, LANES), F32)
    scores = []
    for g in range(n_group):
        dense = lf_refs[g][...]
        in_row = _dot_x(dense, msuf_ref[...], na=3)
        row_tot = _dot_x(dense, mtot_ref[...], na=3)
        later_rows = _dot_x(rowsuf_ref[...], jnp.concatenate([row_tot, zeros_pad], axis=0), nb=3)
        bias = base + carry + in_row + later_rows
        carry = carry + jnp.sum(row_tot, axis=0, keepdims=True)
        bias_wide = jnp.concatenate(
            [jnp.broadcast_to(bias[i:i + 1, :], (8, LANES)) for i in range(8)], axis=1)
        kf = k_refs[g][...].reshape(PAGE * N_HEADS, HEAD_DIM)
        s = _dot_nt(qb, kf) + bias_wide
        scores.append(jnp.where(hmask, s, NEG_INF))
    carry_ref[...] = carry
    s_all = jnp.concatenate(scores, axis=1)
    m_old = m_ref[:, 0:1]
    m_new = jnp.maximum(m_old, jnp.max(s_all, axis=-1, keepdims=True))
    alpha = jnp.exp(m_old - m_new)
    p = jnp.exp(s_all - m_new)
    l_ref[...] = jnp.broadcast_to(alpha * l_ref[:, 0:1] + jnp.sum(p, axis=-1, keepdims=True), (8, LANES))
    m_ref[...] = jnp.broadcast_to(m_new, (8, LANES))
    acc = alpha * acc_ref[...]
    pb = p.astype(BF16)
    width = PAGE * N_HEADS
    for g in range(n_group):
        vf = v_refs[g][...].reshape(width, HEAD_DIM)
        acc = acc + jnp.dot(pb[:, g * width:(g + 1) * width], vf.astype(BF16), preferred_element_type=F32)
    acc_ref[...] = acc

    @pl.when(j == pl.num_programs(1) - 1)
    def _():
        o_ref[...] = acc_ref[...] / l_ref[:, 0:1]


def _dec(page_table, qn, kn, vn, lfn, cache_k, cache_v, cache_lf, n_group):
    bsz, n_pages = page_table.shape
    n_phys = cache_k.shape[0]
    clf = cache_lf.reshape(n_phys, 8, LANES)
    consts = [jnp.asarray(c) for c in _dec_consts()]
    n_steps = n_pages // n_group
    heads = pl.BlockSpec((None, N_HEADS, HEAD_DIM), lambda b, j, pt: (b, 0, 0))
    const = lambda c: pl.BlockSpec(c.shape, lambda b, j, pt: (0,) * c.ndim, pipeline_mode=pl.Buffered(1))

    def page(shape, g):
        def index(b, j, pt):
            return (pt[b * n_pages + (n_pages - 1 - (j * n_group + g))],) + (0,) * len(shape)
        return pl.BlockSpec((None,) + shape, index)

    grid_spec = pltpu.PrefetchScalarGridSpec(
        num_scalar_prefetch=1,
        grid=(bsz, n_steps),
        in_specs=([heads, heads, heads, pl.BlockSpec((None, 1, LANES), lambda b, j, pt: (b, 0, 0))]
                  + [const(c) for c in consts]
                  + [page((PAGE, N_HEADS, HEAD_DIM), g) for g in range(n_group)]
                  + [page((PAGE, N_HEADS, HEAD_DIM), g) for g in range(n_group)]
                  + [page((8, LANES), g) for g in range(n_group)]),
        out_specs=heads,
        scratch_shapes=[pltpu.VMEM((8, LANES), F32), pltpu.VMEM((8, LANES), F32),
                        pltpu.VMEM((8, HEAD_DIM), F32), pltpu.VMEM((8, LANES), F32),
                        pltpu.VMEM((8, LANES), F32)],
    )
    hd = lambda x: x.reshape(bsz, N_HEADS, HEAD_DIM)
    return pl.pallas_call(
        functools.partial(_dec_kernel, n_group),
        grid_spec=grid_spec,
        out_shape=jax.ShapeDtypeStruct((bsz, N_HEADS, HEAD_DIM), F32),
        compiler_params=_params(("parallel", "arbitrary")),
        name="fox_decode",
    )(page_table.reshape(-1), hd(qn), hd(kn), hd(vn), lfn.reshape(bsz, 1, LANES), *consts,
      *([cache_k] * n_group), *([cache_v] * n_group), *([clf] * n_group)).reshape(bsz, WIDTH)


def _merge_kernel(x_ref, y_ref, bonus_ref, g_ref, fox_ref, zg_ref, gng_ref, gnb_ref, ones_bd,
                  wa_ref, wb_ref, wo_ref, o_ref):
    y = y_ref[...]
    inv = 1.0 / HEAD_DIM
    mean = _head_sum(y, ones_bd[...]) * inv
    d = y - mean
    var = _head_sum(d * d, ones_bd[...]) * inv
    yn = d * lax.rsqrt(var + GN_EPS) * gng_ref[...] + gnb_ref[...]
    rw = (yn + bonus_ref[...]) * g_ref[...]
    zg = zg_ref[...]
    merged = (_sigmoid(zg[:, 0:D_MODEL]) * _dot(rw, wa_ref[...])
              + _sigmoid(zg[:, D_MODEL:]) * _dot(fox_ref[...], wb_ref[...]))
    o_ref[...] = x_ref[...] + _dot(merged, wo_ref[...])


def _merge(x1, y, bonus, g, fox, zg, gn_g, gn_b, ones_bd, w_a, w_b, w_o, tm):
    m = x1.shape[0]
    row = lambda n: pl.BlockSpec((tm, n), lambda i: (i, 0))
    return pl.pallas_call(
        _merge_kernel,
        grid=(m // tm,),
        in_specs=[row(D_MODEL), row(WIDTH), row(WIDTH), row(WIDTH), row(WIDTH), row(2 * D_MODEL),
                  _resident((1, WIDTH)), _resident((1, WIDTH)), _resident((WIDTH, WIDTH)),
                  _resident((WIDTH, D_MODEL)), _resident((WIDTH, D_MODEL)), _resident((D_MODEL, D_MODEL))],
        out_specs=row(D_MODEL),
        out_shape=jax.ShapeDtypeStruct((m, D_MODEL), F32),
        compiler_params=_params(("parallel",)),
        name="merge",
    )(x1, y, bonus, g, fox, zg, gn_g, gn_b, ones_bd, w_a, w_b, w_o)


def _to_blockdiag(state):
    bsz = state.shape[0]
    s = state.reshape(bsz, 2, 4, HEAD_DIM, HEAD_DIM)
    eye = jnp.eye(4, dtype=state.dtype)
    out = jnp.einsum("bqhvk,hg->bqhvgk", s, eye)
    return out.reshape(bsz, 2, QUAD, QUAD)


def _from_blockdiag(sbd):
    bsz = sbd.shape[0]
    s = sbd.reshape(bsz, 2, 4, HEAD_DIM, 4, HEAD_DIM)
    idx = jnp.arange(4)
    s = s[:, :, idx, :, idx, :]
    return jnp.moveaxis(s, 0, 2).reshape(bsz, N_HEADS, HEAD_DIM, HEAD_DIM)


def _layer_weights(l, ffn1_norm, ffn1_w_gate, ffn1_w_up, ffn1_w_down, mix_norm, w_in, mu_shift, w0,
                   w_decay_up, a0, w_aaa_up, w_gate_up, k_k, k_a, r_k, gn_g, gn_b, q_norm, k_norm, b_f,
                   w_a, w_b, w_o, ffn2_norm, ffn2_w_gate, ffn2_w_up, ffn2_w_down):
    row = lambda x: x.reshape(1, -1).astype(F32)
    bf = lambda x: x.astype(BF16)
    wi = w_in[l]
    w_f = jnp.pad(wi[:, R_COLS + QKV_COLS:R_COLS + QKV_COLS + N_HEADS], ((0, 0), (0, LANES - N_HEADS)))
    return dict(
        ffn1=(row(ffn1_norm[l]), bf(ffn1_w_gate[l]), bf(ffn1_w_up[l]), bf(ffn1_w_down[l])),
        ffn2=(row(ffn2_norm[l]), bf(ffn2_w_gate[l]), bf(ffn2_w_up[l]), bf(ffn2_w_down[l])),
        mix_norm=row(mix_norm[l]),
        w_r=bf(wi[:, 0:R_COLS]), w_q=bf(wi[:, R_COLS:R_COLS + QKV_COLS]), w_f=bf(w_f),
        w_g=bf(wi[:, R_COLS + QKV_COLS + N_HEADS:]),
        prep=dict(mu=row(mu_shift[l]), w0=row(w0[l]), wdu=bf(w_decay_up[l]), a0=row(a0[l]),
                  wau=bf(w_aaa_up[l]), wgu=bf(w_gate_up[l]), k_k=row(k_k[l]), k_a=row(k_a[l]),
                  r_k=row(r_k[l]), ones_bd=jnp.asarray(_np_ones_bd(WIDTH, HEAD_DIM), BF16)),
        gn_g=row(gn_g[l]), gn_b=row(gn_b[l]),
        q_norm=row(jnp.tile(q_norm[l], N_HEADS)), k_norm=row(jnp.tile(k_norm[l], N_HEADS)),
        b_f=jnp.pad(row(b_f[l]), ((0, 0), (0, LANES - N_HEADS))),
        w_a=bf(w_a[l]), w_b=bf(w_b[l]), w_o=bf(w_o[l]),
    )


def _prompt_layer(x, w, bsz, t_len, n_pad):
    ones_bd = w["prep"]["ones_bd"]
    x1 = _ffn(x, *w["ffn1"], tm=544)
    zr, zq, zf, zg = _proj(x1, w["mix_norm"], w["w_r"], w["w_q"], w["w_f"], w["w_g"], tm=272)
    r, lw, k, v, a, b, g, bonus = _prep(zr, None, w["prep"], tm=544, seq_len=t_len)
    seq = lambda arr: arr.reshape(bsz, t_len, WIDTH)
    s0 = jnp.zeros((bsz, 2, QUAD, QUAD), F32)
    y, s_t = _rec([seq(r), seq(lw), seq(k), seq(v), seq(a), seq(b)], s0, c=64, first_chunk=n_pad // 64)
    qn, kn, lf, cum, cum_t = _foxprep(zq, zf, w["q_norm"], w["k_norm"], w["b_f"], ones_bd,
                                      rows_per_step=t_len, n_steps=bsz, n_pad=n_pad, with_cumsum=True)
    fox = _foxattn(qn, kn, zq, cum, cum_t, bsz, t_len, n_pad, n_qtiles=4)
    x2 = _merge(x1, y.reshape(bsz * t_len, WIDTH), bonus, g, fox, zg, w["gn_g"], w["gn_b"], ones_bd,
                w["w_a"], w["w_b"], w["w_o"], tm=272)
    x3 = _ffn(x2, *w["ffn2"], tm=544)
    return x3, zr, zq, kn, lf, s_t


def _sample_layer(x, w, shift_prev, wkv0, page_table, cache_k, cache_v, cache_lf):
    bsz = x.shape[0]
    ones_bd = w["prep"]["ones_bd"]
    x1 = _ffn(x, *w["ffn1"], tm=bsz)
    zr, zq, zf, zg = _proj(x1, w["mix_norm"], w["w_r"], w["w_q"], w["w_f"], w["w_g"], tm=bsz)
    prepped = _prep(zr, shift_prev, w["prep"], tm=bsz, seq_len=1)
    g, bonus = prepped[6], prepped[7]
    c = 32
    pad = lambda arr: jnp.pad(arr.reshape(bsz, 1, WIDTH), ((0, 0), (0, c - 1), (0, 0)))
    y, s_t = _rec([pad(arr) for arr in prepped[:6]], _to_blockdiag(wkv0.astype(F32)), c=c, first_chunk=0)
    y = y[:, 0, :]
    qn, kn, lf = _foxprep(zq, zf, w["q_norm"], w["k_norm"], w["b_f"], ones_bd,
                          rows_per_step=bsz, n_steps=1, n_pad=0, with_cumsum=False)
    vn = zq[:, 2 * WIDTH:]
    fox = _dec(page_table, qn, kn, vn, lf, cache_k, cache_v, cache_lf, n_group=16)
    x2 = _merge(x1, y, bonus, g, fox, zg, w["gn_g"], w["gn_b"], ones_bd, w["w_a"], w["w_b"], w["w_o"],
                tm=bsz)
    x3 = _ffn(x2, *w["ffn2"], tm=bsz)
    return x3, zr, kn, vn, lf, s_t


def kernel(x_prompt, x_sample, cache_k, cache_v, cache_logf, state_wkv, state_shift, page_table, meta_tokens, ffn1_norm, ffn1_w_gate, ffn1_w_up, ffn1_w_down, mix_norm, w_in, mu_shift, w0, w_decay_up, a0, w_aaa_up, w_gate_up, k_k, k_a, r_k, gn_g, gn_b, q_norm, k_norm, b_f, w_a, w_b, w_o, ffn2_norm, ffn2_w_gate, ffn2_w_up, ffn2_w_down):
    depth = w_in.shape[0]
    bp, seq, _ = x_prompt.shape
    bs = x_sample.shape[0]
    l_tok = seq + N_META
    n_pad = (-l_tok) % LANES
    t_len = l_tok + n_pad
    meta = jnp.broadcast_to(meta_tokens[None].astype(F32), (bp, N_META, D_MODEL))
    h_p = jnp.concatenate([jnp.zeros((bp, n_pad, D_MODEL), F32), meta, x_prompt.astype(F32)], axis=1)
    h_p = h_p.reshape(bp * t_len, D_MODEL)
    h_s = x_sample.reshape(bs, D_MODEL).astype(F32)
    rows_p, rows_s = [], []
    for l in range(depth):
        w = _layer_weights(l, ffn1_norm, ffn1_w_gate, ffn1_w_up, ffn1_w_down, mix_norm, w_in, mu_shift, w0,
                           w_decay_up, a0, w_aaa_up, w_gate_up, k_k, k_a, r_k, gn_g, gn_b, q_norm, k_norm,
                           b_f, w_a, w_b, w_o, ffn2_norm, ffn2_w_gate, ffn2_w_up, ffn2_w_down)
        h_p, zr, zq, kn, lf, s_t = _prompt_layer(h_p, w, bp, t_len, n_pad)
        real = lambda arr, n: arr.reshape(bp, t_len, n)[:, n_pad:]
        rows_p.append((real(kn, WIDTH).reshape(bp, l_tok, N_HEADS, HEAD_DIM),
                       real(zq, QKV_COLS)[:, :, 2 * WIDTH:].reshape(bp, l_tok, N_HEADS, HEAD_DIM),
                       real(lf, LANES)[:, :, :N_HEADS],
                       _from_blockdiag(s_t),
                       zr.reshape(bp, t_len, R_COLS)[:, -1]))
        h_s, zr_s, kn_s, vn_s, lf_s, s_ts = _sample_layer(
            h_s, w, state_shift[l].astype(F32), state_wkv[l], page_table, cache_k[l], cache_v[l],
            cache_logf[l])
        rows_s.append((kn_s.reshape(bs, 1, N_HEADS, HEAD_DIM), vn_s.reshape(bs, 1, N_HEADS, HEAD_DIM),
                       lf_s[:, :N_HEADS].reshape(bs, 1, N_HEADS), _from_blockdiag(s_ts), zr_s))
    y_prompt = h_p.reshape(bp, t_len, D_MODEL)[:, n_pad + N_META:]
    y_sample = h_s.reshape(bs, 1, D_MODEL)
    stk = lambda rows, i: jnp.stack([r[i] for r in rows], axis=0)
    return (y_prompt, y_sample,
            stk(rows_p, 0), stk(rows_p, 1), stk(rows_p, 2), stk(rows_p, 3), stk(rows_p, 4),
            stk(rows_s, 0), stk(rows_s, 1), stk(rows_s, 2), stk(rows_s, 3), stk(rows_s, 4))
```

```python
import functools

import numpy as np
import jax
import jax.numpy as jnp
from jax import lax
from jax.experimental import pallas as pl
from jax.experimental.pallas import tpu as pltpu

F32 = jnp.float32
BF16 = jnp.bfloat16

D_MODEL = 1024
HEAD_DIM = 64
N_HEADS = 8
WIDTH = N_HEADS * HEAD_DIM
N_META = 16
DECAY_LORA = 64
AAA_LORA = 64
GATE_LORA = 160
FFN_DIM = 2816
R_COLS = 3 * WIDTH + DECAY_LORA + AAA_LORA + GATE_LORA
QKV_COLS = 3 * WIDTH
NORM_EPS = 1e-6
GN_EPS = HEAD_DIM * 1e-5
NEG_INF = -1e30
PAGE = 128
LANES = 128
QUAD = 4 * HEAD_DIM
FFN_CHUNK = 256
VMEM_LIMIT = 56 * 1024 * 1024


def _dot(a, b):
    return jnp.dot(a.astype(BF16), b.astype(BF16), preferred_element_type=F32)


def _dot_nt(a, b):
    return lax.dot_general(a.astype(BF16), b.astype(BF16), (((1,), (1,)), ((), ())),
                           preferred_element_type=F32)


def _dot_tn(a, b):
    return lax.dot_general(a.astype(BF16), b.astype(BF16), (((0,), (0,)), ((), ())),
                           preferred_element_type=F32)


def _split(x, n):
    parts = []
    r = x
    for i in range(n):
        p = r.astype(BF16)
        parts.append(p)
        if i + 1 < n:
            r = r - p.astype(F32)
    return parts


def _dot_x(a, b, dot=_dot, na=1, nb=1):
    pa = _split(a, na) if na > 1 else [a]
    pb = _split(b, nb) if nb > 1 else [b]
    out = None
    for i, x in enumerate(pa):
        for j, y in enumerate(pb):
            if i + j >= max(na, nb):
                continue
            t = dot(x, y)
            out = t if out is None else out + t
    return out


def _sigmoid(x):
    return 1.0 / (1.0 + jnp.exp(-x))


def _rms(x, g):
    ms = jnp.mean(x * x, axis=-1, keepdims=True)
    return x * lax.rsqrt(ms + NORM_EPS) * g


def _head_sum(x, ones_bd):
    return _dot_x(x, ones_bd, na=2)


def _params(sem, vmem=VMEM_LIMIT):
    return pltpu.CompilerParams(dimension_semantics=sem, vmem_limit_bytes=vmem)


def _resident(shape):
    nd = len(shape)
    return pl.BlockSpec(shape, lambda *_: (0,) * nd, pipeline_mode=pl.Buffered(1))


def _np_ones_bd(n, blk):
    i = np.arange(n)
    return (i[:, None] // blk == i[None, :] // blk).astype(np.float32)


def _ffn_kernel(x_ref, g_ref, wg_ref, wu_ref, wd_ref, o_ref, acc_ref):
    x = x_ref[...]
    xn = _rms(x, g_ref[...]).astype(BF16)
    for c in range(FFN_DIM // FFN_CHUNK):
        sl = slice(c * FFN_CHUNK, (c + 1) * FFN_CHUNK)
        gate = jnp.dot(xn, wg_ref[:, sl], preferred_element_type=F32)
        up = jnp.dot(xn, wu_ref[:, sl], preferred_element_type=F32)
        h = (gate * _sigmoid(gate) * up).astype(BF16)
        part = jnp.dot(h, wd_ref[sl, :], preferred_element_type=F32)
        if c == 0:
            acc_ref[...] = part
        else:
            acc_ref[...] += part
    o_ref[...] = x + 0.5 * acc_ref[...]


def _ffn(x, norm_g, w_gate, w_up, w_down, tm):
    m = x.shape[0]
    return pl.pallas_call(
        _ffn_kernel,
        grid=(m // tm,),
        in_specs=[
            pl.BlockSpec((tm, D_MODEL), lambda i: (i, 0)),
            _resident((1, D_MODEL)),
            _resident((D_MODEL, FFN_DIM)),
            _resident((D_MODEL, FFN_DIM)),
            _resident((FFN_DIM, D_MODEL)),
        ],
        out_specs=pl.BlockSpec((tm, D_MODEL), lambda i: (i, 0)),
        out_shape=jax.ShapeDtypeStruct((m, D_MODEL), F32),
        scratch_shapes=[pltpu.VMEM((tm, D_MODEL), F32)],
        compiler_params=_params(("parallel",)),
        name="ffn",
    )(x, norm_g, w_gate, w_up, w_down)


def _proj_kernel(x_ref, g_ref, wr_ref, wq_ref, wf_ref, wg_ref, zr_ref, zq_ref, zf_ref, zg_ref):
    xn = _rms(x_ref[...], g_ref[...]).astype(BF16)
    zr_ref[...] = jnp.dot(xn, wr_ref[...], preferred_element_type=F32)
    zq_ref[...] = jnp.dot(xn, wq_ref[...], preferred_element_type=F32)
    zf_ref[...] = jnp.dot(xn, wf_ref[...], preferred_element_type=F32)
    zg_ref[...] = jnp.dot(xn, wg_ref[...], preferred_element_type=F32)


def _proj(x, norm_g, w_r, w_q, w_f, w_g, tm):
    m = x.shape[0]
    row = lambda n: pl.BlockSpec((tm, n), lambda i: (i, 0))
    return pl.pallas_call(
        _proj_kernel,
        grid=(m // tm,),
        in_specs=[row(D_MODEL), _resident((1, D_MODEL)), _resident((D_MODEL, R_COLS)),
                  _resident((D_MODEL, QKV_COLS)), _resident((D_MODEL, LANES)),
                  _resident((D_MODEL, 2 * D_MODEL))],
        out_specs=[row(R_COLS), row(QKV_COLS), row(LANES), row(2 * D_MODEL)],
        out_shape=[jax.ShapeDtypeStruct((m, R_COLS), F32),
                   jax.ShapeDtypeStruct((m, QKV_COLS), F32),
                   jax.ShapeDtypeStruct((m, LANES), F32),
                   jax.ShapeDtypeStruct((m, 2 * D_MODEL), F32)],
        compiler_params=_params(("parallel",)),
        name="in_proj",
    )(x, norm_g, w_r, w_q, w_f, w_g)


def _prep_math(z, zprev, mu, w0, wdu, a0, wau, wgu, k_k, k_a, r_k, ones_bd, outs):
    r_ref, lw_ref, k_ref, v_ref, a_ref, b_ref, g_ref, bonus_ref = outs
    zs = z + (zprev - z) * mu
    r = zs[:, 0:WIDTH]
    k = zs[:, WIDTH:2 * WIDTH]
    v = zs[:, 2 * WIDTH:3 * WIDTH]
    o = 3 * WIDTH
    d_lo = zs[:, o:o + DECAY_LORA]
    a_lo = zs[:, o + DECAY_LORA:o + DECAY_LORA + AAA_LORA]
    g_lo = zs[:, o + DECAY_LORA + AAA_LORA:R_COLS]
    wpre = w0 + _dot(jnp.tanh(d_lo), wdu)
    y = -wpre
    softplus = jnp.maximum(y, 0.0) + jnp.log(1.0 + jnp.exp(-jnp.abs(y)))
    w_log = -softplus - 0.5
    lw_ref[...] = -jnp.exp(w_log)
    a = _sigmoid(a0 + _dot(a_lo, wau))
    g_ref[...] = _dot(_sigmoid(g_lo), wgu)
    kk = k * k_k
    k2 = k * (1.0 + (a - 1.0) * k_a)
    kk = kk * lax.rsqrt(_head_sum(kk * kk, ones_bd) + 1e-12)
    r_ref[...] = r
    k_ref[...] = k2
    v_ref[...] = v
    a_ref[...] = -kk
    b_ref[...] = kk * a
    bonus_ref[...] = _head_sum(r * k2 * r_k, ones_bd) * v


def _prep_prompt_kernel(seq_len, z_ref, zp_ref, mu, w0, wdu, a0, wau, wgu, k_k, k_a, r_k, ones_bd, *outs):
    z = z_ref[...]
    tm = z.shape[0]
    rolled = pltpu.roll(z, 1, 0)
    row = lax.broadcasted_iota(jnp.int32, (tm, 1), 0)
    zprev = jnp.where(row == 0, zp_ref[7:8, :], rolled)
    grow = row + pl.program_id(0) * tm
    zprev = jnp.where(grow % seq_len == 0, 0.0, zprev)
    _prep_math(z, zprev, mu[...], w0[...], wdu[...], a0[...], wau[...], wgu[...], k_k[...], k_a[...],
               r_k[...], ones_bd[...], outs)


def _prep_sample_kernel(z_ref, zp_ref, mu, w0, wdu, a0, wau, wgu, k_k, k_a, r_k, ones_bd, *outs):
    _prep_math(z_ref[...], zp_ref[...], mu[...], w0[...], wdu[...], a0[...], wau[...], wgu[...],
               k_k[...], k_a[...], r_k[...], ones_bd[...], outs)


def _prep(zr, zprev, pw, tm, seq_len):
    m = zr.shape[0]
    row = lambda n: pl.BlockSpec((tm, n), lambda i: (i, 0))
    if zprev is None:
        body = functools.partial(_prep_prompt_kernel, seq_len)
        prev_spec = pl.BlockSpec((8, R_COLS), lambda i: (jnp.maximum(i * (tm // 8) - 1, 0), 0))
        zprev = zr
    else:
        body = _prep_sample_kernel
        prev_spec = row(R_COLS)
    consts = [pw["mu"], pw["w0"], pw["wdu"], pw["a0"], pw["wau"], pw["wgu"], pw["k_k"], pw["k_a"],
              pw["r_k"], pw["ones_bd"]]
    return pl.pallas_call(
        body,
        grid=(m // tm,),
        in_specs=[row(R_COLS), prev_spec] + [_resident(c.shape) for c in consts],
        out_specs=[row(WIDTH)] * 8,
        out_shape=[jax.ShapeDtypeStruct((m, WIDTH), F32)] * 8,
        compiler_params=_params(("parallel",)),
        name="rwkv_prep",
    )(zr, zprev, *consts)


def _rec_masks(c):
    t = np.arange(c)[:, None]
    hs = np.arange(4 * c)[None, :]
    s = hs % c
    strict = (s < t).astype(np.float32)
    incl = (s <= t).astype(np.float32)
    eye = (s == t).astype(np.float32)
    rows = np.arange(4 * c)[:, None]
    bd = (rows // c == hs // c).astype(np.float32)
    tri = (np.arange(c)[None, :] <= np.arange(c)[:, None]).astype(np.float32)
    lane = np.arange(QUAD)[None, :] // HEAD_DIM
    head = (lane == rows // c).astype(np.float32)
    sbd = _np_ones_bd(QUAD, HEAD_DIM)
    return strict, incl, eye, bd, tri, head, sbd


def _stack_heads(x, head):
    return jnp.concatenate([x.astype(BF16)] * 4, axis=0) * head


def _rec_kernel(c, first_chunk, n_doubling, group, r_ref, lw_ref, k_ref, v_ref, a_ref, b_ref, s0_ref,
                strict_ref, incl_ref, eye_ref, bd_ref, tri_ref, head_ref, sbd_ref,
                y_ref, sT_ref, s_ref, ry_scr, yc_scr, tr_scr, cs_scr, wl_scr):
    t_len = r_ref.shape[0]
    n_iter = t_len // c - first_chunk
    if first_chunk > 0:
        y_ref[0:first_chunk * c, :] = jnp.zeros((first_chunk * c, QUAD), F32)

    def to_bd(p_row):
        return jnp.concatenate([p_row.astype(BF16)] * 4, axis=0) * bd_ref[...]

    def chunk_rows(ci):
        return pl.ds(pl.multiple_of(ci * c, c), c)

    def prepare(gi, carry):
        strict = strict_ref[...]
        incl2 = jnp.concatenate([incl_ref[...]] * 2, axis=1)
        head = head_ref[...]
        slots = [gi * group + j for j in range(group)]
        loaded = []
        for slot in slots:
            rows = chunk_rows(slot + first_chunk)
            loaded.append((r_ref[rows, :], lw_ref[rows, :], k_ref[rows, :], v_ref[rows, :],
                           a_ref[rows, :], b_ref[rows, :]))
        cums = [_dot_x(tri_ref[...], ld[1], nb=3) for ld in loaded]
        work = []
        for slot, (r, lw, k, v, a, b), cum in zip(slots, loaded, cums):
            cum_last = cum[c - 1:c, :]
            e_inv = jnp.exp(-cum)
            at = (a * jnp.exp(cum - lw)).astype(BF16)
            rt = r * jnp.exp(cum)
            e_last = jnp.exp(cum_last - cum)
            wl_scr[slot] = jnp.broadcast_to(jnp.exp(cum_last), (8, QUAD))
            rhs = jnp.concatenate([_stack_heads(b * e_inv, head), _stack_heads(k * e_inv, head)], axis=0)
            gmat = _dot_nt(jnp.concatenate([at, rt.astype(BF16)], axis=0), rhs)
            work.append(dict(at=at, rt=rt, v=v, bc=(b * e_last).astype(BF16), kc=(k * e_last).astype(BF16),
                             gmat=gmat))
        p_rows, t_rows = [], []
        for w in work:
            gmat = w.pop("gmat")
            n_row = gmat[0:c, 0:4 * c] * strict
            w["v_st"] = _stack_heads(w["v"], head)
            w["xv"] = _dot(gmat[0:c, 4 * c:8 * c] * strict, w["v_st"])
            w["n_r"] = (gmat[c:2 * c, :] * incl2).astype(BF16)
            p_rows.append(n_row)
            t_rows.append(eye_ref[...] + n_row)
        for _ in range(n_doubling):
            t_bds = [to_bd(t) for t in t_rows]
            p_rows = [_dot(p, to_bd(p)) for p in p_rows]
            t_rows = [t + _dot(p, t_bd) for t, p, t_bd in zip(t_rows, p_rows, t_bds)]
        for slot, w, t_row in zip(slots, work, t_rows):
            both = jnp.concatenate([_stack_heads(w["at"], head), _stack_heads(w["xv"], head)], axis=1)
            t_both = _dot(t_row, both)
            ta, txv = t_both[:, 0:QUAD], t_both[:, QUAD:]
            n_rb = w["n_r"][:, 0:4 * c]
            ry_scr[slot] = (w["rt"] + _dot(n_rb, _stack_heads(ta, head))).astype(BF16)
            yc_scr[slot] = _dot(w["n_r"], jnp.concatenate([_stack_heads(txv, head), w["v_st"]],
                                                          axis=0))
            tr_scr[slot] = (_dot_tn(ta, w["bc"]) * sbd_ref[...]).astype(BF16)
            cs_scr[slot] = _dot_tn(jnp.concatenate([txv, w["v"]], axis=0),
                                   jnp.concatenate([w["bc"], w["kc"]], axis=0)) * sbd_ref[...]
        return carry

    lax.fori_loop(0, n_iter // group, prepare, 0)

    s_ref[...] = s0_ref[...]

    def advance(slot, carry):
        rows = chunk_rows(slot + first_chunk)
        s_old = s_ref[...]
        s_bf = s_old.astype(BF16)
        y_ref[rows, :] = _dot_nt(ry_scr[slot], s_bf) + yc_scr[slot]
        s_ref[...] = s_old * wl_scr[slot][0:1, :] + _dot(s_bf, tr_scr[slot]) + cs_scr[slot]
        return carry

    lax.fori_loop(0, n_iter, advance, 0)
    sT_ref[...] = s_ref[...]


def _rec(arrs, s0, c, first_chunk):
    bsz, t_len, _ = arrs[0].shape
    strict, incl, eye, bd, tri, head, sbd = (jnp.asarray(m) for m in _rec_masks(c))
    masks = [strict, incl, eye, bd.astype(BF16), tri, head.astype(BF16), sbd]
    n_doubling = max(int(np.ceil(np.log2(c))) - 1, 0)
    n_iter = t_len // c - first_chunk
    group = next(g for g in (11, 3, 1) if n_iter % g == 0)
    seq = pl.BlockSpec((None, t_len, QUAD), lambda b, q: (b, 0, q))
    st = pl.BlockSpec((None, None, QUAD, QUAD), lambda b, q: (b, q, 0, 0))
    return pl.pallas_call(
        functools.partial(_rec_kernel, c, first_chunk, n_doubling, group),
        grid=(bsz, 2),
        in_specs=[seq] * 6 + [st] + [_resident(m.shape) for m in masks],
        out_specs=[seq, st],
        out_shape=[jax.ShapeDtypeStruct((bsz, t_len, WIDTH), F32),
                   jax.ShapeDtypeStruct((bsz, 2, QUAD, QUAD), F32)],
        scratch_shapes=[pltpu.VMEM((QUAD, QUAD), F32),
                        pltpu.VMEM((n_iter, c, QUAD), BF16), pltpu.VMEM((n_iter, c, QUAD), F32),
                        pltpu.VMEM((n_iter, QUAD, QUAD), BF16), pltpu.VMEM((n_iter, QUAD, QUAD), F32),
                        pltpu.VMEM((n_iter, 8, QUAD), F32)],
        compiler_params=_params(("parallel", "parallel")),
        name="rwkv_rec",
    )(*arrs, s0, *masks)


def _foxprep_kernel(n_pad, with_cumsum, q_ref, k_ref, f_ref, qn_ref, kn_ref, bf_ref, ones_bd, tri_ref,
                    qo_ref, ko_ref, lf_ref, *cum_refs):
    q = q_ref[...]
    k = k_ref[...]
    inv = 1.0 / HEAD_DIM
    qo_ref[...] = (q * lax.rsqrt(_head_sum(q * q, ones_bd[...]) * inv + NORM_EPS) * qn_ref[...]
                   * (HEAD_DIM ** -0.5))
    ko_ref[...] = k * lax.rsqrt(_head_sum(k * k, ones_bd[...]) * inv + NORM_EPS) * kn_ref[...]
    x = f_ref[...] + bf_ref[...]
    lf = jnp.minimum(x, 0.0) - jnp.log(1.0 + jnp.exp(-jnp.abs(x)))
    lane = lax.broadcasted_iota(jnp.int32, lf.shape, 1)
    lf = jnp.where(lane < N_HEADS, lf, 0.0)
    if with_cumsum:
        row = lax.broadcasted_iota(jnp.int32, lf.shape, 0)
        lf = jnp.where(row >= n_pad, lf, 0.0)
    lf_ref[...] = lf
    if with_cumsum:
        (c_ref,) = cum_refs
        t_len = lf.shape[0]
        carry = jnp.zeros((1, LANES), F32)
        for blk in range(t_len // LANES):
            rows = slice(blk * LANES, (blk + 1) * LANES)
            cb = _dot_x(tri_ref[...], lf[rows, :], nb=3) + carry
            c_ref[rows, :] = cb
            carry = cb[LANES - 1:LANES, :]


def _foxprep(zq, zf, qn_w, kn_w, bf_row, ones_bd, rows_per_step, n_steps, n_pad, with_cumsum):
    m = zq.shape[0]
    tm = rows_per_step
    tri = jnp.asarray(np.tril(np.ones((LANES, LANES), np.float32)))
    col = lambda j: pl.BlockSpec((tm, WIDTH), lambda i: (i, j))
    row = lambda n: pl.BlockSpec((tm, n), lambda i: (i, 0))
    out_specs = [row(WIDTH), row(WIDTH), row(LANES)]
    out_shape = [jax.ShapeDtypeStruct((m, WIDTH), F32), jax.ShapeDtypeStruct((m, WIDTH), F32),
                 jax.ShapeDtypeStruct((m, LANES), F32)]
    if with_cumsum:
        out_specs += [row(LANES)]
        out_shape += [jax.ShapeDtypeStruct((m, LANES), F32)]
    return pl.pallas_call(
        functools.partial(_foxprep_kernel, n_pad, with_cumsum),
        grid=(n_steps,),
        in_specs=[col(0), col(1), row(LANES), _resident((1, WIDTH)), _resident((1, WIDTH)),
                  _resident((1, LANES)), _resident((WIDTH, WIDTH)), _resident((LANES, LANES))],
        out_specs=out_specs,
        out_shape=out_shape,
        compiler_params=_params(("parallel",)),
        name="fox_prep",
    )(zq, zq, zf, qn_w, kn_w, bf_row, ones_bd, tri)


def _foxattn_kernel(n_pad, n_qtiles, q_ref, k_ref, v_ref, c_ref, o_ref, qa_ref, ka_ref):
    t_len = q_ref.shape[0]
    tq = t_len // n_qtiles
    pair = pl.program_id(1)
    lane = lax.broadcasted_iota(jnp.int32, (1, LANES), 1)
    row = lax.broadcasted_iota(jnp.int32, (t_len, 1), 0)
    causal = (lax.broadcasted_iota(jnp.int32, (tq, tq), 1) <= lax.broadcasted_iota(jnp.int32, (tq, tq), 0))
    for h in range(2):
        hmask = (lane // HEAD_DIM == h).astype(F32)
        sel = (lane == 2 * pair + h).astype(F32)
        c = jnp.sum(c_ref[...] * sel, axis=-1, keepdims=True)
        hi = c.astype(BF16).astype(F32)
        mid = (c - hi).astype(BF16).astype(F32)
        lo = (c - hi - mid).astype(BF16).astype(F32)
        base = HEAD_DIM * (1 - h)
        at = lambda i: (lane == base + i).astype(F32)
        ones_q = at(3) + at(4) + at(5)
        ones_k = at(0) + at(1) + at(2)
        hi_k = jnp.where(row >= n_pad, hi, -NEG_INF)
        qa_ref[...] = (q_ref[...] * hmask + hi * at(0) + mid * at(1) + lo * at(2) + ones_q).astype(BF16)
        ka_ref[...] = (k_ref[...] * hmask + ones_k - hi_k * at(3) - mid * at(4) - lo * at(5)).astype(BF16)
        for qi in range(n_qtiles):
            q0 = qi * tq
            qa = qa_ref[q0:q0 + tq, :]
            s_diag = jnp.where(causal, _dot_nt(qa, ka_ref[q0:q0 + tq, :]), NEG_INF)
            m = jnp.max(s_diag, axis=-1, keepdims=True)
            if qi > 0:
                s_past = _dot_nt(qa, ka_ref[0:q0, :])
                m = jnp.maximum(m, jnp.max(s_past, axis=-1, keepdims=True))
            p_diag = jnp.exp(s_diag - m)
            l = jnp.sum(p_diag, axis=-1, keepdims=True)
            out = _dot(p_diag, v_ref[q0:q0 + tq, :])
            if qi > 0:
                p_past = jnp.exp(s_past - m)
                l = l + jnp.sum(p_past, axis=-1, keepdims=True)
                out = out + _dot(p_past, v_ref[0:q0, :])
            out = out / l * hmask
            if h == 0:
                o_ref[q0:q0 + tq, :] = out
            else:
                o_ref[q0:q0 + tq, :] += out


def _foxattn(qn, kn, zq, cum, bsz, t_len, n_pad, n_qtiles):
    seq = lambda j0: pl.BlockSpec((t_len, LANES), lambda b, p: (b, j0 + p))
    return pl.pallas_call(
        functools.partial(_foxattn_kernel, n_pad, n_qtiles),
        grid=(bsz, N_HEADS // 2),
        in_specs=[seq(0), seq(0), seq(2 * WIDTH // LANES),
                  pl.BlockSpec((t_len, LANES), lambda b, p: (b, 0))],
        out_specs=seq(0),
        out_shape=jax.ShapeDtypeStruct((bsz * t_len, WIDTH), F32),
        scratch_shapes=[pltpu.VMEM((t_len, LANES), BF16), pltpu.VMEM((t_len, LANES), BF16)],
        compiler_params=_params(("parallel", "parallel")),
        name="fox_attn",
    )(qn, kn, zq, cum)


def _dec_consts():
    t = np.arange(PAGE)
    suf = (t[:, None] > t[None, :]).astype(np.float32)
    bdm = (np.arange(WIDTH)[None, :] // HEAD_DIM == np.arange(8)[:, None]).astype(np.float32)
    eye = (np.arange(LANES)[None, :] == np.arange(8)[:, None]).astype(np.float32)
    return suf, bdm, eye


def _dec_kernel(n_group, pt_ref, q_ref, kn_ref, vn_ref, lfn_ref, suf_ref, bdm_ref, eye_ref, *rest):
    k_refs = rest[0:n_group]
    v_refs = rest[n_group:2 * n_group]
    lf_refs = rest[2 * n_group:3 * n_group]
    o_ref = rest[3 * n_group]
    m_ref, l_ref, acc_ref, carry_ref = rest[3 * n_group + 1:]
    j = pl.program_id(1)
    bdm = bdm_ref[...]
    qbd = q_ref[...] * bdm
    lf_new = jnp.sum(eye_ref[...] * lfn_ref[...], axis=-1, keepdims=True)

    @pl.when(j == 0)
    def _():
        m_ref[...] = jnp.broadcast_to(jnp.sum(qbd * kn_ref[...], axis=-1, keepdims=True), (8, LANES))
        l_ref[...] = jnp.ones((8, LANES), F32)
        acc_ref[...] = vn_ref[...] * bdm
        carry_ref[...] = jnp.zeros((8, LANES), F32)

    carry = carry_ref[:, 0:1]
    qb = qbd.astype(BF16)
    scores = []
    for g in range(n_group):
        lf_t = lf_refs[g][...]
        suffix = _dot_x(lf_t, suf_ref[...], na=3)
        kt = k_refs[g][...].reshape(WIDTH, PAGE)
        scores.append(_dot(qb, kt) + (lf_new + carry + suffix))
        carry = carry + jnp.sum(lf_t, axis=-1, keepdims=True)
    carry_ref[...] = jnp.broadcast_to(carry, (8, LANES))
    s_all = jnp.concatenate(scores, axis=1)
    m_old = m_ref[:, 0:1]
    m_new = jnp.maximum(m_old, jnp.max(s_all, axis=-1, keepdims=True))
    alpha = jnp.exp(m_old - m_new)
    p = jnp.exp(s_all - m_new)
    l_ref[...] = jnp.broadcast_to(alpha * l_ref[:, 0:1] + jnp.sum(p, axis=-1, keepdims=True), (8, LANES))
    m_ref[...] = jnp.broadcast_to(m_new, (8, LANES))
    acc = alpha * acc_ref[...]
    pb = p.astype(BF16)
    for g in range(n_group):
        vt = v_refs[g][...].reshape(WIDTH, PAGE)
        acc = acc + _dot_nt(pb[:, g * PAGE:(g + 1) * PAGE], vt)
    acc_ref[...] = acc

    @pl.when(j == pl.num_programs(1) - 1)
    def _():
        o_ref[...] = jnp.sum(acc_ref[...] * bdm / l_ref[:, 0:1], axis=0, keepdims=True)


def _dec(page_table, qn, kn, vn, lfn, cache_k, cache_v, cache_lf, n_group):
    bsz, n_pages = page_table.shape
    kt = jnp.transpose(cache_k, (0, 2, 3, 1))
    vt = jnp.transpose(cache_v, (0, 2, 3, 1))
    lft = jnp.transpose(cache_lf, (0, 2, 1))
    consts = [jnp.asarray(c) for c in _dec_consts()]
    n_steps = n_pages // n_group
    tok = lambda n: pl.BlockSpec((None, 1, n), lambda b, j, pt: (b, 0, 0))
    const = lambda c: pl.BlockSpec(c.shape, lambda b, j, pt: (0,) * c.ndim, pipeline_mode=pl.Buffered(1))

    def page(shape, g):
        def index(b, j, pt):
            return (pt[b * n_pages + (n_pages - 1 - (j * n_group + g))],) + (0,) * len(shape)
        return pl.BlockSpec((None,) + shape, index)

    grid_spec = pltpu.PrefetchScalarGridSpec(
        num_scalar_prefetch=1,
        grid=(bsz, n_steps),
        in_specs=([tok(WIDTH), tok(WIDTH), tok(WIDTH), tok(LANES)] + [const(c) for c in consts]
                  + [page((N_HEADS, HEAD_DIM, PAGE), g) for g in range(n_group)]
                  + [page((N_HEADS, HEAD_DIM, PAGE), g) for g in range(n_group)]
                  + [page((N_HEADS, PAGE), g) for g in range(n_group)]),
        out_specs=tok(WIDTH),
        scratch_shapes=[pltpu.VMEM((8, LANES), F32), pltpu.VMEM((8, LANES), F32),
                        pltpu.VMEM((8, WIDTH), F32), pltpu.VMEM((8, LANES), F32)],
    )
    r3 = lambda x: x.reshape(bsz, 1, x.shape[-1])
    return pl.pallas_call(
        functools.partial(_dec_kernel, n_group),
        grid_spec=grid_spec,
        out_shape=jax.ShapeDtypeStruct((bsz, 1, WIDTH), F32),
        compiler_params=_params(("parallel", "arbitrary")),
        name="fox_decode",
    )(page_table.reshape(-1), r3(qn), r3(kn), r3(vn), r3(lfn), *consts,
      *([kt] * n_group), *([vt] * n_group), *([lft] * n_group)).reshape(bsz, WIDTH)


def _merge_kernel(x_ref, y_ref, bonus_ref, g_ref, fox_ref, zg_ref, gng_ref, gnb_ref, ones_bd,
                  wa_ref, wb_ref, wo_ref, o_ref):
    y = y_ref[...]
    inv = 1.0 / HEAD_DIM
    mean = _head_sum(y, ones_bd[...]) * inv
    d = y - mean
    var = _head_sum(d * d, ones_bd[...]) * inv
    yn = d * lax.rsqrt(var + GN_EPS) * gng_ref[...] + gnb_ref[...]
    rw = (yn + bonus_ref[...]) * g_ref[...]
    zg = zg_ref[...]
    merged = (_sigmoid(zg[:, 0:D_MODEL]) * _dot(rw, wa_ref[...])
              + _sigmoid(zg[:, D_MODEL:]) * _dot(fox_ref[...], wb_ref[...]))
    o_ref[...] = x_ref[...] + _dot(merged, wo_ref[...])


def _merge(x1, y, bonus, g, fox, zg, gn_g, gn_b, ones_bd, w_a, w_b, w_o, tm):
    m = x1.shape[0]
    row = lambda n: pl.BlockSpec((tm, n), lambda i: (i, 0))
    return pl.pallas_call(
        _merge_kernel,
        grid=(m // tm,),
        in_specs=[row(D_MODEL), row(WIDTH), row(WIDTH), row(WIDTH), row(WIDTH), row(2 * D_MODEL),
                  _resident((1, WIDTH)), _resident((1, WIDTH)), _resident((WIDTH, WIDTH)),
                  _resident((WIDTH, D_MODEL)), _resident((WIDTH, D_MODEL)), _resident((D_MODEL, D_MODEL))],
        out_specs=row(D_MODEL),
        out_shape=jax.ShapeDtypeStruct((m, D_MODEL), F32),
        compiler_params=_params(("parallel",)),
        name="merge",
    )(x1, y, bonus, g, fox, zg, gn_g, gn_b, ones_bd, w_a, w_b, w_o)


def _to_blockdiag(state):
    bsz = state.shape[0]
    s = state.reshape(bsz, 2, 4, HEAD_DIM, HEAD_DIM)
    eye = jnp.eye(4, dtype=state.dtype)
    out = jnp.einsum("bqhvk,hg->bqhvgk", s, eye)
    return out.reshape(bsz, 2, QUAD, QUAD)


def _from_blockdiag(sbd):
    bsz = sbd.shape[0]
    s = sbd.reshape(bsz, 2, 4, HEAD_DIM, 4, HEAD_DIM)
    idx = jnp.arange(4)
    s = s[:, :, idx, :, idx, :]
    return jnp.moveaxis(s, 0, 2).reshape(bsz, N_HEADS, HEAD_DIM, HEAD_DIM)


def _layer_weights(l, ffn1_norm, ffn1_w_gate, ffn1_w_up, ffn1_w_down, mix_norm, w_in, mu_shift, w0,
                   w_decay_up, a0, w_aaa_up, w_gate_up, k_k, k_a, r_k, gn_g, gn_b, q_norm, k_norm, b_f,
                   w_a, w_b, w_o, ffn2_norm, ffn2_w_gate, ffn2_w_up, ffn2_w_down):
    row = lambda x: x.reshape(1, -1).astype(F32)
    bf = lambda x: x.astype(BF16)
    wi = w_in[l]
    w_f = jnp.pad(wi[:, R_COLS + QKV_COLS:R_COLS + QKV_COLS + N_HEADS], ((0, 0), (0, LANES - N_HEADS)))
    return dict(
        ffn1=(row(ffn1_norm[l]), bf(ffn1_w_gate[l]), bf(ffn1_w_up[l]), bf(ffn1_w_down[l])),
        ffn2=(row(ffn2_norm[l]), bf(ffn2_w_gate[l]), bf(ffn2_w_up[l]), bf(ffn2_w_down[l])),
        mix_norm=row(mix_norm[l]),
        w_r=bf(wi[:, 0:R_COLS]), w_q=bf(wi[:, R_COLS:R_COLS + QKV_COLS]), w_f=bf(w_f),
        w_g=bf(wi[:, R_COLS + QKV_COLS + N_HEADS:]),
        prep=dict(mu=row(mu_shift[l]), w0=row(w0[l]), wdu=bf(w_decay_up[l]), a0=row(a0[l]),
                  wau=bf(w_aaa_up[l]), wgu=bf(w_gate_up[l]), k_k=row(k_k[l]), k_a=row(k_a[l]),
                  r_k=row(r_k[l]), ones_bd=jnp.asarray(_np_ones_bd(WIDTH, HEAD_DIM), BF16)),
        gn_g=row(gn_g[l]), gn_b=row(gn_b[l]),
        q_norm=row(jnp.tile(q_norm[l], N_HEADS)), k_norm=row(jnp.tile(k_norm[l], N_HEADS)),
        b_f=jnp.pad(row(b_f[l]), ((0, 0), (0, LANES - N_HEADS))),
        w_a=bf(w_a[l]), w_b=bf(w_b[l]), w_o=bf(w_o[l]),
    )


def _prompt_layer(x, w, bsz, t_len, n_pad):
    ones_bd = w["prep"]["ones_bd"]
    x1 = _ffn(x, *w["ffn1"], tm=544)
    zr, zq, zf, zg = _proj(x1, w["mix_norm"], w["w_r"], w["w_q"], w["w_f"], w["w_g"], tm=272)
    r, lw, k, v, a, b, g, bonus = _prep(zr, None, w["prep"], tm=544, seq_len=t_len)
    seq = lambda arr: arr.reshape(bsz, t_len, WIDTH)
    s0 = jnp.zeros((bsz, 2, QUAD, QUAD), F32)
    y, s_t = _rec([seq(r), seq(lw), seq(k), seq(v), seq(a), seq(b)], s0, c=64, first_chunk=n_pad // 64)
    qn, kn, lf, cum = _foxprep(zq, zf, w["q_norm"], w["k_norm"], w["b_f"], ones_bd,
                                      rows_per_step=t_len, n_steps=bsz, n_pad=n_pad, with_cumsum=True)
    fox = _foxattn(qn, kn, zq, cum, bsz, t_len, n_pad, n_qtiles=4)
    x2 = _merge(x1, y.reshape(bsz * t_len, WIDTH), bonus, g, fox, zg, w["gn_g"], w["gn_b"], ones_bd,
                w["w_a"], w["w_b"], w["w_o"], tm=272)
    x3 = _ffn(x2, *w["ffn2"], tm=544)
    return x3, zr, zq, kn, lf, s_t


def _sample_layer(x, w, shift_prev, wkv0, page_table, cache_k, cache_v, cache_lf):
    bsz = x.shape[0]
    ones_bd = w["prep"]["ones_bd"]
    x1 = _ffn(x, *w["ffn1"], tm=bsz)
    zr, zq, zf, zg = _proj(x1, w["mix_norm"], w["w_r"], w["w_q"], w["w_f"], w["w_g"], tm=bsz)
    prepped = _prep(zr, shift_prev, w["prep"], tm=bsz, seq_len=1)
    g, bonus = prepped[6], prepped[7]
    c = 32
    pad = lambda arr: jnp.pad(arr.reshape(bsz, 1, WIDTH), ((0, 0), (0, c - 1), (0, 0)))
    y, s_t = _rec([pad(arr) for arr in prepped[:6]], _to_blockdiag(wkv0.astype(F32)), c=c, first_chunk=0)
    y = y[:, 0, :]
    qn, kn, lf = _foxprep(zq, zf, w["q_norm"], w["k_norm"], w["b_f"], ones_bd,
                          rows_per_step=bsz, n_steps=1, n_pad=0, with_cumsum=False)
    vn = zq[:, 2 * WIDTH:]
    fox = _dec(page_table, qn, kn, vn, lf, cache_k, cache_v, cache_lf, n_group=16)
    x2 = _merge(x1, y, bonus, g, fox, zg, w["gn_g"], w["gn_b"], ones_bd, w["w_a"], w["w_b"], w["w_o"],
                tm=bsz)
    x3 = _ffn(x2, *w["ffn2"], tm=bsz)
    return x3, zr, kn, vn, lf, s_t


def kernel(x_prompt, x_sample, cache_k, cache_v, cache_logf, state_wkv, state_shift, page_table, meta_tokens, ffn1_norm, ffn1_w_gate, ffn1_w_up, ffn1_w_down, mix_norm, w_in, mu_shift, w0, w_decay_up, a0, w_aaa_up, w_gate_up, k_k, k_a, r_k, gn_g, gn_b, q_norm, k_norm, b_f, w_a, w_b, w_o, ffn2_norm, ffn2_w_gate, ffn2_w_up, ffn2_w_down):
    depth = w_in.shape[0]
    bp, seq, _ = x_prompt.shape
    bs = x_sample.shape[0]
    l_tok = seq + N_META
    n_pad = (-l_tok) % LANES
    t_len = l_tok + n_pad
    meta = jnp.broadcast_to(meta_tokens[None].astype(F32), (bp, N_META, D_MODEL))
    h_p = jnp.concatenate([jnp.zeros((bp, n_pad, D_MODEL), F32), meta, x_prompt.astype(F32)], axis=1)
    h_p = h_p.reshape(bp * t_len, D_MODEL)
    h_s = x_sample.reshape(bs, D_MODEL).astype(F32)
    rows_p, rows_s = [], []
    for l in range(depth):
        w = _layer_weights(l, ffn1_norm, ffn1_w_gate, ffn1_w_up, ffn1_w_down, mix_norm, w_in, mu_shift, w0,
                           w_decay_up, a0, w_aaa_up, w_gate_up, k_k, k_a, r_k, gn_g, gn_b, q_norm, k_norm,
                           b_f, w_a, w_b, w_o, ffn2_norm, ffn2_w_gate, ffn2_w_up, ffn2_w_down)
        h_p, zr, zq, kn, lf, s_t = _prompt_layer(h_p, w, bp, t_len, n_pad)
        real = lambda arr, n: arr.reshape(bp, t_len, n)[:, n_pad:]
        rows_p.append((real(kn, WIDTH).reshape(bp, l_tok, N_HEADS, HEAD_DIM),
                       real(zq, QKV_COLS)[:, :, 2 * WIDTH:].reshape(bp, l_tok, N_HEADS, HEAD_DIM),
                       real(lf, LANES)[:, :, :N_HEADS],
                       _from_blockdiag(s_t),
                       zr.reshape(bp, t_len, R_COLS)[:, -1]))
        h_s, zr_s, kn_s, vn_s, lf_s, s_ts = _sample_layer(
            h_s, w, state_shift[l].astype(F32), state_wkv[l], page_table, cache_k[l], cache_v[l],
            cache_logf[l])
        rows_s.append((kn_s.reshape(bs, 1, N_HEADS, HEAD_DIM), vn_s.reshape(bs, 1, N_HEADS, HEAD_DIM),
                       lf_s[:, :N_HEADS].reshape(bs, 1, N_HEADS), _from_blockdiag(s_ts), zr_s))
    y_prompt = h_p.reshape(bp, t_len, D_MODEL)[:, n_pad + N_META:]
    y_sample = h_s.reshape(bs, 1, D_MODEL)
    stk = lambda rows, i: jnp.stack([r[i] for r in rows], axis=0)
    return (y_prompt, y_sample,
            stk(rows_p, 0), stk(rows_p, 1), stk(rows_p, 2), stk(rows_p, 3), stk(rows_p, 4),
            stk(rows_s, 0), stk(rows_s, 1), stk(rows_s, 2), stk(rows_s, 3), stk(rows_s, 4))
```

```python
import functools

import numpy as np
import jax
import jax.numpy as jnp
from jax import lax
from jax.experimental import pallas as pl
from jax.experimental.pallas import tpu as pltpu

F32 = jnp.float32
BF16 = jnp.bfloat16

D_MODEL = 1024
HEAD_DIM = 64
N_HEADS = 8
WIDTH = N_HEADS * HEAD_DIM
N_META = 16
DECAY_LORA = 64
AAA_LORA = 64
GATE_LORA = 160
FFN_DIM = 2816
R_COLS = 3 * WIDTH + DECAY_LORA + AAA_LORA + GATE_LORA
QKV_COLS = 3 * WIDTH
NORM_EPS = 1e-6
GN_EPS = HEAD_DIM * 1e-5
NEG_INF = -1e30
PAGE = 128
LANES = 128
QUAD = 4 * HEAD_DIM
FFN_CHUNK = 256
VMEM_LIMIT = 56 * 1024 * 1024


def _dot(a, b):
    return jnp.dot(a.astype(BF16), b.astype(BF16), preferred_element_type=F32)


def _dot_nt(a, b):
    return lax.dot_general(a.astype(BF16), b.astype(BF16), (((1,), (1,)), ((), ())),
                           preferred_element_type=F32)


def _dot_tn(a, b):
    return lax.dot_general(a.astype(BF16), b.astype(BF16), (((0,), (0,)), ((), ())),
                           preferred_element_type=F32)


def _split(x, n):
    parts = []
    r = x
    for i in range(n):
        p = r.astype(BF16)
        parts.append(p)
        if i + 1 < n:
            r = r - p.astype(F32)
    return parts


def _dot_x(a, b, dot=_dot, na=1, nb=1):
    pa = _split(a, na) if na > 1 else [a]
    pb = _split(b, nb) if nb > 1 else [b]
    out = None
    for i, x in enumerate(pa):
        for j, y in enumerate(pb):
            if i + j >= max(na, nb):
                continue
            t = dot(x, y)
            out = t if out is None else out + t
    return out


def _sigmoid(x):
    return 1.0 / (1.0 + jnp.exp(-x))


def _rms(x, g):
    ms = jnp.mean(x * x, axis=-1, keepdims=True)
    return x * lax.rsqrt(ms + NORM_EPS) * g


def _head_sum(x, ones_bd):
    return _dot_x(x, ones_bd, na=2)


def _params(sem, vmem=VMEM_LIMIT):
    return pltpu.CompilerParams(dimension_semantics=sem, vmem_limit_bytes=vmem)


def _resident(shape):
    nd = len(shape)
    return pl.BlockSpec(shape, lambda *_: (0,) * nd, pipeline_mode=pl.Buffered(1))


def _np_ones_bd(n, blk):
    i = np.arange(n)
    return (i[:, None] // blk == i[None, :] // blk).astype(np.float32)


def _ffn_kernel(x_ref, g_ref, wg_ref, wu_ref, wd_ref, o_ref, acc_ref):
    x = x_ref[...]
    xn = _rms(x, g_ref[...]).astype(BF16)
    for c in range(FFN_DIM // FFN_CHUNK):
        sl = slice(c * FFN_CHUNK, (c + 1) * FFN_CHUNK)
        gate = jnp.dot(xn, wg_ref[:, sl], preferred_element_type=F32)
        up = jnp.dot(xn, wu_ref[:, sl], preferred_element_type=F32)
        h = (gate * _sigmoid(gate) * up).astype(BF16)
        part = jnp.dot(h, wd_ref[sl, :], preferred_element_type=F32)
        if c == 0:
            acc_ref[...] = part
        else:
            acc_ref[...] += part
    o_ref[...] = x + 0.5 * acc_ref[...]


def _ffn(x, norm_g, w_gate, w_up, w_down, tm):
    m = x.shape[0]
    return pl.pallas_call(
        _ffn_kernel,
        grid=(m // tm,),
        in_specs=[
            pl.BlockSpec((tm, D_MODEL), lambda i: (i, 0)),
            _resident((1, D_MODEL)),
            _resident((D_MODEL, FFN_DIM)),
            _resident((D_MODEL, FFN_DIM)),
            _resident((FFN_DIM, D_MODEL)),
        ],
        out_specs=pl.BlockSpec((tm, D_MODEL), lambda i: (i, 0)),
        out_shape=jax.ShapeDtypeStruct((m, D_MODEL), F32),
        scratch_shapes=[pltpu.VMEM((tm, D_MODEL), F32)],
        compiler_params=_params(("parallel",)),
        name="ffn",
    )(x, norm_g, w_gate, w_up, w_down)


def _proj_kernel(x_ref, g_ref, wr_ref, wq_ref, wf_ref, wg_ref, zr_ref, zq_ref, zf_ref, zg_ref):
    xn = _rms(x_ref[...], g_ref[...]).astype(BF16)
    zr_ref[...] = jnp.dot(xn, wr_ref[...], preferred_element_type=F32)
    zq_ref[...] = jnp.dot(xn, wq_ref[...], preferred_element_type=F32)
    zf_ref[...] = jnp.dot(xn, wf_ref[...], preferred_element_type=F32)
    zg_ref[...] = jnp.dot(xn, wg_ref[...], preferred_element_type=F32)


def _proj(x, norm_g, w_r, w_q, w_f, w_g, tm):
    m = x.shape[0]
    row = lambda n: pl.BlockSpec((tm, n), lambda i: (i, 0))
    return pl.pallas_call(
        _proj_kernel,
        grid=(m // tm,),
        in_specs=[row(D_MODEL), _resident((1, D_MODEL)), _resident((D_MODEL, R_COLS)),
                  _resident((D_MODEL, QKV_COLS)), _resident((D_MODEL, LANES)),
                  _resident((D_MODEL, 2 * D_MODEL))],
        out_specs=[row(R_COLS), row(QKV_COLS), row(LANES), row(2 * D_MODEL)],
        out_shape=[jax.ShapeDtypeStruct((m, R_COLS), F32),
                   jax.ShapeDtypeStruct((m, QKV_COLS), F32),
                   jax.ShapeDtypeStruct((m, LANES), F32),
                   jax.ShapeDtypeStruct((m, 2 * D_MODEL), F32)],
        compiler_params=_params(("parallel",)),
        name="in_proj",
    )(x, norm_g, w_r, w_q, w_f, w_g)


def _prep_math(z, zprev, mu, w0, wdu, a0, wau, wgu, k_k, k_a, r_k, ones_bd, outs):
    r_ref, lw_ref, k_ref, v_ref, a_ref, b_ref, g_ref, bonus_ref = outs
    zs = z + (zprev - z) * mu
    r = zs[:, 0:WIDTH]
    k = zs[:, WIDTH:2 * WIDTH]
    v = zs[:, 2 * WIDTH:3 * WIDTH]
    o = 3 * WIDTH
    d_lo = zs[:, o:o + DECAY_LORA]
    a_lo = zs[:, o + DECAY_LORA:o + DECAY_LORA + AAA_LORA]
    g_lo = zs[:, o + DECAY_LORA + AAA_LORA:R_COLS]
    wpre = w0 + _dot(jnp.tanh(d_lo), wdu)
    y = -wpre
    softplus = jnp.maximum(y, 0.0) + jnp.log(1.0 + jnp.exp(-jnp.abs(y)))
    w_log = -softplus - 0.5
    lw_ref[...] = -jnp.exp(w_log)
    a = _sigmoid(a0 + _dot(a_lo, wau))
    g_ref[...] = _dot(_sigmoid(g_lo), wgu)
    kk = k * k_k
    k2 = k * (1.0 + (a - 1.0) * k_a)
    kk = kk * lax.rsqrt(_head_sum(kk * kk, ones_bd) + 1e-12)
    r_ref[...] = r
    k_ref[...] = k2
    v_ref[...] = v
    a_ref[...] = -kk
    b_ref[...] = kk * a
    bonus_ref[...] = _head_sum(r * k2 * r_k, ones_bd) * v


def _prep_prompt_kernel(seq_len, z_ref, zp_ref, mu, w0, wdu, a0, wau, wgu, k_k, k_a, r_k, ones_bd, *outs):
    z = z_ref[...]
    tm = z.shape[0]
    rolled = pltpu.roll(z, 1, 0)
    row = lax.broadcasted_iota(jnp.int32, (tm, 1), 0)
    zprev = jnp.where(row == 0, zp_ref[7:8, :], rolled)
    grow = row + pl.program_id(0) * tm
    zprev = jnp.where(grow % seq_len == 0, 0.0, zprev)
    _prep_math(z, zprev, mu[...], w0[...], wdu[...], a0[...], wau[...], wgu[...], k_k[...], k_a[...],
               r_k[...], ones_bd[...], outs)


def _prep_sample_kernel(z_ref, zp_ref, mu, w0, wdu, a0, wau, wgu, k_k, k_a, r_k, ones_bd, *outs):
    _prep_math(z_ref[...], zp_ref[...], mu[...], w0[...], wdu[...], a0[...], wau[...], wgu[...],
               k_k[...], k_a[...], r_k[...], ones_bd[...], outs)


def _prep(zr, zprev, pw, tm, seq_len):
    m = zr.shape[0]
    row = lambda n: pl.BlockSpec((tm, n), lambda i: (i, 0))
    if zprev is None:
        body = functools.partial(_prep_prompt_kernel, seq_len)
        prev_spec = pl.BlockSpec((8, R_COLS), lambda i: (jnp.maximum(i * (tm // 8) - 1, 0), 0))
        zprev = zr
    else:
        body = _prep_sample_kernel
        prev_spec = row(R_COLS)
    consts = [pw["mu"], pw["w0"], pw["wdu"], pw["a0"], pw["wau"], pw["wgu"], pw["k_k"], pw["k_a"],
              pw["r_k"], pw["ones_bd"]]
    return pl.pallas_call(
        body,
        grid=(m // tm,),
        in_specs=[row(R_COLS), prev_spec] + [_resident(c.shape) for c in consts],
        out_specs=[row(WIDTH)] * 8,
        out_shape=[jax.ShapeDtypeStruct((m, WIDTH), F32)] * 8,
        compiler_params=_params(("parallel",)),
        name="rwkv_prep",
    )(zr, zprev, *consts)


def _rec_masks(c):
    t = np.arange(c)[:, None]
    hs = np.arange(4 * c)[None, :]
    s = hs % c
    strict = (s < t).astype(np.float32)
    incl = (s <= t).astype(np.float32)
    eye = (s == t).astype(np.float32)
    rows = np.arange(4 * c)[:, None]
    bd = (rows // c == hs // c).astype(np.float32)
    tri = (np.arange(c)[None, :] <= np.arange(c)[:, None]).astype(np.float32)
    lane = np.arange(QUAD)[None, :] // HEAD_DIM
    head = (lane == rows // c).astype(np.float32)
    sbd = _np_ones_bd(QUAD, HEAD_DIM)
    return strict, incl, eye, bd, tri, head, sbd


def _stack_heads(x, head):
    return jnp.concatenate([x.astype(BF16)] * 4, axis=0) * head


def _rec_kernel(c, first_chunk, n_doubling, group, r_ref, lw_ref, k_ref, v_ref, a_ref, b_ref, s0_ref,
                strict_ref, incl_ref, eye_ref, bd_ref, tri_ref, head_ref, sbd_ref,
                y_ref, sT_ref, s_ref, ry_scr, yc_scr, tr_scr, cs_scr, wl_scr):
    t_len = r_ref.shape[0]
    n_iter = t_len // c - first_chunk
    if first_chunk > 0:
        y_ref[0:first_chunk * c, :] = jnp.zeros((first_chunk * c, QUAD), F32)

    def to_bd(p_row):
        return jnp.concatenate([p_row.astype(BF16)] * 4, axis=0) * bd_ref[...]

    def chunk_rows(ci):
        return pl.ds(pl.multiple_of(ci * c, c), c)

    def prepare(gi, carry):
        strict = strict_ref[...]
        incl2 = jnp.concatenate([incl_ref[...]] * 2, axis=1)
        head = head_ref[...]
        slots = [gi * group + j for j in range(group)]
        loaded = []
        for slot in slots:
            rows = chunk_rows(slot + first_chunk)
            loaded.append((r_ref[rows, :], lw_ref[rows, :], k_ref[rows, :], v_ref[rows, :],
                           a_ref[rows, :], b_ref[rows, :]))
        cums = [_dot_x(tri_ref[...], ld[1], nb=3) for ld in loaded]
        work = []
        for slot, (r, lw, k, v, a, b), cum in zip(slots, loaded, cums):
            cum_last = cum[c - 1:c, :]
            e_inv = jnp.exp(-cum)
            at = (a * jnp.exp(cum - lw)).astype(BF16)
            rt = r * jnp.exp(cum)
            e_last = jnp.exp(cum_last - cum)
            wl_scr[slot] = jnp.broadcast_to(jnp.exp(cum_last), (8, QUAD))
            rhs = jnp.concatenate([_stack_heads(b * e_inv, head), _stack_heads(k * e_inv, head)], axis=0)
            gmat = _dot_nt(jnp.concatenate([at, rt.astype(BF16)], axis=0), rhs)
            work.append(dict(at=at, rt=rt, v=v, bc=(b * e_last).astype(BF16), kc=(k * e_last).astype(BF16),
                             gmat=gmat))
        p_rows, t_rows = [], []
        for w in work:
            gmat = w.pop("gmat")
            n_row = gmat[0:c, 0:4 * c] * strict
            w["v_st"] = _stack_heads(w["v"], head)
            w["xv"] = _dot(gmat[0:c, 4 * c:8 * c] * strict, w["v_st"])
            w["n_r"] = (gmat[c:2 * c, :] * incl2).astype(BF16)
            p_rows.append(n_row)
            t_rows.append(eye_ref[...] + n_row)
        for _ in range(n_doubling):
            t_bds = [to_bd(t) for t in t_rows]
            p_rows = [_dot(p, to_bd(p)) for p in p_rows]
            t_rows = [t + _dot(p, t_bd) for t, p, t_bd in zip(t_rows, p_rows, t_bds)]
        for slot, w, t_row in zip(slots, work, t_rows):
            both = jnp.concatenate([_stack_heads(w["at"], head), _stack_heads(w["xv"], head)], axis=1)
            t_both = _dot(t_row, both)
            ta, txv = t_both[:, 0:QUAD], t_both[:, QUAD:]
            n_rb = w["n_r"][:, 0:4 * c]
            ry_scr[slot] = (w["rt"] + _dot(n_rb, _stack_heads(ta, head))).astype(BF16)
            yc_scr[slot] = _dot(w["n_r"], jnp.concatenate([_stack_heads(txv, head), w["v_st"]],
                                                          axis=0))
            tr_scr[slot] = (_dot_tn(ta, w["bc"]) * sbd_ref[...]).astype(BF16)
            cs_scr[slot] = _dot_tn(jnp.concatenate([txv, w["v"]], axis=0),
                                   jnp.concatenate([w["bc"], w["kc"]], axis=0)) * sbd_ref[...]
        return carry

    lax.fori_loop(0, n_iter // group, prepare, 0)

    s_ref[...] = s0_ref[...]

    def advance(slot, carry):
        rows = chunk_rows(slot + first_chunk)
        s_old = s_ref[...]
        s_bf = s_old.astype(BF16)
        y_ref[rows, :] = _dot_nt(ry_scr[slot], s_bf) + yc_scr[slot]
        s_ref[...] = s_old * wl_scr[slot][0:1, :] + _dot(s_bf, tr_scr[slot]) + cs_scr[slot]
        return carry

    lax.fori_loop(0, n_iter, advance, 0)
    sT_ref[...] = s_ref[...]


def _rec(arrs, s0, c, first_chunk):
    bsz, t_len, _ = arrs[0].shape
    strict, incl, eye, bd, tri, head, sbd = (jnp.asarray(m) for m in _rec_masks(c))
    masks = [strict, incl, eye, bd.astype(BF16), tri, head.astype(BF16), sbd]
    n_doubling = max(int(np.ceil(np.log2(c))) - 1, 0)
    n_iter = t_len // c - first_chunk
    group = next(g for g in (11, 3, 1) if n_iter % g == 0)
    seq = pl.BlockSpec((None, t_len, QUAD), lambda b, q: (b, 0, q))
    st = pl.BlockSpec((None, None, QUAD, QUAD), lambda b, q: (b, q, 0, 0))
    return pl.pallas_call(
        functools.partial(_rec_kernel, c, first_chunk, n_doubling, group),
        grid=(bsz, 2),
        in_specs=[seq] * 6 + [st] + [_resident(m.shape) for m in masks],
        out_specs=[seq, st],
        out_shape=[jax.ShapeDtypeStruct((bsz, t_len, WIDTH), F32),
                   jax.ShapeDtypeStruct((bsz, 2, QUAD, QUAD), F32)],
        scratch_shapes=[pltpu.VMEM((QUAD, QUAD), F32),
                        pltpu.VMEM((n_iter, c, QUAD), BF16), pltpu.VMEM((n_iter, c, QUAD), F32),
                        pltpu.VMEM((n_iter, QUAD, QUAD), BF16), pltpu.VMEM((n_iter, QUAD, QUAD), F32),
                        pltpu.VMEM((n_iter, 8, QUAD), F32)],
        compiler_params=_params(("parallel", "parallel")),
        name="rwkv_rec",
    )(*arrs, s0, *masks)


def _step_kernel(tok_ref, s_ref, y_ref, so_ref):
    s = s_ref[...]
    r, lw, k, v, a, b = (tok_ref[:, i] for i in range(6))
    eye = (lax.broadcasted_iota(jnp.int32, (HEAD_DIM, HEAD_DIM), 0)
           == lax.broadcasted_iota(jnp.int32, (HEAD_DIM, HEAD_DIM), 1)).astype(F32)
    u = jnp.sum(s * a, axis=-1, keepdims=True)
    v_col = jnp.sum(eye * v, axis=-1, keepdims=True)
    s_new = s * jnp.exp(lw) + u * b + v_col * k
    so_ref[...] = s_new
    y_col = jnp.sum(s_new * r, axis=-1, keepdims=True)
    y_ref[...] = jnp.sum(y_col * eye, axis=-2, keepdims=True)


def _step(tok, state, nb):
    bsz = state.shape[0]
    return pl.pallas_call(
        _step_kernel,
        grid=(bsz // nb,),
        in_specs=[pl.BlockSpec((nb, 6, N_HEADS, 1, HEAD_DIM), lambda i: (i, 0, 0, 0, 0)),
                  pl.BlockSpec((nb, N_HEADS, HEAD_DIM, HEAD_DIM), lambda i: (i, 0, 0, 0))],
        out_specs=[pl.BlockSpec((nb, N_HEADS, 1, HEAD_DIM), lambda i: (i, 0, 0, 0)),
                   pl.BlockSpec((nb, N_HEADS, HEAD_DIM, HEAD_DIM), lambda i: (i, 0, 0, 0))],
        out_shape=[jax.ShapeDtypeStruct((bsz, N_HEADS, 1, HEAD_DIM), F32),
                   jax.ShapeDtypeStruct((bsz, N_HEADS, HEAD_DIM, HEAD_DIM), F32)],
        compiler_params=_params(("parallel",)),
        name="rwkv_step",
    )(tok, state)


def _foxprep_kernel(n_pad, with_cumsum, q_ref, k_ref, f_ref, qn_ref, kn_ref, bf_ref, ones_bd, tri_ref,
                    qo_ref, ko_ref, lf_ref, *cum_refs):
    q = q_ref[...]
    k = k_ref[...]
    inv = 1.0 / HEAD_DIM
    qo_ref[...] = (q * lax.rsqrt(_head_sum(q * q, ones_bd[...]) * inv + NORM_EPS) * qn_ref[...]
                   * (HEAD_DIM ** -0.5))
    ko_ref[...] = k * lax.rsqrt(_head_sum(k * k, ones_bd[...]) * inv + NORM_EPS) * kn_ref[...]
    x = f_ref[...] + bf_ref[...]
    lf = jnp.minimum(x, 0.0) - jnp.log(1.0 + jnp.exp(-jnp.abs(x)))
    lane = lax.broadcasted_iota(jnp.int32, lf.shape, 1)
    lf = jnp.where(lane < N_HEADS, lf, 0.0)
    if with_cumsum:
        row = lax.broadcasted_iota(jnp.int32, lf.shape, 0)
        lf = jnp.where(row >= n_pad, lf, 0.0)
    lf_ref[...] = lf
    if with_cumsum:
        (c_ref,) = cum_refs
        t_len = lf.shape[0]
        carry = jnp.zeros((1, LANES), F32)
        for blk in range(t_len // LANES):
            rows = slice(blk * LANES, (blk + 1) * LANES)
            cb = _dot_x(tri_ref[...], lf[rows, :], nb=3) + carry
            c_ref[rows, :] = cb
            carry = cb[LANES - 1:LANES, :]


def _foxprep(zq, zf, qn_w, kn_w, bf_row, ones_bd, rows_per_step, n_steps, n_pad, with_cumsum):
    m = zq.shape[0]
    tm = rows_per_step
    tri = jnp.asarray(np.tril(np.ones((LANES, LANES), np.float32)))
    col = lambda j: pl.BlockSpec((tm, WIDTH), lambda i: (i, j))
    row = lambda n: pl.BlockSpec((tm, n), lambda i: (i, 0))
    out_specs = [row(WIDTH), row(WIDTH), row(LANES)]
    out_shape = [jax.ShapeDtypeStruct((m, WIDTH), F32), jax.ShapeDtypeStruct((m, WIDTH), F32),
                 jax.ShapeDtypeStruct((m, LANES), F32)]
    if with_cumsum:
        out_specs += [row(LANES)]
        out_shape += [jax.ShapeDtypeStruct((m, LANES), F32)]
    return pl.pallas_call(
        functools.partial(_foxprep_kernel, n_pad, with_cumsum),
        grid=(n_steps,),
        in_specs=[col(0), col(1), row(LANES), _resident((1, WIDTH)), _resident((1, WIDTH)),
                  _resident((1, LANES)), _resident((WIDTH, WIDTH)), _resident((LANES, LANES))],
        out_specs=out_specs,
        out_shape=out_shape,
        compiler_params=_params(("parallel",)),
        name="fox_prep",
    )(zq, zq, zf, qn_w, kn_w, bf_row, ones_bd, tri)


def _foxattn_kernel(n_pad, n_qtiles, q_ref, k_ref, v_ref, c_ref, o_ref, qa_ref, ka_ref):
    t_len = q_ref.shape[0]
    tq = t_len // n_qtiles
    pair = pl.program_id(1)
    lane = lax.broadcasted_iota(jnp.int32, (1, LANES), 1)
    row = lax.broadcasted_iota(jnp.int32, (t_len, 1), 0)
    causal = (lax.broadcasted_iota(jnp.int32, (tq, tq), 1) <= lax.broadcasted_iota(jnp.int32, (tq, tq), 0))
    for h in range(2):
        hmask = (lane // HEAD_DIM == h).astype(F32)
        sel = (lane == 2 * pair + h).astype(F32)
        c = jnp.sum(c_ref[...] * sel, axis=-1, keepdims=True)
        hi = c.astype(BF16).astype(F32)
        mid = (c - hi).astype(BF16).astype(F32)
        lo = (c - hi - mid).astype(BF16).astype(F32)
        base = HEAD_DIM * (1 - h)
        at = lambda i: (lane == base + i).astype(F32)
        ones_q = at(3) + at(4) + at(5)
        ones_k = at(0) + at(1) + at(2)
        hi_k = jnp.where(row >= n_pad, hi, -NEG_INF)
        qa_ref[...] = (q_ref[...] * hmask + hi * at(0) + mid * at(1) + lo * at(2) + ones_q).astype(BF16)
        ka_ref[...] = (k_ref[...] * hmask + ones_k - hi_k * at(3) - mid * at(4) - lo * at(5)).astype(BF16)
        for qi in range(n_qtiles):
            q0 = qi * tq
            qa = qa_ref[q0:q0 + tq, :]
            s_diag = jnp.where(causal, _dot_nt(qa, ka_ref[q0:q0 + tq, :]), NEG_INF)
            m = jnp.max(s_diag, axis=-1, keepdims=True)
            if qi > 0:
                s_past = _dot_nt(qa, ka_ref[0:q0, :])
                m = jnp.maximum(m, jnp.max(s_past, axis=-1, keepdims=True))
            p_diag = jnp.exp(s_diag - m)
            l = jnp.sum(p_diag, axis=-1, keepdims=True)
            out = _dot(p_diag, v_ref[q0:q0 + tq, :])
            if qi > 0:
                p_past = jnp.exp(s_past - m)
                l = l + jnp.sum(p_past, axis=-1, keepdims=True)
                out = out + _dot(p_past, v_ref[0:q0, :])
            out = out / l * hmask
            if h == 0:
                o_ref[q0:q0 + tq, :] = out
            else:
                o_ref[q0:q0 + tq, :] += out


def _foxattn(qn, kn, zq, cum, bsz, t_len, n_pad, n_qtiles):
    seq = lambda j0: pl.BlockSpec((t_len, LANES), lambda b, p: (b, j0 + p))
    return pl.pallas_call(
        functools.partial(_foxattn_kernel, n_pad, n_qtiles),
        grid=(bsz, N_HEADS // 2),
        in_specs=[seq(0), seq(0), seq(2 * WIDTH // LANES),
                  pl.BlockSpec((t_len, LANES), lambda b, p: (b, 0))],
        out_specs=seq(0),
        out_shape=jax.ShapeDtypeStruct((bsz * t_len, WIDTH), F32),
        scratch_shapes=[pltpu.VMEM((t_len, LANES), BF16), pltpu.VMEM((t_len, LANES), BF16)],
        compiler_params=_params(("parallel", "parallel")),
        name="fox_attn",
    )(qn, kn, zq, cum)


def _dec_consts():
    t = np.arange(PAGE)
    suf = (t[:, None] > t[None, :]).astype(np.float32)
    bdm = (np.arange(WIDTH)[None, :] // HEAD_DIM == np.arange(8)[:, None]).astype(np.float32)
    eye = (np.arange(LANES)[None, :] == np.arange(8)[:, None]).astype(np.float32)
    return suf, bdm, eye


def _dec_kernel(n_group, pt_ref, q_ref, kn_ref, vn_ref, lfn_ref, suf_ref, bdm_ref, eye_ref, *rest):
    k_refs = rest[0:n_group]
    v_refs = rest[n_group:2 * n_group]
    lf_refs = rest[2 * n_group:3 * n_group]
    o_ref = rest[3 * n_group]
    m_ref, l_ref, acc_ref, carry_ref = rest[3 * n_group + 1:]
    j = pl.program_id(1)
    bdm = bdm_ref[...]
    qbd = q_ref[...] * bdm
    lf_new = jnp.sum(eye_ref[...] * lfn_ref[...], axis=-1, keepdims=True)

    @pl.when(j == 0)
    def _():
        m_ref[...] = jnp.broadcast_to(jnp.sum(qbd * kn_ref[...], axis=-1, keepdims=True), (8, LANES))
        l_ref[...] = jnp.ones((8, LANES), F32)
        acc_ref[...] = vn_ref[...] * bdm
        carry_ref[...] = jnp.zeros((8, LANES), F32)

    carry = carry_ref[:, 0:1]
    qb = qbd.astype(BF16)
    scores = []
    for g in range(n_group):
        lf_t = lf_refs[g][...]
        suffix = _dot_x(lf_t, suf_ref[...], na=3)
        kt = k_refs[g][...].reshape(WIDTH, PAGE)
        scores.append(_dot(qb, kt) + (lf_new + carry + suffix))
        carry = carry + jnp.sum(lf_t, axis=-1, keepdims=True)
    carry_ref[...] = jnp.broadcast_to(carry, (8, LANES))
    s_all = jnp.concatenate(scores, axis=1)
    m_old = m_ref[:, 0:1]
    m_new = jnp.maximum(m_old, jnp.max(s_all, axis=-1, keepdims=True))
    alpha = jnp.exp(m_old - m_new)
    p = jnp.exp(s_all - m_new)
    l_ref[...] = jnp.broadcast_to(alpha * l_ref[:, 0:1] + jnp.sum(p, axis=-1, keepdims=True), (8, LANES))
    m_ref[...] = jnp.broadcast_to(m_new, (8, LANES))
    acc = alpha * acc_ref[...]
    pb = p.astype(BF16)
    for g in range(n_group):
        vt = v_refs[g][...].reshape(WIDTH, PAGE)
        acc = acc + _dot_nt(pb[:, g * PAGE:(g + 1) * PAGE], vt)
    acc_ref[...] = acc

    @pl.when(j == pl.num_programs(1) - 1)
    def _():
        o_ref[...] = jnp.sum(acc_ref[...] * bdm / l_ref[:, 0:1], axis=0, keepdims=True)


def _dec(page_table, qn, kn, vn, lfn, cache_k, cache_v, cache_lf, n_group):
    bsz, n_pages = page_table.shape
    kt = jnp.transpose(cache_k, (0, 2, 3, 1))
    vt = jnp.transpose(cache_v, (0, 2, 3, 1))
    lft = jnp.transpose(cache_lf, (0, 2, 1))
    consts = [jnp.asarray(c) for c in _dec_consts()]
    n_steps = n_pages // n_group
    tok = lambda n: pl.BlockSpec((None, 1, n), lambda b, j, pt: (b, 0, 0))
    const = lambda c: pl.BlockSpec(c.shape, lambda b, j, pt: (0,) * c.ndim, pipeline_mode=pl.Buffered(1))

    def page(shape, g):
        def index(b, j, pt):
            return (pt[b * n_pages + (n_pages - 1 - (j * n_group + g))],) + (0,) * len(shape)
        return pl.BlockSpec((None,) + shape, index)

    grid_spec = pltpu.PrefetchScalarGridSpec(
        num_scalar_prefetch=1,
        grid=(bsz, n_steps),
        in_specs=([tok(WIDTH), tok(WIDTH), tok(WIDTH), tok(LANES)] + [const(c) for c in consts]
                  + [page((N_HEADS, HEAD_DIM, PAGE), g) for g in range(n_group)]
                  + [page((N_HEADS, HEAD_DIM, PAGE), g) for g in range(n_group)]
                  + [page((N_HEADS, PAGE), g) for g in range(n_group)]),
        out_specs=tok(WIDTH),
        scratch_shapes=[pltpu.VMEM((8, LANES), F32), pltpu.VMEM((8, LANES), F32),
                        pltpu.VMEM((8, WIDTH), F32), pltpu.VMEM((8, LANES), F32)],
    )
    r3 = lambda x: x.reshape(bsz, 1, x.shape[-1])
    return pl.pallas_call(
        functools.partial(_dec_kernel, n_group),
        grid_spec=grid_spec,
        out_shape=jax.ShapeDtypeStruct((bsz, 1, WIDTH), F32),
        compiler_params=_params(("parallel", "arbitrary")),
        name="fox_decode",
    )(page_table.reshape(-1), r3(qn), r3(kn), r3(vn), r3(lfn), *consts,
      *([kt] * n_group), *([vt] * n_group), *([lft] * n_group)).reshape(bsz, WIDTH)


def _merge_kernel(x_ref, y_ref, bonus_ref, g_ref, fox_ref, zg_ref, gng_ref, gnb_ref, ones_bd,
                  wa_ref, wb_ref, wo_ref, o_ref):
    y = y_ref[...]
    inv = 1.0 / HEAD_DIM
    mean = _head_sum(y, ones_bd[...]) * inv
    d = y - mean
    var = _head_sum(d * d, ones_bd[...]) * inv
    yn = d * lax.rsqrt(var + GN_EPS) * gng_ref[...] + gnb_ref[...]
    rw = (yn + bonus_ref[...]) * g_ref[...]
    zg = zg_ref[...]
    merged = (_sigmoid(zg[:, 0:D_MODEL]) * _dot(rw, wa_ref[...])
              + _sigmoid(zg[:, D_MODEL:]) * _dot(fox_ref[...], wb_ref[...]))
    o_ref[...] = x_ref[...] + _dot(merged, wo_ref[...])


def _merge(x1, y, bonus, g, fox, zg, gn_g, gn_b, ones_bd, w_a, w_b, w_o, tm, keep=None):
    m = x1.shape[0]
    out_row = pl.BlockSpec((tm, D_MODEL), lambda i: (i, 0))
    if keep is None:
        m_out = m
        row = lambda n: pl.BlockSpec((tm, n), lambda i: (i, 0))
    else:
        seq_len, start = keep
        assert (seq_len - start) % tm == 0 and start % 8 == 0
        per_seq = (seq_len - start) // tm
        m_out = (m // seq_len) * per_seq * tm
        row = lambda n: pl.BlockSpec(
            (pl.Element(tm), pl.Element(n)),
            lambda i: (pl.multiple_of((i // per_seq) * seq_len + start + (i % per_seq) * tm, 8), 0))
    return pl.pallas_call(
        _merge_kernel,
        grid=(m_out // tm,),
        in_specs=[row(D_MODEL), row(WIDTH), row(WIDTH), row(WIDTH), row(WIDTH), row(2 * D_MODEL),
                  _resident((1, WIDTH)), _resident((1, WIDTH)), _resident((WIDTH, WIDTH)),
                  _resident((WIDTH, D_MODEL)), _resident((WIDTH, D_MODEL)), _resident((D_MODEL, D_MODEL))],
        out_specs=out_row,
        out_shape=jax.ShapeDtypeStruct((m_out, D_MODEL), F32),
        compiler_params=_params(("parallel",)),
        name="merge",
    )(x1, y, bonus, g, fox, zg, gn_g, gn_b, ones_bd, w_a, w_b, w_o)


def _to_blockdiag(state):
    bsz = state.shape[0]
    s = state.reshape(bsz, 2, 4, HEAD_DIM, HEAD_DIM)
    eye = jnp.eye(4, dtype=state.dtype)
    out = jnp.einsum("bqhvk,hg->bqhvgk", s, eye)
    return out.reshape(bsz, 2, QUAD, QUAD)


def _from_blockdiag(sbd):
    bsz = sbd.shape[0]
    s = sbd.reshape(bsz, 2, 4, HEAD_DIM, 4, HEAD_DIM)
    idx = jnp.arange(4)
    s = s[:, :, idx, :, idx, :]
    return jnp.moveaxis(s, 0, 2).reshape(bsz, N_HEADS, HEAD_DIM, HEAD_DIM)


def _layer_weights(l, ffn1_norm, ffn1_w_gate, ffn1_w_up, ffn1_w_down, mix_norm, w_in, mu_shift, w0,
                   w_decay_up, a0, w_aaa_up, w_gate_up, k_k, k_a, r_k, gn_g, gn_b, q_norm, k_norm, b_f,
                   w_a, w_b, w_o, ffn2_norm, ffn2_w_gate, ffn2_w_up, ffn2_w_down):
    row = lambda x: x.reshape(1, -1).astype(F32)
    bf = lambda x: x.astype(BF16)
    wi = w_in[l]
    w_f = jnp.pad(wi[:, R_COLS + QKV_COLS:R_COLS + QKV_COLS + N_HEADS], ((0, 0), (0, LANES - N_HEADS)))
    return dict(
        ffn1=(row(ffn1_norm[l]), bf(ffn1_w_gate[l]), bf(ffn1_w_up[l]), bf(ffn1_w_down[l])),
        ffn2=(row(ffn2_norm[l]), bf(ffn2_w_gate[l]), bf(ffn2_w_up[l]), bf(ffn2_w_down[l])),
        mix_norm=row(mix_norm[l]),
        w_r=bf(wi[:, 0:R_COLS]), w_q=bf(wi[:, R_COLS:R_COLS + QKV_COLS]), w_f=bf(w_f),
        w_g=bf(wi[:, R_COLS + QKV_COLS + N_HEADS:]),
        prep=dict(mu=row(mu_shift[l]), w0=row(w0[l]), wdu=bf(w_decay_up[l]), a0=row(a0[l]),
                  wau=bf(w_aaa_up[l]), wgu=bf(w_gate_up[l]), k_k=row(k_k[l]), k_a=row(k_a[l]),
                  r_k=row(r_k[l]), ones_bd=jnp.asarray(_np_ones_bd(WIDTH, HEAD_DIM), BF16)),
        gn_g=row(gn_g[l]), gn_b=row(gn_b[l]),
        q_norm=row(jnp.tile(q_norm[l], N_HEADS)), k_norm=row(jnp.tile(k_norm[l], N_HEADS)),
        b_f=jnp.pad(row(b_f[l]), ((0, 0), (0, LANES - N_HEADS))),
        w_a=bf(w_a[l]), w_b=bf(w_b[l]), w_o=bf(w_o[l]),
    )


def _prompt_layer(x, w, bsz, t_len, n_pad, last):
    ones_bd = w["prep"]["ones_bd"]
    x1 = _ffn(x, *w["ffn1"], tm=544)
    zr, zq, zf, zg = _proj(x1, w["mix_norm"], w["w_r"], w["w_q"], w["w_f"], w["w_g"], tm=272)
    r, lw, k, v, a, b, g, bonus = _prep(zr, None, w["prep"], tm=544, seq_len=t_len)
    seq = lambda arr: arr.reshape(bsz, t_len, WIDTH)
    s0 = jnp.zeros((bsz, 2, QUAD, QUAD), F32)
    y, s_t = _rec([seq(r), seq(lw), seq(k), seq(v), seq(a), seq(b)], s0, c=64, first_chunk=n_pad // 64)
    qn, kn, lf, cum = _foxprep(zq, zf, w["q_norm"], w["k_norm"], w["b_f"], ones_bd,
                                      rows_per_step=t_len, n_steps=bsz, n_pad=n_pad, with_cumsum=True)
    fox = _foxattn(qn, kn, zq, cum, bsz, t_len, n_pad, n_qtiles=4)
    x2 = _merge(x1, y.reshape(bsz * t_len, WIDTH), bonus, g, fox, zg, w["gn_g"], w["gn_b"], ones_bd,
                w["w_a"], w["w_b"], w["w_o"], tm=256 if last else 272,
                keep=(t_len, n_pad + N_META) if last else None)
    x3 = _ffn(x2, *w["ffn2"], tm=512 if last else 544)
    return x3, zr, zq, kn, lf, s_t


def _sample_layer(x, w, shift_prev, wkv0, page_table, cache_k, cache_v, cache_lf):
    bsz = x.shape[0]
    ones_bd = w["prep"]["ones_bd"]
    x1 = _ffn(x, *w["ffn1"], tm=bsz)
    zr, zq, zf, zg = _proj(x1, w["mix_norm"], w["w_r"], w["w_q"], w["w_f"], w["w_g"], tm=bsz)
    prepped = _prep(zr, shift_prev, w["prep"], tm=bsz, seq_len=1)
    g, bonus = prepped[6], prepped[7]
    tok = jnp.stack(prepped[:6], axis=1).reshape(bsz, 6, N_HEADS, 1, HEAD_DIM)
    y, s_t = _step(tok, wkv0.astype(F32), nb=8)
    y = y.reshape(bsz, WIDTH)
    qn, kn, lf = _foxprep(zq, zf, w["q_norm"], w["k_norm"], w["b_f"], ones_bd,
                          rows_per_step=bsz, n_steps=1, n_pad=0, with_cumsum=False)
    vn = zq[:, 2 * WIDTH:]
    fox = _dec(page_table, qn, kn, vn, lf, cache_k, cache_v, cache_lf, n_group=32)
    x2 = _merge(x1, y, bonus, g, fox, zg, w["gn_g"], w["gn_b"], ones_bd, w["w_a"], w["w_b"], w["w_o"],
                tm=bsz)
    x3 = _ffn(x2, *w["ffn2"], tm=bsz)
    return x3, zr, kn, vn, lf, s_t


def kernel(x_prompt, x_sample, cache_k, cache_v, cache_logf, state_wkv, state_shift, page_table, meta_tokens, ffn1_norm, ffn1_w_gate, ffn1_w_up, ffn1_w_down, mix_norm, w_in, mu_shift, w0, w_decay_up, a0, w_aaa_up, w_gate_up, k_k, k_a, r_k, gn_g, gn_b, q_norm, k_norm, b_f, w_a, w_b, w_o, ffn2_norm, ffn2_w_gate, ffn2_w_up, ffn2_w_down):
    depth = w_in.shape[0]
    bp, seq, _ = x_prompt.shape
    bs = x_sample.shape[0]
    l_tok = seq + N_META
    n_pad = (-l_tok) % LANES
    t_len = l_tok + n_pad
    meta = jnp.broadcast_to(meta_tokens[None].astype(F32), (bp, N_META, D_MODEL))
    h_p = jnp.concatenate([jnp.zeros((bp, n_pad, D_MODEL), F32), meta, x_prompt.astype(F32)], axis=1)
    h_p = h_p.reshape(bp * t_len, D_MODEL)
    h_s = x_sample.reshape(bs, D_MODEL).astype(F32)
    rows_p, rows_s = [], []
    for l in range(depth):
        w = _layer_weights(l, ffn1_norm, ffn1_w_gate, ffn1_w_up, ffn1_w_down, mix_norm, w_in, mu_shift, w0,
                           w_decay_up, a0, w_aaa_up, w_gate_up, k_k, k_a, r_k, gn_g, gn_b, q_norm, k_norm,
                           b_f, w_a, w_b, w_o, ffn2_norm, ffn2_w_gate, ffn2_w_up, ffn2_w_down)
        h_p, zr, zq, kn, lf, s_t = _prompt_layer(h_p, w, bp, t_len, n_pad, last=(l == depth - 1))
        real = lambda arr, n: arr.reshape(bp, t_len, n)[:, n_pad:]
        rows_p.append((real(kn, WIDTH).reshape(bp, l_tok, N_HEADS, HEAD_DIM),
                       real(zq, QKV_COLS)[:, :, 2 * WIDTH:].reshape(bp, l_tok, N_HEADS, HEAD_DIM),
                       real(lf, LANES)[:, :, :N_HEADS],
                       _from_blockdiag(s_t),
                       zr.reshape(bp, t_len, R_COLS)[:, -1]))
        h_s, zr_s, kn_s, vn_s, lf_s, s_ts = _sample_layer(
            h_s, w, state_shift[l].astype(F32), state_wkv[l], page_table, cache_k[l], cache_v[l],
            cache_logf[l])
        rows_s.append((kn_s.reshape(bs, 1, N_HEADS, HEAD_DIM), vn_s.reshape(bs, 1, N_HEADS, HEAD_DIM),
                       lf_s[:, :N_HEADS].reshape(bs, 1, N_HEADS), s_ts, zr_s))
    y_prompt = h_p.reshape(bp, seq, D_MODEL)
    y_sample = h_s.reshape(bs, 1, D_MODEL)
    stk = lambda rows, i: jnp.stack([r[i] for r in rows], axis=0)
    return (y_prompt, y_sample,
            stk(rows_p, 0), stk(rows_p, 1), stk(rows_p, 2), stk(rows_p, 3), stk(rows_p, 4),
            stk(rows_s, 0), stk(rows_s, 1), stk(rows_s, 2), stk(rows_s, 3), stk(rows_s, 4))
```

```python
import functools

import numpy as np
import jax
import jax.numpy as jnp
from jax import lax
from jax.experimental import pallas as pl
from jax.experimental.pallas import tpu as pltpu

F32 = jnp.float32
BF16 = jnp.bfloat16

D_MODEL = 1024
HEAD_DIM = 64
N_HEADS = 8
WIDTH = N_HEADS * HEAD_DIM
N_META = 16
DECAY_LORA = 64
AAA_LORA = 64
GATE_LORA = 160
FFN_DIM = 2816
R_COLS = 3 * WIDTH + DECAY_LORA + AAA_LORA + GATE_LORA
QKV_COLS = 3 * WIDTH
NORM_EPS = 1e-6
GN_EPS = HEAD_DIM * 1e-5
NEG_INF = -1e30
PAGE = 128
LANES = 128
QUAD = 4 * HEAD_DIM
FFN_CHUNK = 256
VMEM_LIMIT = 56 * 1024 * 1024


def _dot(a, b):
    return jnp.dot(a.astype(BF16), b.astype(BF16), preferred_element_type=F32)


def _dot_nt(a, b):
    return lax.dot_general(a.astype(BF16), b.astype(BF16), (((1,), (1,)), ((), ())),
                           preferred_element_type=F32)


def _dot_tn(a, b):
    return lax.dot_general(a.astype(BF16), b.astype(BF16), (((0,), (0,)), ((), ())),
                           preferred_element_type=F32)


def _split(x, n):
    parts = []
    r = x
    for i in range(n):
        p = r.astype(BF16)
        parts.append(p)
        if i + 1 < n:
            r = r - p.astype(F32)
    return parts


def _dot_x(a, b, dot=_dot, na=1, nb=1):
    pa = _split(a, na) if na > 1 else [a]
    pb = _split(b, nb) if nb > 1 else [b]
    out = None
    for i, x in enumerate(pa):
        for j, y in enumerate(pb):
            if i + j >= max(na, nb):
                continue
            t = dot(x, y)
            out = t if out is None else out + t
    return out


def _sigmoid(x):
    return 1.0 / (1.0 + jnp.exp(-x))


def _rms(x, g):
    ms = jnp.mean(x * x, axis=-1, keepdims=True)
    return x * lax.rsqrt(ms + NORM_EPS) * g


def _head_sum(x, ones_bd):
    return _dot_x(x, ones_bd, na=2)


def _params(sem, vmem=VMEM_LIMIT):
    return pltpu.CompilerParams(dimension_semantics=sem, vmem_limit_bytes=vmem)


def _resident(shape):
    nd = len(shape)
    return pl.BlockSpec(shape, lambda *_: (0,) * nd, pipeline_mode=pl.Buffered(1))


def _np_ones_bd(n, blk):
    i = np.arange(n)
    return (i[:, None] // blk == i[None, :] // blk).astype(np.float32)


def _ffn_kernel(x_ref, g_ref, wg_ref, wu_ref, wd_ref, o_ref, acc_ref):
    x = x_ref[...]
    xn = _rms(x, g_ref[...]).astype(BF16)
    for c in range(FFN_DIM // FFN_CHUNK):
        sl = slice(c * FFN_CHUNK, (c + 1) * FFN_CHUNK)
        gate = jnp.dot(xn, wg_ref[:, sl], preferred_element_type=F32)
        up = jnp.dot(xn, wu_ref[:, sl], preferred_element_type=F32)
        h = (gate * _sigmoid(gate) * up).astype(BF16)
        part = jnp.dot(h, wd_ref[sl, :], preferred_element_type=F32)
        if c == 0:
            acc_ref[...] = part
        else:
            acc_ref[...] += part
    o_ref[...] = x + 0.5 * acc_ref[...]


def _ffn(x, norm_g, w_gate, w_up, w_down, tm, side=None):
    m = x.shape[0]
    row = pl.BlockSpec((tm, D_MODEL), lambda i, *_: (i, 0))
    return _row_call(
        _ffn_kernel, "ffn", m // tm,
        in_specs=[row, _resident((1, D_MODEL)), _resident((D_MODEL, FFN_DIM)),
                  _resident((D_MODEL, FFN_DIM)), _resident((FFN_DIM, D_MODEL))],
        out_specs=[row],
        out_shape=[jax.ShapeDtypeStruct((m, D_MODEL), F32)],
        scratch_shapes=[pltpu.VMEM((tm, D_MODEL), F32)],
        args=(x, norm_g, w_gate, w_up, w_down), side=side)


def _proj_kernel(x_ref, g_ref, wr_ref, wq_ref, wf_ref, wg_ref, zr_ref, zq_ref, zf_ref, zg_ref):
    xn = _rms(x_ref[...], g_ref[...]).astype(BF16)
    zr_ref[...] = jnp.dot(xn, wr_ref[...], preferred_element_type=F32)
    zq_ref[...] = jnp.dot(xn, wq_ref[...], preferred_element_type=F32)
    zf_ref[...] = jnp.dot(xn, wf_ref[...], preferred_element_type=F32)
    zg_ref[...] = jnp.dot(xn, wg_ref[...], preferred_element_type=F32)


def _proj(x, norm_g, w_r, w_q, w_f, w_g, tm, side=None):
    m = x.shape[0]
    row = lambda n: pl.BlockSpec((tm, n), lambda i, *_: (i, 0))
    return _row_call(
        _proj_kernel, "in_proj", m // tm,
        in_specs=[row(D_MODEL), _resident((1, D_MODEL)), _resident((D_MODEL, R_COLS)),
                  _resident((D_MODEL, QKV_COLS)), _resident((D_MODEL, LANES)),
                  _resident((D_MODEL, 2 * D_MODEL))],
        out_specs=[row(R_COLS), row(QKV_COLS), row(LANES), row(2 * D_MODEL)],
        out_shape=[jax.ShapeDtypeStruct((m, R_COLS), F32),
                   jax.ShapeDtypeStruct((m, QKV_COLS), F32),
                   jax.ShapeDtypeStruct((m, LANES), F32),
                   jax.ShapeDtypeStruct((m, 2 * D_MODEL), F32)],
        scratch_shapes=[],
        args=(x, norm_g, w_r, w_q, w_f, w_g), side=side)


def _prep_math(z, zprev, mu, w0, wdu, a0, wau, wgu, k_k, k_a, r_k, ones_bd, outs):
    r_ref, lw_ref, k_ref, v_ref, a_ref, b_ref, g_ref, bonus_ref = outs
    zs = z + (zprev - z) * mu
    r = zs[:, 0:WIDTH]
    k = zs[:, WIDTH:2 * WIDTH]
    v = zs[:, 2 * WIDTH:3 * WIDTH]
    o = 3 * WIDTH
    d_lo = zs[:, o:o + DECAY_LORA]
    a_lo = zs[:, o + DECAY_LORA:o + DECAY_LORA + AAA_LORA]
    g_lo = zs[:, o + DECAY_LORA + AAA_LORA:R_COLS]
    wpre = w0 + _dot(jnp.tanh(d_lo), wdu)
    y = -wpre
    softplus = jnp.maximum(y, 0.0) + jnp.log(1.0 + jnp.exp(-jnp.abs(y)))
    w_log = -softplus - 0.5
    lw_ref[...] = -jnp.exp(w_log)
    a = _sigmoid(a0 + _dot(a_lo, wau))
    g_ref[...] = _dot(_sigmoid(g_lo), wgu)
    kk = k * k_k
    k2 = k * (1.0 + (a - 1.0) * k_a)
    kk = kk * lax.rsqrt(_head_sum(kk * kk, ones_bd) + 1e-12)
    r_ref[...] = r
    k_ref[...] = k2
    v_ref[...] = v
    a_ref[...] = -kk
    b_ref[...] = kk * a
    bonus_ref[...] = _head_sum(r * k2 * r_k, ones_bd) * v


def _prep_prompt_kernel(seq_len, z_ref, zp_ref, mu, w0, wdu, a0, wau, wgu, k_k, k_a, r_k, ones_bd, *outs):
    z = z_ref[...]
    tm = z.shape[0]
    rolled = pltpu.roll(z, 1, 0)
    row = lax.broadcasted_iota(jnp.int32, (tm, 1), 0)
    zprev = jnp.where(row == 0, zp_ref[7:8, :], rolled)
    grow = row + pl.program_id(0) * tm
    zprev = jnp.where(grow % seq_len == 0, 0.0, zprev)
    _prep_math(z, zprev, mu[...], w0[...], wdu[...], a0[...], wau[...], wgu[...], k_k[...], k_a[...],
               r_k[...], ones_bd[...], outs)


def _prep_sample_kernel(z_ref, zp_ref, mu, w0, wdu, a0, wau, wgu, k_k, k_a, r_k, ones_bd, *outs):
    _prep_math(z_ref[...], zp_ref[...], mu[...], w0[...], wdu[...], a0[...], wau[...], wgu[...],
               k_k[...], k_a[...], r_k[...], ones_bd[...], outs)


def _prep(zr, zprev, pw, tm, seq_len):
    m = zr.shape[0]
    row = lambda n: pl.BlockSpec((tm, n), lambda i: (i, 0))
    if zprev is None:
        body = functools.partial(_prep_prompt_kernel, seq_len)
        prev_spec = pl.BlockSpec((8, R_COLS), lambda i: (jnp.maximum(i * (tm // 8) - 1, 0), 0))
        zprev = zr
    else:
        body = _prep_sample_kernel
        prev_spec = row(R_COLS)
    consts = [pw["mu"], pw["w0"], pw["wdu"], pw["a0"], pw["wau"], pw["wgu"], pw["k_k"], pw["k_a"],
              pw["r_k"], pw["ones_bd"]]
    return pl.pallas_call(
        body,
        grid=(m // tm,),
        in_specs=[row(R_COLS), prev_spec] + [_resident(c.shape) for c in consts],
        out_specs=[row(WIDTH)] * 8,
        out_shape=[jax.ShapeDtypeStruct((m, WIDTH), F32)] * 8,
        compiler_params=_params(("parallel",)),
        name="rwkv_prep",
    )(zr, zprev, *consts)


def _rec_masks(c):
    t = np.arange(c)[:, None]
    hs = np.arange(4 * c)[None, :]
    s = hs % c
    strict = (s < t).astype(np.float32)
    incl = (s <= t).astype(np.float32)
    eye = (s == t).astype(np.float32)
    rows = np.arange(4 * c)[:, None]
    bd = (rows // c == hs // c).astype(np.float32)
    tri = (np.arange(c)[None, :] <= np.arange(c)[:, None]).astype(np.float32)
    lane = np.arange(QUAD)[None, :] // HEAD_DIM
    head = (lane == rows // c).astype(np.float32)
    sbd = _np_ones_bd(QUAD, HEAD_DIM)
    return strict, incl, eye, bd, tri, head, sbd


def _stack_heads(x, head):
    return jnp.concatenate([x.astype(BF16)] * 4, axis=0) * head


def _rec_kernel(c, first_chunk, n_doubling, group, r_ref, lw_ref, k_ref, v_ref, a_ref, b_ref, s0_ref,
                strict_ref, incl_ref, eye_ref, bd_ref, tri_ref, head_ref, sbd_ref,
                y_ref, sT_ref, s_ref, ry_scr, yc_scr, tr_scr, cs_scr, wl_scr):
    t_len = r_ref.shape[0]
    n_iter = t_len // c - first_chunk
    if first_chunk > 0:
        y_ref[0:first_chunk * c, :] = jnp.zeros((first_chunk * c, QUAD), F32)

    def to_bd(p_row):
        return jnp.concatenate([p_row.astype(BF16)] * 4, axis=0) * bd_ref[...]

    def chunk_rows(ci):
        return pl.ds(pl.multiple_of(ci * c, c), c)

    def prepare(gi, carry):
        strict = strict_ref[...]
        incl2 = jnp.concatenate([incl_ref[...]] * 2, axis=1)
        head = head_ref[...]
        slots = [gi * group + j for j in range(group)]
        loaded = []
        for slot in slots:
            rows = chunk_rows(slot + first_chunk)
            loaded.append((r_ref[rows, :], lw_ref[rows, :], k_ref[rows, :], v_ref[rows, :],
                           a_ref[rows, :], b_ref[rows, :]))
        cums = [_dot_x(tri_ref[...], ld[1], nb=3) for ld in loaded]
        work = []
        for slot, (r, lw, k, v, a, b), cum in zip(slots, loaded, cums):
            cum_last = cum[c - 1:c, :]
            e_inv = jnp.exp(-cum)
            at = (a * jnp.exp(cum - lw)).astype(BF16)
            rt = r * jnp.exp(cum)
            e_last = jnp.exp(cum_last - cum)
            wl_scr[slot] = jnp.broadcast_to(jnp.exp(cum_last), (8, QUAD))
            rhs = jnp.concatenate([_stack_heads(b * e_inv, head), _stack_heads(k * e_inv, head)], axis=0)
            gmat = _dot_nt(jnp.concatenate([at, rt.astype(BF16)], axis=0), rhs)
            work.append(dict(at=at, rt=rt, v=v, bc=(b * e_last).astype(BF16), kc=(k * e_last).astype(BF16),
                             gmat=gmat))
        p_rows, t_rows = [], []
        for w in work:
            gmat = w.pop("gmat")
            n_row = gmat[0:c, 0:4 * c] * strict
            w["v_st"] = _stack_heads(w["v"], head)
            w["xv"] = _dot(gmat[0:c, 4 * c:8 * c] * strict, w["v_st"])
            w["n_r"] = (gmat[c:2 * c, :] * incl2).astype(BF16)
            p_rows.append(n_row)
            t_rows.append(eye_ref[...] + n_row)
        for _ in range(n_doubling):
            t_bds = [to_bd(t) for t in t_rows]
            p_rows = [_dot(p, to_bd(p)) for p in p_rows]
            t_rows = [t + _dot(p, t_bd) for t, p, t_bd in zip(t_rows, p_rows, t_bds)]
        for slot, w, t_row in zip(slots, work, t_rows):
            both = jnp.concatenate([_stack_heads(w["at"], head), _stack_heads(w["xv"], head)], axis=1)
            t_both = _dot(t_row, both)
            ta, txv = t_both[:, 0:QUAD], t_both[:, QUAD:]
            n_rb = w["n_r"][:, 0:4 * c]
            ry_scr[slot] = (w["rt"] + _dot(n_rb, _stack_heads(ta, head))).astype(BF16)
            yc_scr[slot] = _dot(w["n_r"], jnp.concatenate([_stack_heads(txv, head), w["v_st"]],
                                                          axis=0))
            tr_scr[slot] = (_dot_tn(ta, w["bc"]) * sbd_ref[...]).astype(BF16)
            cs_scr[slot] = _dot_tn(jnp.concatenate([txv, w["v"]], axis=0),
                                   jnp.concatenate([w["bc"], w["kc"]], axis=0)) * sbd_ref[...]
        return carry

    lax.fori_loop(0, n_iter // group, prepare, 0)

    s_ref[...] = s0_ref[...]

    def advance(slot, carry):
        rows = chunk_rows(slot + first_chunk)
        s_old = s_ref[...]
        s_bf = s_old.astype(BF16)
        y_ref[rows, :] = _dot_nt(ry_scr[slot], s_bf) + yc_scr[slot]
        s_ref[...] = s_old * wl_scr[slot][0:1, :] + _dot(s_bf, tr_scr[slot]) + cs_scr[slot]
        return carry

    lax.fori_loop(0, n_iter, advance, 0)
    sT_ref[...] = s_ref[...]


def _rec(arrs, s0, c, first_chunk):
    bsz, t_len, _ = arrs[0].shape
    strict, incl, eye, bd, tri, head, sbd = (jnp.asarray(m) for m in _rec_masks(c))
    masks = [strict, incl, eye, bd.astype(BF16), tri, head.astype(BF16), sbd]
    n_doubling = max(int(np.ceil(np.log2(c))) - 1, 0)
    n_iter = t_len // c - first_chunk
    group = next(g for g in (11, 3, 1) if n_iter % g == 0)
    seq = pl.BlockSpec((None, t_len, QUAD), lambda b, q: (b, 0, q))
    st = pl.BlockSpec((None, None, QUAD, QUAD), lambda b, q: (b, q, 0, 0))
    return pl.pallas_call(
        functools.partial(_rec_kernel, c, first_chunk, n_doubling, group),
        grid=(bsz, 2),
        in_specs=[seq] * 6 + [st] + [_resident(m.shape) for m in masks],
        out_specs=[seq, st],
        out_shape=[jax.ShapeDtypeStruct((bsz, t_len, WIDTH), F32),
                   jax.ShapeDtypeStruct((bsz, 2, QUAD, QUAD), F32)],
        scratch_shapes=[pltpu.VMEM((QUAD, QUAD), F32),
                        pltpu.VMEM((n_iter, c, QUAD), BF16), pltpu.VMEM((n_iter, c, QUAD), F32),
                        pltpu.VMEM((n_iter, QUAD, QUAD), BF16), pltpu.VMEM((n_iter, QUAD, QUAD), F32),
                        pltpu.VMEM((n_iter, 8, QUAD), F32)],
        compiler_params=_params(("parallel", "parallel")),
        name="rwkv_rec",
    )(*arrs, s0, *masks)


def _step_kernel(tok_ref, s_ref, y_ref, so_ref):
    s = s_ref[...]
    r, lw, k, v, a, b = (tok_ref[:, i] for i in range(6))
    eye = (lax.broadcasted_iota(jnp.int32, (HEAD_DIM, HEAD_DIM), 0)
           == lax.broadcasted_iota(jnp.int32, (HEAD_DIM, HEAD_DIM), 1)).astype(F32)
    u = jnp.sum(s * a, axis=-1, keepdims=True)
    v_col = jnp.sum(eye * v, axis=-1, keepdims=True)
    s_new = s * jnp.exp(lw) + u * b + v_col * k
    so_ref[...] = s_new
    y_col = jnp.sum(s_new * r, axis=-1, keepdims=True)
    y_ref[...] = jnp.sum(y_col * eye, axis=-2, keepdims=True)


def _step(tok, state, nb):
    bsz = state.shape[0]
    return pl.pallas_call(
        _step_kernel,
        grid=(bsz // nb,),
        in_specs=[pl.BlockSpec((nb, 6, N_HEADS, 1, HEAD_DIM), lambda i: (i, 0, 0, 0, 0)),
                  pl.BlockSpec((nb, N_HEADS, HEAD_DIM, HEAD_DIM), lambda i: (i, 0, 0, 0))],
        out_specs=[pl.BlockSpec((nb, N_HEADS, 1, HEAD_DIM), lambda i: (i, 0, 0, 0)),
                   pl.BlockSpec((nb, N_HEADS, HEAD_DIM, HEAD_DIM), lambda i: (i, 0, 0, 0))],
        out_shape=[jax.ShapeDtypeStruct((bsz, N_HEADS, 1, HEAD_DIM), F32),
                   jax.ShapeDtypeStruct((bsz, N_HEADS, HEAD_DIM, HEAD_DIM), F32)],
        compiler_params=_params(("parallel",)),
        name="rwkv_step",
    )(tok, state)


def _foxprep_kernel(n_pad, with_cumsum, q_ref, k_ref, f_ref, qn_ref, kn_ref, bf_ref, ones_bd, tri_ref,
                    qo_ref, ko_ref, lf_ref, *cum_refs):
    q = q_ref[...]
    k = k_ref[...]
    inv = 1.0 / HEAD_DIM
    qo_ref[...] = (q * lax.rsqrt(_head_sum(q * q, ones_bd[...]) * inv + NORM_EPS) * qn_ref[...]
                   * (HEAD_DIM ** -0.5))
    ko_ref[...] = k * lax.rsqrt(_head_sum(k * k, ones_bd[...]) * inv + NORM_EPS) * kn_ref[...]
    x = f_ref[...] + bf_ref[...]
    lf = jnp.minimum(x, 0.0) - jnp.log(1.0 + jnp.exp(-jnp.abs(x)))
    lane = lax.broadcasted_iota(jnp.int32, lf.shape, 1)
    lf = jnp.where(lane < N_HEADS, lf, 0.0)
    if with_cumsum:
        row = lax.broadcasted_iota(jnp.int32, lf.shape, 0)
        lf = jnp.where(row >= n_pad, lf, 0.0)
    lf_ref[...] = lf
    if with_cumsum:
        (c_ref,) = cum_refs
        t_len = lf.shape[0]
        carry = jnp.zeros((1, LANES), F32)
        for blk in range(t_len // LANES):
            rows = slice(blk * LANES, (blk + 1) * LANES)
            cb = _dot_x(tri_ref[...], lf[rows, :], nb=3) + carry
            c_ref[rows, :] = cb
            carry = cb[LANES - 1:LANES, :]


def _foxprep(zq, zf, qn_w, kn_w, bf_row, ones_bd, rows_per_step, n_steps, n_pad, with_cumsum):
    m = zq.shape[0]
    tm = rows_per_step
    tri = jnp.asarray(np.tril(np.ones((LANES, LANES), np.float32)))
    col = lambda j: pl.BlockSpec((tm, WIDTH), lambda i: (i, j))
    row = lambda n: pl.BlockSpec((tm, n), lambda i: (i, 0))
    out_specs = [row(WIDTH), row(WIDTH), row(LANES)]
    out_shape = [jax.ShapeDtypeStruct((m, WIDTH), F32), jax.ShapeDtypeStruct((m, WIDTH), F32),
                 jax.ShapeDtypeStruct((m, LANES), F32)]
    if with_cumsum:
        out_specs += [row(LANES)]
        out_shape += [jax.ShapeDtypeStruct((m, LANES), F32)]
    return pl.pallas_call(
        functools.partial(_foxprep_kernel, n_pad, with_cumsum),
        grid=(n_steps,),
        in_specs=[col(0), col(1), row(LANES), _resident((1, WIDTH)), _resident((1, WIDTH)),
                  _resident((1, LANES)), _resident((WIDTH, WIDTH)), _resident((LANES, LANES))],
        out_specs=out_specs,
        out_shape=out_shape,
        compiler_params=_params(("parallel",)),
        name="fox_prep",
    )(zq, zq, zf, qn_w, kn_w, bf_row, ones_bd, tri)


def _foxattn_kernel(n_pad, n_qtiles, q_ref, k_ref, v_ref, c_ref, o_ref, qa_ref, ka_ref):
    t_len = q_ref.shape[0]
    tq = t_len // n_qtiles
    pair = pl.program_id(1)
    lane = lax.broadcasted_iota(jnp.int32, (1, LANES), 1)
    row = lax.broadcasted_iota(jnp.int32, (t_len, 1), 0)
    causal = (lax.broadcasted_iota(jnp.int32, (tq, tq), 1) <= lax.broadcasted_iota(jnp.int32, (tq, tq), 0))
    for h in range(2):
        hmask = (lane // HEAD_DIM == h).astype(F32)
        sel = (lane == 2 * pair + h).astype(F32)
        c = jnp.sum(c_ref[...] * sel, axis=-1, keepdims=True)
        hi = c.astype(BF16).astype(F32)
        mid = (c - hi).astype(BF16).astype(F32)
        lo = (c - hi - mid).astype(BF16).astype(F32)
        base = HEAD_DIM * (1 - h)
        at = lambda i: (lane == base + i).astype(F32)
        ones_q = at(3) + at(4) + at(5)
        ones_k = at(0) + at(1) + at(2)
        hi_k = jnp.where(row >= n_pad, hi, -NEG_INF)
        qa_ref[...] = (q_ref[...] * hmask + hi * at(0) + mid * at(1) + lo * at(2) + ones_q).astype(BF16)
        ka_ref[...] = (k_ref[...] * hmask + ones_k - hi_k * at(3) - mid * at(4) - lo * at(5)).astype(BF16)
        for qi in range(n_qtiles):
            q0 = qi * tq
            qa = qa_ref[q0:q0 + tq, :]
            s_diag = jnp.where(causal, _dot_nt(qa, ka_ref[q0:q0 + tq, :]), NEG_INF)
            m = jnp.max(s_diag, axis=-1, keepdims=True)
            if qi > 0:
                s_past = _dot_nt(qa, ka_ref[0:q0, :])
                m = jnp.maximum(m, jnp.max(s_past, axis=-1, keepdims=True))
            p_diag = jnp.exp(s_diag - m)
            l = jnp.sum(p_diag, axis=-1, keepdims=True)
            out = _dot(p_diag, v_ref[q0:q0 + tq, :])
            if qi > 0:
                p_past = jnp.exp(s_past - m)
                l = l + jnp.sum(p_past, axis=-1, keepdims=True)
                out = out + _dot(p_past, v_ref[0:q0, :])
            out = out / l * hmask
            if h == 0:
                o_ref[q0:q0 + tq, :] = out
            else:
                o_ref[q0:q0 + tq, :] += out


def _foxattn(qn, kn, zq, cum, bsz, t_len, n_pad, n_qtiles):
    seq = lambda j0: pl.BlockSpec((t_len, LANES), lambda b, p: (b, j0 + p))
    return pl.pallas_call(
        functools.partial(_foxattn_kernel, n_pad, n_qtiles),
        grid=(bsz, N_HEADS // 2),
        in_specs=[seq(0), seq(0), seq(2 * WIDTH // LANES),
                  pl.BlockSpec((t_len, LANES), lambda b, p: (b, 0))],
        out_specs=seq(0),
        out_shape=jax.ShapeDtypeStruct((bsz * t_len, WIDTH), F32),
        scratch_shapes=[pltpu.VMEM((t_len, LANES), BF16), pltpu.VMEM((t_len, LANES), BF16)],
        compiler_params=_params(("parallel", "parallel")),
        name="fox_attn",
    )(qn, kn, zq, cum)


def _dec_consts():
    t = np.arange(PAGE)
    suf = (t[:, None] > t[None, :]).astype(np.float32)
    bdm = (np.arange(WIDTH)[None, :] // HEAD_DIM == np.arange(8)[:, None]).astype(np.float32)
    eye = (np.arange(LANES)[None, :] == np.arange(8)[:, None]).astype(np.float32)
    return suf, bdm, eye


ST_M, ST_L, ST_C, ST_ACC = 0, LANES, 2 * LANES, 3 * LANES
STATE_W = 3 * LANES + WIDTH


def _dec_init(st_ref, q_ref, kn_ref, vn_ref, bdm_ref):
    bdm = bdm_ref[...]
    s_new = jnp.sum(q_ref[...] * bdm * kn_ref[...], axis=-1, keepdims=True)
    st_ref[:, ST_M:ST_L] = jnp.broadcast_to(s_new, (8, LANES))
    st_ref[:, ST_L:ST_C] = jnp.ones((8, LANES), F32)
    st_ref[:, ST_C:ST_ACC] = jnp.zeros((8, LANES), F32)
    st_ref[:, ST_ACC:] = vn_ref[...] * bdm


def _dec_update(st_ref, q_ref, lfn_ref, suf_ref, bdm_ref, eye_ref, k_refs, v_refs, lf_refs):
    qb = (q_ref[...] * bdm_ref[...]).astype(BF16)
    lf_new = jnp.sum(eye_ref[...] * lfn_ref[...], axis=-1, keepdims=True)
    carry = st_ref[:, ST_C:ST_C + 1]
    scores = []
    for g in range(len(k_refs)):
        lf_t = lf_refs[g][...]
        suffix = _dot_x(lf_t, suf_ref[...], na=3)
        kt = k_refs[g][...].reshape(WIDTH, PAGE)
        scores.append(_dot(qb, kt) + (lf_new + carry + suffix))
        carry = carry + jnp.sum(lf_t, axis=-1, keepdims=True)
    s_all = jnp.concatenate(scores, axis=1)
    m_old = st_ref[:, ST_M:ST_M + 1]
    m_new = jnp.maximum(m_old, jnp.max(s_all, axis=-1, keepdims=True))
    alpha = jnp.exp(m_old - m_new)
    p = jnp.exp(s_all - m_new)
    l_new = alpha * st_ref[:, ST_L:ST_L + 1] + jnp.sum(p, axis=-1, keepdims=True)
    acc = alpha * st_ref[:, ST_ACC:]
    pb = p.astype(BF16)
    for g in range(len(v_refs)):
        vt = v_refs[g][...].reshape(WIDTH, PAGE)
        acc = acc + _dot_nt(pb[:, g * PAGE:(g + 1) * PAGE], vt)
    st_ref[:, ST_M:ST_L] = jnp.broadcast_to(m_new, (8, LANES))
    st_ref[:, ST_L:ST_C] = jnp.broadcast_to(l_new, (8, LANES))
    st_ref[:, ST_C:ST_ACC] = jnp.broadcast_to(carry, (8, LANES))
    st_ref[:, ST_ACC:] = acc


def _dec_first_kernel(n_group, pt_ref, q_ref, kn_ref, vn_ref, lfn_ref, suf_ref, bdm_ref, eye_ref, *rest):
    k_refs = rest[0:n_group]
    v_refs = rest[n_group:2 * n_group]
    lf_refs = rest[2 * n_group:3 * n_group]
    st_ref = rest[3 * n_group]

    @pl.when(pl.program_id(1) == 0)
    def _():
        _dec_init(st_ref, q_ref, kn_ref, vn_ref, bdm_ref)

    _dec_update(st_ref, q_ref, lfn_ref, suf_ref, bdm_ref, eye_ref, k_refs, v_refs, lf_refs)


def _stage_kernel(body, n_in, n_out, n_pages, steps_per_seq, final, pt_ref, *refs):
    main_in = refs[:n_in]
    q_ref, lfn_ref, stin_ref, suf_ref, bdm_ref, eye_ref = refs[n_in:n_in + 6]
    p0 = n_in + 6
    k_refs = refs[p0:p0 + n_pages]
    v_refs = refs[p0 + n_pages:p0 + 2 * n_pages]
    lf_refs = refs[p0 + 2 * n_pages:p0 + 3 * n_pages]
    o0 = p0 + 3 * n_pages
    main_out = refs[o0:o0 + n_out]
    st_ref = refs[o0 + n_out]
    n_side_out = 2 if final else 1
    scratch = refs[o0 + n_out + n_side_out:]
    part = pl.program_id(0) % steps_per_seq

    @pl.when(part == 0)
    def _():
        st_ref[...] = stin_ref[...]

    body(*main_in, *main_out, *scratch)
    _dec_update(st_ref, q_ref, lfn_ref, suf_ref, bdm_ref, eye_ref, k_refs, v_refs, lf_refs)
    if final:
        fox_ref = refs[o0 + n_out + 1]

        @pl.when(part == steps_per_seq - 1)
        def _():
            fox_ref[...] = jnp.sum(st_ref[:, ST_ACC:] * bdm_ref[...] / st_ref[:, ST_L:ST_L + 1],
                                   axis=0, keepdims=True)


def _dec_context(page_table, qn, lfn, cache_k, cache_v, cache_lf):
    bsz, n_pages = page_table.shape
    return dict(
        bsz=bsz, n_pages=n_pages, pt=page_table.reshape(-1),
        q=qn.reshape(bsz, 1, WIDTH), lfn=lfn.reshape(bsz, 1, LANES),
        consts=[jnp.asarray(c) for c in _dec_consts()],
        kt=jnp.transpose(cache_k, (0, 2, 3, 1)),
        vt=jnp.transpose(cache_v, (0, 2, 3, 1)),
        lft=jnp.transpose(cache_lf, (0, 2, 1)),
    )


def _page_specs(dec, n_group, logical_page):
    n_pages = dec["n_pages"]

    def page(shape, g):
        def index(*ids):
            pt = ids[-1]
            seq, lp = logical_page(g, *ids[:-1])
            return (pt[seq * n_pages + lp],) + (0,) * len(shape)
        return pl.BlockSpec((None,) + shape, index)

    return ([page((N_HEADS, HEAD_DIM, PAGE), g) for g in range(n_group)]
            + [page((N_HEADS, HEAD_DIM, PAGE), g) for g in range(n_group)]
            + [page((N_HEADS, PAGE), g) for g in range(n_group)])


def _page_args(dec, n_group):
    return [dec["kt"]] * n_group + [dec["vt"]] * n_group + [dec["lft"]] * n_group


def _dec_first(dec, kn, vn, n_group, n_steps):
    bsz, n_pages = dec["bsz"], dec["n_pages"]
    tok = lambda n: pl.BlockSpec((None, 1, n), lambda b, j, pt: (b, 0, 0))
    const = lambda c: pl.BlockSpec(c.shape, lambda b, j, pt: (0,) * c.ndim, pipeline_mode=pl.Buffered(1))
    grid_spec = pltpu.PrefetchScalarGridSpec(
        num_scalar_prefetch=1,
        grid=(bsz, n_steps),
        in_specs=([tok(WIDTH), tok(WIDTH), tok(WIDTH), tok(LANES)] + [const(c) for c in dec["consts"]]
                  + _page_specs(dec, n_group, lambda g, b, j: (b, n_pages - 1 - (j * n_group + g)))),
        out_specs=pl.BlockSpec((None, 8, STATE_W), lambda b, j, pt: (b, 0, 0)),
    )
    return pl.pallas_call(
        functools.partial(_dec_first_kernel, n_group),
        grid_spec=grid_spec,
        out_shape=jax.ShapeDtypeStruct((bsz, 8, STATE_W), F32),
        compiler_params=_params(("parallel", "arbitrary")),
        name="fox_decode",
    )(dec["pt"], dec["q"], kn.reshape(bsz, 1, WIDTH), vn.reshape(bsz, 1, WIDTH), dec["lfn"],
      *dec["consts"], *_page_args(dec, n_group))


def _row_call(body, name, n_steps, in_specs, out_specs, out_shape, scratch_shapes, args, side=None):
    if side is None:
        return pl.pallas_call(
            body, grid=(n_steps,), in_specs=in_specs, out_specs=out_specs, out_shape=out_shape,
            scratch_shapes=scratch_shapes, compiler_params=_params(("parallel",)), name=name)(*args)
    dec, state, first_page, n_pg, k, final = side
    bsz = dec["bsz"]
    assert n_steps == bsz * k and first_page + 1 >= n_pg * k
    per_seq = lambda shape: pl.BlockSpec((None,) + shape, lambda i, pt: (i // k,) + (0,) * len(shape))
    const = lambda c: pl.BlockSpec(c.shape, lambda i, pt: (0,) * c.ndim, pipeline_mode=pl.Buffered(1))
    side_in = ([per_seq((1, WIDTH)), per_seq((1, LANES)), per_seq((8, STATE_W))]
               + [const(c) for c in dec["consts"]]
               + _page_specs(dec, n_pg, lambda g, i: (i // k, first_page - ((i % k) * n_pg + g))))
    side_out = [per_seq((8, STATE_W))] + ([per_seq((1, WIDTH))] if final else [])
    side_shape = [jax.ShapeDtypeStruct((bsz, 8, STATE_W), F32)]
    if final:
        side_shape.append(jax.ShapeDtypeStruct((bsz, 1, WIDTH), F32))
    grid_spec = pltpu.PrefetchScalarGridSpec(
        num_scalar_prefetch=1, grid=(n_steps,), in_specs=list(in_specs) + side_in,
        out_specs=list(out_specs) + side_out, scratch_shapes=scratch_shapes)
    return pl.pallas_call(
        functools.partial(_stage_kernel, body, len(in_specs), len(out_specs), n_pg, k, final),
        grid_spec=grid_spec, out_shape=list(out_shape) + side_shape,
        compiler_params=_params(("arbitrary",)), name=name,
    )(dec["pt"], *args, dec["q"], dec["lfn"], state, *dec["consts"], *_page_args(dec, n_pg))


def _merge_kernel(x_ref, y_ref, bonus_ref, g_ref, fox_ref, zg_ref, gng_ref, gnb_ref, ones_bd,
                  wa_ref, wb_ref, wo_ref, o_ref):
    y = y_ref[...]
    inv = 1.0 / HEAD_DIM
    mean = _head_sum(y, ones_bd[...]) * inv
    d = y - mean
    var = _head_sum(d * d, ones_bd[...]) * inv
    yn = d * lax.rsqrt(var + GN_EPS) * gng_ref[...] + gnb_ref[...]
    rw = (yn + bonus_ref[...]) * g_ref[...]
    zg = zg_ref[...]
    merged = (_sigmoid(zg[:, 0:D_MODEL]) * _dot(rw, wa_ref[...])
              + _sigmoid(zg[:, D_MODEL:]) * _dot(fox_ref[...], wb_ref[...]))
    o_ref[...] = x_ref[...] + _dot(merged, wo_ref[...])


def _merge(x1, y, bonus, g, fox, zg, gn_g, gn_b, ones_bd, w_a, w_b, w_o, tm, keep=None):
    m = x1.shape[0]
    out_row = pl.BlockSpec((tm, D_MODEL), lambda i: (i, 0))
    if keep is None:
        m_out = m
        row = lambda n: pl.BlockSpec((tm, n), lambda i: (i, 0))
    else:
        seq_len, start = keep
        assert (seq_len - start) % tm == 0 and start % 8 == 0
        per_seq = (seq_len - start) // tm
        m_out = (m // seq_len) * per_seq * tm
        row = lambda n: pl.BlockSpec(
            (pl.Element(tm), pl.Element(n)),
            lambda i: (pl.multiple_of((i // per_seq) * seq_len + start + (i % per_seq) * tm, 8), 0))
    return pl.pallas_call(
        _merge_kernel,
        grid=(m_out // tm,),
        in_specs=[row(D_MODEL), row(WIDTH), row(WIDTH), row(WIDTH), row(WIDTH), row(2 * D_MODEL),
                  _resident((1, WIDTH)), _resident((1, WIDTH)), _resident((WIDTH, WIDTH)),
                  _resident((WIDTH, D_MODEL)), _resident((WIDTH, D_MODEL)), _resident((D_MODEL, D_MODEL))],
        out_specs=out_row,
        out_shape=jax.ShapeDtypeStruct((m_out, D_MODEL), F32),
        compiler_params=_params(("parallel",)),
        name="merge",
    )(x1, y, bonus, g, fox, zg, gn_g, gn_b, ones_bd, w_a, w_b, w_o)


def _to_blockdiag(state):
    bsz = state.shape[0]
    s = state.reshape(bsz, 2, 4, HEAD_DIM, HEAD_DIM)
    eye = jnp.eye(4, dtype=state.dtype)
    out = jnp.einsum("bqhvk,hg->bqhvgk", s, eye)
    return out.reshape(bsz, 2, QUAD, QUAD)


def _from_blockdiag(sbd):
    bsz = sbd.shape[0]
    s = sbd.reshape(bsz, 2, 4, HEAD_DIM, 4, HEAD_DIM)
    idx = jnp.arange(4)
    s = s[:, :, idx, :, idx, :]
    return jnp.moveaxis(s, 0, 2).reshape(bsz, N_HEADS, HEAD_DIM, HEAD_DIM)


def _layer_weights(l, ffn1_norm, ffn1_w_gate, ffn1_w_up, ffn1_w_down, mix_norm, w_in, mu_shift, w0,
                   w_decay_up, a0, w_aaa_up, w_gate_up, k_k, k_a, r_k, gn_g, gn_b, q_norm, k_norm, b_f,
                   w_a, w_b, w_o, ffn2_norm, ffn2_w_gate, ffn2_w_up, ffn2_w_down):
    row = lambda x: x.reshape(1, -1).astype(F32)
    bf = lambda x: x.astype(BF16)
    wi = w_in[l]
    w_f = jnp.pad(wi[:, R_COLS + QKV_COLS:R_COLS + QKV_COLS + N_HEADS], ((0, 0), (0, LANES - N_HEADS)))
    return dict(
        ffn1=(row(ffn1_norm[l]), bf(ffn1_w_gate[l]), bf(ffn1_w_up[l]), bf(ffn1_w_down[l])),
        ffn2=(row(ffn2_norm[l]), bf(ffn2_w_gate[l]), bf(ffn2_w_up[l]), bf(ffn2_w_down[l])),
        mix_norm=row(mix_norm[l]),
        w_r=bf(wi[:, 0:R_COLS]), w_q=bf(wi[:, R_COLS:R_COLS + QKV_COLS]), w_f=bf(w_f),
        w_g=bf(wi[:, R_COLS + QKV_COLS + N_HEADS:]),
        prep=dict(mu=row(mu_shift[l]), w0=row(w0[l]), wdu=bf(w_decay_up[l]), a0=row(a0[l]),
                  wau=bf(w_aaa_up[l]), wgu=bf(w_gate_up[l]), k_k=row(k_k[l]), k_a=row(k_a[l]),
                  r_k=row(r_k[l]), ones_bd=jnp.asarray(_np_ones_bd(WIDTH, HEAD_DIM), BF16)),
        gn_g=row(gn_g[l]), gn_b=row(gn_b[l]),
        q_norm=row(jnp.tile(q_norm[l], N_HEADS)), k_norm=row(jnp.tile(k_norm[l], N_HEADS)),
        b_f=jnp.pad(row(b_f[l]), ((0, 0), (0, LANES - N_HEADS))),
        w_a=bf(w_a[l]), w_b=bf(w_b[l]), w_o=bf(w_o[l]),
    )


PAGES_PER_FFN_STEP = 20
PAGES_PER_PROJ_STEP = 16
PAGES_PER_DECODE_STEP = 28


def _layer(xp, xs, w, bp, t_len, n_pad, last, shift_prev, wkv0, page_table, cache_k, cache_v, cache_lf):
    bs = xs.shape[0]
    ones_bd = w["prep"]["ones_bd"]
    proj_w = (w["mix_norm"], w["w_r"], w["w_q"], w["w_f"], w["w_g"])
    merge_w = (w["gn_g"], w["gn_b"], ones_bd, w["w_a"], w["w_b"], w["w_o"])

    (x1s,) = _ffn(xs, *w["ffn1"], tm=bs)
    zr_s, zq_s, zf_s, zg_s = _proj(x1s, *proj_w, tm=bs)
    prepped = _prep(zr_s, shift_prev, w["prep"], tm=bs, seq_len=1)
    tok = jnp.stack(prepped[:6], axis=1).reshape(bs, 6, N_HEADS, 1, HEAD_DIM)
    y_s, s_ts = _step(tok, wkv0.astype(F32), nb=8)
    qn_s, kn_s, lf_s = _foxprep(zq_s, zf_s, w["q_norm"], w["k_norm"], w["b_f"], ones_bd,
                                rows_per_step=bs, n_steps=1, n_pad=0, with_cumsum=False)
    vn_s = zq_s[:, 2 * WIDTH:]

    dec = _dec_context(page_table, qn_s, lf_s, cache_k, cache_v, cache_lf)
    tm_ffn1, tm_proj, tm_ffn2 = 544, 272, (512 if last else 544)
    m_p = bp * t_len
    m_out = bp * (t_len - n_pad - N_META) if last else m_p
    k_ffn1, k_proj, k_ffn2 = m_p // tm_ffn1 // bs, m_p // tm_proj // bs, m_out // tm_ffn2 // bs
    assert (k_ffn1 * tm_ffn1 * bs, k_proj * tm_proj * bs, k_ffn2 * tm_ffn2 * bs) == (m_p, m_p, m_out)
    n_pages = dec["n_pages"]
    n_first = n_pages - (k_ffn1 + k_ffn2) * PAGES_PER_FFN_STEP - k_proj * PAGES_PER_PROJ_STEP
    assert n_first > 0 and n_first % PAGES_PER_DECODE_STEP == 0
    state = _dec_first(dec, kn_s, vn_s, PAGES_PER_DECODE_STEP, n_first // PAGES_PER_DECODE_STEP)
    page = n_pages - 1 - n_first

    x1, state = _ffn(xp, *w["ffn1"], tm=tm_ffn1, side=(dec, state, page, PAGES_PER_FFN_STEP, k_ffn1, False))
    page -= k_ffn1 * PAGES_PER_FFN_STEP
    zr, zq, zf, zg, state = _proj(x1, *proj_w, tm=tm_proj,
                                  side=(dec, state, page, PAGES_PER_PROJ_STEP, k_proj, False))
    page -= k_proj * PAGES_PER_PROJ_STEP
    r, lw, k, v, a, b, g, bonus = _prep(zr, None, w["prep"], tm=544, seq_len=t_len)
    seq = lambda arr: arr.reshape(bp, t_len, WIDTH)
    s0 = jnp.zeros((bp, 2, QUAD, QUAD), F32)
    y, s_t = _rec([seq(r), seq(lw), seq(k), seq(v), seq(a), seq(b)], s0, c=64, first_chunk=n_pad // 64)
    qn, kn, lf, cum = _foxprep(zq, zf, w["q_norm"], w["k_norm"], w["b_f"], ones_bd,
                               rows_per_step=t_len, n_steps=bp, n_pad=n_pad, with_cumsum=True)
    fox = _foxattn(qn, kn, zq, cum, bp, t_len, n_pad, n_qtiles=4)
    x2 = _merge(x1, y.reshape(m_p, WIDTH), bonus, g, fox, zg, *merge_w, tm=256 if last else 272,
                keep=(t_len, n_pad + N_META) if last else None)
    x3, state, fox_s = _ffn(x2, *w["ffn2"], tm=tm_ffn2,
                            side=(dec, state, page, PAGES_PER_FFN_STEP, k_ffn2, True))
    assert page + 1 == k_ffn2 * PAGES_PER_FFN_STEP

    x2s = _merge(x1s, y_s.reshape(bs, WIDTH), prepped[7], prepped[6], fox_s.reshape(bs, WIDTH), zg_s,
                 *merge_w, tm=bs)
    (x3s,) = _ffn(x2s, *w["ffn2"], tm=bs)
    return (x3, zr, zq, kn, lf, s_t), (x3s, zr_s, kn_s, vn_s, lf_s, s_ts)


def kernel(x_prompt, x_sample, cache_k, cache_v, cache_logf, state_wkv, state_shift, page_table, meta_tokens, ffn1_norm, ffn1_w_gate, ffn1_w_up, ffn1_w_down, mix_norm, w_in, mu_shift, w0, w_decay_up, a0, w_aaa_up, w_gate_up, k_k, k_a, r_k, gn_g, gn_b, q_norm, k_norm, b_f, w_a, w_b, w_o, ffn2_norm, ffn2_w_gate, ffn2_w_up, ffn2_w_down):
    depth = w_in.shape[0]
    bp, seq, _ = x_prompt.shape
    bs = x_sample.shape[0]
    l_tok = seq + N_META
    n_pad = (-l_tok) % LANES
    t_len = l_tok + n_pad
    meta = jnp.broadcast_to(meta_tokens[None].astype(F32), (bp, N_META, D_MODEL))
    h_p = jnp.concatenate([jnp.zeros((bp, n_pad, D_MODEL), F32), meta, x_prompt.astype(F32)], axis=1)
    h_p = h_p.reshape(bp * t_len, D_MODEL)
    h_s = x_sample.reshape(bs, D_MODEL).astype(F32)
    rows_p, rows_s = [], []
    for l in range(depth):
        w = _layer_weights(l, ffn1_norm, ffn1_w_gate, ffn1_w_up, ffn1_w_down, mix_norm, w_in, mu_shift, w0,
                           w_decay_up, a0, w_aaa_up, w_gate_up, k_k, k_a, r_k, gn_g, gn_b, q_norm, k_norm,
                           b_f, w_a, w_b, w_o, ffn2_norm, ffn2_w_gate, ffn2_w_up, ffn2_w_down)
        (h_p, zr, zq, kn, lf, s_t), (h_s, zr_s, kn_s, vn_s, lf_s, s_ts) = _layer(
            h_p, h_s, w, bp, t_len, n_pad, l == depth - 1, state_shift[l].astype(F32), state_wkv[l],
            page_table, cache_k[l], cache_v[l], cache_logf[l])
        real = lambda arr, n: arr.reshape(bp, t_len, n)[:, n_pad:]
        rows_p.append((real(kn, WIDTH).reshape(bp, l_tok, N_HEADS, HEAD_DIM),
                       real(zq, QKV_COLS)[:, :, 2 * WIDTH:].reshape(bp, l_tok, N_HEADS, HEAD_DIM),
                       real(lf, LANES)[:, :, :N_HEADS],
                       _from_blockdiag(s_t),
                       zr.reshape(bp, t_len, R_COLS)[:, -1]))
        rows_s.append((kn_s.reshape(bs, 1, N_HEADS, HEAD_DIM), vn_s.reshape(bs, 1, N_HEADS, HEAD_DIM),
                       lf_s[:, :N_HEADS].reshape(bs, 1, N_HEADS), s_ts, zr_s))
    y_prompt = h_p.reshape(bp, seq, D_MODEL)
    y_sample = h_s.reshape(bs, 1, D_MODEL)
    stk = lambda rows, i: jnp.stack([r[i] for r in rows], axis=0)
    return (y_prompt, y_sample,
            stk(rows_p, 0), stk(rows_p, 1), stk(rows_p, 2), stk(rows_p, 3), stk(rows_p, 4),
            stk(rows_s, 0), stk(rows_s, 1), stk(rows_s, 2), stk(rows_s, 3), stk(rows_s, 4))
```

```python
import functools

import numpy as np
import jax
import jax.numpy as jnp
from jax import lax
from jax.experimental import pallas as pl
from jax.experimental.pallas import tpu as pltpu

F32 = jnp.float32
BF16 = jnp.bfloat16

D_MODEL = 1024
HEAD_DIM = 64
N_HEADS = 8
WIDTH = N_HEADS * HEAD_DIM
N_META = 16
DECAY_LORA = 64
AAA_LORA = 64
GATE_LORA = 160
FFN_DIM = 2816
R_COLS = 3 * WIDTH + DECAY_LORA + AAA_LORA + GATE_LORA
QKV_COLS = 3 * WIDTH
NORM_EPS = 1e-6
GN_EPS = HEAD_DIM * 1e-5
NEG_INF = -1e30
PAGE = 128
LANES = 128
QUAD = 4 * HEAD_DIM
FFN_CHUNK = 256
VMEM_LIMIT = 56 * 1024 * 1024
REC_VMEM_LIMIT = 62 * 1024 * 1024


def _dot(a, b):
    return jnp.dot(a.astype(BF16), b.astype(BF16), preferred_element_type=F32)


def _dot_nt(a, b):
    return lax.dot_general(a.astype(BF16), b.astype(BF16), (((1,), (1,)), ((), ())),
                           preferred_element_type=F32)


def _dot_tn(a, b):
    return lax.dot_general(a.astype(BF16), b.astype(BF16), (((0,), (0,)), ((), ())),
                           preferred_element_type=F32)


def _split(x, n):
    parts = []
    r = x
    for i in range(n):
        p = r.astype(BF16)
        parts.append(p)
        if i + 1 < n:
            r = r - p.astype(F32)
    return parts


def _dot_x(a, b, dot=_dot, na=1, nb=1):
    pa = _split(a, na) if na > 1 else [a]
    pb = _split(b, nb) if nb > 1 else [b]
    out = None
    for i, x in enumerate(pa):
        for j, y in enumerate(pb):
            if i + j >= max(na, nb):
                continue
            t = dot(x, y)
            out = t if out is None else out + t
    return out


def _sigmoid(x):
    return 1.0 / (1.0 + jnp.exp(-x))


def _rms(x, g):
    ms = jnp.mean(x * x, axis=-1, keepdims=True)
    return x * lax.rsqrt(ms + NORM_EPS) * g


def _head_sum(x, ones_bd):
    return _dot_x(x, ones_bd, na=2)


def _params(sem, vmem=VMEM_LIMIT):
    return pltpu.CompilerParams(dimension_semantics=sem, vmem_limit_bytes=vmem)


def _resident(shape):
    nd = len(shape)
    return pl.BlockSpec(shape, lambda *_: (0,) * nd, pipeline_mode=pl.Buffered(1))


def _np_ones_bd(n, blk):
    i = np.arange(n)
    return (i[:, None] // blk == i[None, :] // blk).astype(np.float32)


def _ffn_kernel(x_ref, g_ref, wg_ref, wu_ref, wd_ref, o_ref, acc_ref):
    x = x_ref[...]
    xn = _rms(x, g_ref[...]).astype(BF16)
    for c in range(FFN_DIM // FFN_CHUNK):
        sl = slice(c * FFN_CHUNK, (c + 1) * FFN_CHUNK)
        gate = jnp.dot(xn, wg_ref[:, sl], preferred_element_type=F32)
        up = jnp.dot(xn, wu_ref[:, sl], preferred_element_type=F32)
        h = (gate * _sigmoid(gate) * up).astype(BF16)
        part = jnp.dot(h, wd_ref[sl, :], preferred_element_type=F32)
        if c == 0:
            acc_ref[...] = part
        else:
            acc_ref[...] += part
    o_ref[...] = x + 0.5 * acc_ref[...]


def _ffn(x, norm_g, w_gate, w_up, w_down, tm, scatter=None):
    m = x.shape[0]
    row = pl.BlockSpec((tm, D_MODEL), lambda i: (i, 0))
    out_row, m_out = row, m
    if scatter is not None:
        seq_len, start, m_out = scatter
        assert (seq_len - start) % tm == 0 and start % 8 == 0 and seq_len % 8 == 0
        per_seq = (seq_len - start) // tm
        out_row = pl.BlockSpec(
            (pl.Element(tm), pl.Element(D_MODEL)),
            lambda i: (pl.multiple_of((i // per_seq) * seq_len + start + (i % per_seq) * tm, 8), 0))
    return pl.pallas_call(
        _ffn_kernel,
        grid=(m // tm,),
        in_specs=[row, _resident((1, D_MODEL)), _resident((D_MODEL, FFN_DIM)),
                  _resident((D_MODEL, FFN_DIM)), _resident((FFN_DIM, D_MODEL))],
        out_specs=out_row,
        out_shape=jax.ShapeDtypeStruct((m_out, D_MODEL), F32),
        scratch_shapes=[pltpu.VMEM((tm, D_MODEL), F32)],
        compiler_params=_params(("parallel",)),
        name="ffn",
    )(x, norm_g, w_gate, w_up, w_down)


def _place_kernel(rows_ref, big_ref, o_ref):
    del big_ref
    o_ref[...] = rows_ref[...]


def _place_rows(big, rows, seq_len):
    n = rows.shape[0]
    assert n % 8 == 0 and seq_len % 8 == 0
    return pl.pallas_call(
        _place_kernel,
        grid=(big.shape[0] // seq_len,),
        in_specs=[_resident(rows.shape), pl.BlockSpec(memory_space=pl.ANY)],
        out_specs=pl.BlockSpec((pl.Element(n), pl.Element(D_MODEL)),
                               lambda b: (pl.multiple_of(b * seq_len, 8), 0)),
        out_shape=jax.ShapeDtypeStruct(big.shape, big.dtype),
        input_output_aliases={1: 0},
        compiler_params=_params(("arbitrary",)),
        name="place_rows",
    )(rows, big)


def _proj_kernel(x_ref, g_ref, wr_ref, wq_ref, wf_ref, wg_ref, zr_ref, zq_ref, zf_ref, zg_ref):
    xn = _rms(x_ref[...], g_ref[...]).astype(BF16)
    zr_ref[...] = jnp.dot(xn, wr_ref[...], preferred_element_type=F32)
    zq_ref[...] = jnp.dot(xn, wq_ref[...], preferred_element_type=F32)
    zf_ref[...] = jnp.dot(xn, wf_ref[...], preferred_element_type=F32)
    zg_ref[...] = jnp.dot(xn, wg_ref[...], preferred_element_type=F32)


def _proj(x, norm_g, w_r, w_q, w_f, w_g, tm):
    m = x.shape[0]
    row = lambda n: pl.BlockSpec((tm, n), lambda i: (i, 0))
    return pl.pallas_call(
        _proj_kernel,
        grid=(m // tm,),
        in_specs=[row(D_MODEL), _resident((1, D_MODEL)), _resident((D_MODEL, R_COLS)),
                  _resident((D_MODEL, QKV_COLS)), _resident((D_MODEL, LANES)),
                  _resident((D_MODEL, 2 * D_MODEL))],
        out_specs=[row(R_COLS), row(QKV_COLS), row(LANES), row(2 * D_MODEL)],
        out_shape=[jax.ShapeDtypeStruct((m, R_COLS), F32),
                   jax.ShapeDtypeStruct((m, QKV_COLS), F32),
                   jax.ShapeDtypeStruct((m, LANES), F32),
                   jax.ShapeDtypeStruct((m, 2 * D_MODEL), F32)],
        compiler_params=_params(("parallel",)),
        name="in_proj",
    )(x, norm_g, w_r, w_q, w_f, w_g)


def _prep_math(z, zprev, mu, w0, wdu, a0, wau, wgu, k_k, k_a, r_k, ones_bd, outs):
    r_ref, lw_ref, k_ref, v_ref, a_ref, b_ref, g_ref, bonus_ref = outs
    zs = z + (zprev - z) * mu
    r = zs[:, 0:WIDTH]
    k = zs[:, WIDTH:2 * WIDTH]
    v = zs[:, 2 * WIDTH:3 * WIDTH]
    o = 3 * WIDTH
    d_lo = zs[:, o:o + DECAY_LORA]
    a_lo = zs[:, o + DECAY_LORA:o + DECAY_LORA + AAA_LORA]
    g_lo = zs[:, o + DECAY_LORA + AAA_LORA:R_COLS]
    wpre = w0 + _dot(jnp.tanh(d_lo), wdu)
    y = -wpre
    softplus = jnp.maximum(y, 0.0) + jnp.log(1.0 + jnp.exp(-jnp.abs(y)))
    w_log = -softplus - 0.5
    lw_ref[...] = -jnp.exp(w_log)
    a = _sigmoid(a0 + _dot(a_lo, wau))
    g_ref[...] = _dot(_sigmoid(g_lo), wgu)
    kk = k * k_k
    k2 = k * (1.0 + (a - 1.0) * k_a)
    kk = kk * lax.rsqrt(_head_sum(kk * kk, ones_bd) + 1e-12)
    r_ref[...] = r
    k_ref[...] = k2
    v_ref[...] = v
    a_ref[...] = -kk
    b_ref[...] = kk * a
    bonus_ref[...] = _head_sum(r * k2 * r_k, ones_bd) * v


def _prep_prompt_kernel(seq_len, z_ref, zp_ref, mu, w0, wdu, a0, wau, wgu, k_k, k_a, r_k, ones_bd, *outs):
    z = z_ref[...]
    tm = z.shape[0]
    rolled = pltpu.roll(z, 1, 0)
    row = lax.broadcasted_iota(jnp.int32, (tm, 1), 0)
    zprev = jnp.where(row == 0, zp_ref[7:8, :], rolled)
    grow = row + pl.program_id(0) * tm
    zprev = jnp.where(grow % seq_len == 0, 0.0, zprev)
    _prep_math(z, zprev, mu[...], w0[...], wdu[...], a0[...], wau[...], wgu[...], k_k[...], k_a[...],
               r_k[...], ones_bd[...], outs)


def _prep_sample_kernel(z_ref, zp_ref, mu, w0, wdu, a0, wau, wgu, k_k, k_a, r_k, ones_bd, *outs):
    _prep_math(z_ref[...], zp_ref[...], mu[...], w0[...], wdu[...], a0[...], wau[...], wgu[...],
               k_k[...], k_a[...], r_k[...], ones_bd[...], outs)


def _prep(zr, zprev, pw, tm, seq_len):
    m = zr.shape[0]
    row = lambda n: pl.BlockSpec((tm, n), lambda i: (i, 0))
    if zprev is None:
        body = functools.partial(_prep_prompt_kernel, seq_len)
        prev_spec = pl.BlockSpec((8, R_COLS), lambda i: (jnp.maximum(i * (tm // 8) - 1, 0), 0))
        zprev = zr
    else:
        body = _prep_sample_kernel
        prev_spec = row(R_COLS)
    consts = [pw["mu"], pw["w0"], pw["wdu"], pw["a0"], pw["wau"], pw["wgu"], pw["k_k"], pw["k_a"],
              pw["r_k"], pw["ones_bd"]]
    return pl.pallas_call(
        body,
        grid=(m // tm,),
        in_specs=[row(R_COLS), prev_spec] + [_resident(c.shape) for c in consts],
        out_specs=[row(WIDTH)] * 8,
        out_shape=[jax.ShapeDtypeStruct((m, WIDTH), F32)] * 8,
        compiler_params=_params(("parallel",)),
        name="rwkv_prep",
    )(zr, zprev, *consts)


def _rec_masks(c):
    t = np.arange(c)[:, None]
    hs = np.arange(4 * c)[None, :]
    s = hs % c
    strict = (s < t).astype(np.float32)
    incl = (s <= t).astype(np.float32)
    eye = (s == t).astype(np.float32)
    rows = np.arange(4 * c)[:, None]
    bd = (rows // c == hs // c).astype(np.float32)
    tri = (np.arange(c)[None, :] <= np.arange(c)[:, None]).astype(np.float32)
    lane = np.arange(QUAD)[None, :] // HEAD_DIM
    head = (lane == rows // c).astype(np.float32)
    sbd = _np_ones_bd(QUAD, HEAD_DIM)
    return strict, incl, eye, bd, tri, head, sbd


def _stack_heads(x, head):
    return jnp.concatenate([x.astype(BF16)] * 4, axis=0) * head


def _rec_kernel(c, first_chunk, n_doubling, group, r_ref, lw_ref, k_ref, v_ref, a_ref, b_ref, s0_ref,
                strict_ref, incl_ref, eye_ref, bd_ref, tri_ref, head_ref, sbd_ref,
                y_ref, sT_ref, s_ref, ry_scr, yc_scr, tr_scr, cs_scr, wl_scr):
    t_len = r_ref.shape[0]
    n_iter = t_len // c - first_chunk
    if first_chunk > 0:
        y_ref[0:first_chunk * c, :] = jnp.zeros((first_chunk * c, QUAD), F32)

    def to_bd(p_row):
        return jnp.concatenate([p_row.astype(BF16)] * 4, axis=0) * bd_ref[...]

    def chunk_rows(ci):
        return pl.ds(pl.multiple_of(ci * c, c), c)

    def prepare(gi, carry, tick=lambda: None):
        strict = strict_ref[...]
        incl2 = jnp.concatenate([incl_ref[...]] * 2, axis=1)
        head = head_ref[...]
        slots = [gi * group + j for j in range(group)]
        loaded = []
        for slot in slots:
            rows = chunk_rows(slot + first_chunk)
            loaded.append((r_ref[rows, :], lw_ref[rows, :], k_ref[rows, :], v_ref[rows, :],
                           a_ref[rows, :], b_ref[rows, :]))
        cums = [_dot_x(tri_ref[...], ld[1], nb=3) for ld in loaded]
        tick()
        work = []
        for slot, (r, lw, k, v, a, b), cum in zip(slots, loaded, cums):
            cum_last = cum[c - 1:c, :]
            e_inv = jnp.exp(-cum)
            at = (a * jnp.exp(cum - lw)).astype(BF16)
            rt = r * jnp.exp(cum)
            e_last = jnp.exp(cum_last - cum)
            wl_scr[slot] = jnp.broadcast_to(jnp.exp(cum_last), (8, QUAD))
            rhs = jnp.concatenate([_stack_heads(b * e_inv, head), _stack_heads(k * e_inv, head)], axis=0)
            gmat = _dot_nt(jnp.concatenate([at, rt.astype(BF16)], axis=0), rhs)
            work.append(dict(at=at, rt=rt, v=v, bc=(b * e_last).astype(BF16), kc=(k * e_last).astype(BF16),
                             gmat=gmat))
        tick()
        p_rows, t_rows = [], []
        for w in work:
            gmat = w.pop("gmat")
            n_row = gmat[0:c, 0:4 * c] * strict
            w["v_st"] = _stack_heads(w["v"], head)
            w["xv"] = _dot(gmat[0:c, 4 * c:8 * c] * strict, w["v_st"])
            w["n_r"] = (gmat[c:2 * c, :] * incl2).astype(BF16)
            p_rows.append(n_row)
            t_rows.append(eye_ref[...] + n_row)
        tick()
        for _ in range(n_doubling):
            t_bds = [to_bd(t) for t in t_rows]
            p_rows = [_dot(p, to_bd(p)) for p in p_rows]
            tick()
            t_rows = [t + _dot(p, t_bd) for t, p, t_bd in zip(t_rows, p_rows, t_bds)]
        tick()
        for slot, w, t_row in zip(slots, work, t_rows):
            both = jnp.concatenate([_stack_heads(w["at"], head), _stack_heads(w["xv"], head)], axis=1)
            t_both = _dot(t_row, both)
            ta, txv = t_both[:, 0:QUAD], t_both[:, QUAD:]
            n_rb = w["n_r"][:, 0:4 * c]
            ry_scr[slot] = (w["rt"] + _dot(n_rb, _stack_heads(ta, head))).astype(BF16)
            yc_scr[slot] = _dot(w["n_r"], jnp.concatenate([_stack_heads(txv, head), w["v_st"]],
                                                          axis=0))
            tr_scr[slot] = (_dot_tn(ta, w["bc"]) * sbd_ref[...]).astype(BF16)
            cs_scr[slot] = _dot_tn(jnp.concatenate([txv, w["v"]], axis=0),
                                   jnp.concatenate([w["bc"], w["kc"]], axis=0)) * sbd_ref[...]
        return carry

    s_ref[...] = s0_ref[...]

    def advance(slot):
        rows = chunk_rows(slot + first_chunk)
        s_old = s_ref[...]
        s_bf = s_old.astype(BF16)
        y_ref[rows, :] = _dot_nt(ry_scr[slot], s_bf) + yc_scr[slot]
        s_ref[...] = s_old * wl_scr[slot][0:1, :] + _dot(s_bf, tr_scr[slot]) + cs_scr[slot]

    def both_passes(gi, carry):
        pending = list(range(group))

        def tick():
            if pending:
                advance((gi - 1) * group + pending.pop(0))

        prepare(gi, carry, tick)
        while pending:
            tick()
        return carry

    n_groups = n_iter // group
    prepare(0, 0)
    lax.fori_loop(1, n_groups, both_passes, 0)
    for j in range(group):
        advance((n_groups - 1) * group + j)
    sT_ref[...] = s_ref[...]


def _rec(arrs, s0, c, first_chunk):
    bsz, t_len, _ = arrs[0].shape
    strict, incl, eye, bd, tri, head, sbd = (jnp.asarray(m) for m in _rec_masks(c))
    masks = [strict, incl, eye, bd.astype(BF16), tri, head.astype(BF16), sbd]
    n_doubling = max(int(np.ceil(np.log2(c))) - 1, 0)
    n_iter = t_len // c - first_chunk
    group = next(g for g in (11, 3, 1) if n_iter % g == 0)
    seq = pl.BlockSpec((None, t_len, QUAD), lambda b, q: (b, 0, q))
    st = pl.BlockSpec((None, None, QUAD, QUAD), lambda b, q: (b, q, 0, 0))
    return pl.pallas_call(
        functools.partial(_rec_kernel, c, first_chunk, n_doubling, group),
        grid=(bsz, 2),
        in_specs=[seq] * 6 + [st] + [_resident(m.shape) for m in masks],
        out_specs=[seq, st],
        out_shape=[jax.ShapeDtypeStruct((bsz, t_len, WIDTH), F32),
                   jax.ShapeDtypeStruct((bsz, 2, QUAD, QUAD), F32)],
        scratch_shapes=[pltpu.VMEM((QUAD, QUAD), F32),
                        pltpu.VMEM((n_iter, c, QUAD), BF16), pltpu.VMEM((n_iter, c, QUAD), F32),
                        pltpu.VMEM((n_iter, QUAD, QUAD), BF16), pltpu.VMEM((n_iter, QUAD, QUAD), F32),
                        pltpu.VMEM((n_iter, 8, QUAD), F32)],
        compiler_params=_params(("parallel", "parallel"), vmem=REC_VMEM_LIMIT),
        name="rwkv_rec",
    )(*arrs, s0, *masks)


def _step_kernel(tok_ref, s_ref, y_ref, so_ref):
    s = s_ref[...]
    r, lw, k, v, a, b = (tok_ref[:, i] for i in range(6))
    eye = (lax.broadcasted_iota(jnp.int32, (HEAD_DIM, HEAD_DIM), 0)
           == lax.broadcasted_iota(jnp.int32, (HEAD_DIM, HEAD_DIM), 1)).astype(F32)
    u = jnp.sum(s * a, axis=-1, keepdims=True)
    v_col = jnp.sum(eye * v, axis=-1, keepdims=True)
    s_new = s * jnp.exp(lw) + u * b + v_col * k
    so_ref[...] = s_new
    y_col = jnp.sum(s_new * r, axis=-1, keepdims=True)
    y_ref[...] = jnp.sum(y_col * eye, axis=-2, keepdims=True)


def _step(tok, state, nb):
    bsz = state.shape[0]
    return pl.pallas_call(
        _step_kernel,
        grid=(bsz // nb,),
        in_specs=[pl.BlockSpec((nb, 6, N_HEADS, 1, HEAD_DIM), lambda i: (i, 0, 0, 0, 0)),
                  pl.BlockSpec((nb, N_HEADS, HEAD_DIM, HEAD_DIM), lambda i: (i, 0, 0, 0))],
        out_specs=[pl.BlockSpec((nb, N_HEADS, 1, HEAD_DIM), lambda i: (i, 0, 0, 0)),
                   pl.BlockSpec((nb, N_HEADS, HEAD_DIM, HEAD_DIM), lambda i: (i, 0, 0, 0))],
        out_shape=[jax.ShapeDtypeStruct((bsz, N_HEADS, 1, HEAD_DIM), F32),
                   jax.ShapeDtypeStruct((bsz, N_HEADS, HEAD_DIM, HEAD_DIM), F32)],
        compiler_params=_params(("parallel",)),
        name="rwkv_step",
    )(tok, state)


def _foxprep_kernel(n_pad, with_cumsum, q_ref, k_ref, f_ref, qn_ref, kn_ref, bf_ref, ones_bd, tri_ref,
                    qo_ref, ko_ref, lf_ref, *cum_refs):
    q = q_ref[...]
    k = k_ref[...]
    inv = 1.0 / HEAD_DIM
    qo_ref[...] = (q * lax.rsqrt(_head_sum(q * q, ones_bd[...]) * inv + NORM_EPS) * qn_ref[...]
                   * (HEAD_DIM ** -0.5))
    ko_ref[...] = k * lax.rsqrt(_head_sum(k * k, ones_bd[...]) * inv + NORM_EPS) * kn_ref[...]
    x = f_ref[...] + bf_ref[...]
    lf = jnp.minimum(x, 0.0) - jnp.log(1.0 + jnp.exp(-jnp.abs(x)))
    lane = lax.broadcasted_iota(jnp.int32, lf.shape, 1)
    lf = jnp.where(lane < N_HEADS, lf, 0.0)
    if with_cumsum:
        row = lax.broadcasted_iota(jnp.int32, lf.shape, 0)
        lf = jnp.where(row >= n_pad, lf, 0.0)
    lf_ref[...] = lf
    if with_cumsum:
        (c_ref,) = cum_refs
        t_len = lf.shape[0]
        carry = jnp.zeros((1, LANES), F32)
        for blk in range(t_len // LANES):
            rows = slice(blk * LANES, (blk + 1) * LANES)
            cb = _dot_x(tri_ref[...], lf[rows, :], nb=3) + carry
            c_ref[rows, :] = cb
            carry = cb[LANES - 1:LANES, :]


def _foxprep(zq, zf, qn_w, kn_w, bf_row, ones_bd, rows_per_step, n_steps, n_pad, with_cumsum):
    m = zq.shape[0]
    tm = rows_per_step
    tri = jnp.asarray(np.tril(np.ones((LANES, LANES), np.float32)))
    col = lambda j: pl.BlockSpec((tm, WIDTH), lambda i: (i, j))
    row = lambda n: pl.BlockSpec((tm, n), lambda i: (i, 0))
    out_specs = [row(WIDTH), row(WIDTH), row(LANES)]
    out_shape = [jax.ShapeDtypeStruct((m, WIDTH), F32), jax.ShapeDtypeStruct((m, WIDTH), F32),
                 jax.ShapeDtypeStruct((m, LANES), F32)]
    if with_cumsum:
        out_specs += [row(LANES)]
        out_shape += [jax.ShapeDtypeStruct((m, LANES), F32)]
    return pl.pallas_call(
        functools.partial(_foxprep_kernel, n_pad, with_cumsum),
        grid=(n_steps,),
        in_specs=[col(0), col(1), row(LANES), _resident((1, WIDTH)), _resident((1, WIDTH)),
                  _resident((1, LANES)), _resident((WIDTH, WIDTH)), _resident((LANES, LANES))],
        out_specs=out_specs,
        out_shape=out_shape,
        compiler_params=_params(("parallel",)),
        name="fox_prep",
    )(zq, zq, zf, qn_w, kn_w, bf_row, ones_bd, tri)


def _foxattn_kernel(n_pad, n_qtiles, q_ref, k_ref, v_ref, c_ref, o_ref, qa_ref, ka_ref):
    t_len = q_ref.shape[0]
    tq = t_len // n_qtiles
    pair = pl.program_id(1)
    lane = lax.broadcasted_iota(jnp.int32, (1, LANES), 1)
    row = lax.broadcasted_iota(jnp.int32, (t_len, 1), 0)
    causal = (lax.broadcasted_iota(jnp.int32, (tq, tq), 1) <= lax.broadcasted_iota(jnp.int32, (tq, tq), 0))
    for h in range(2):
        hmask = (lane // HEAD_DIM == h).astype(F32)
        sel = (lane == 2 * pair + h).astype(F32)
        c = jnp.sum(c_ref[...] * sel, axis=-1, keepdims=True)
        hi = c.astype(BF16).astype(F32)
        mid = (c - hi).astype(BF16).astype(F32)
        lo = (c - hi - mid).astype(BF16).astype(F32)
        base = HEAD_DIM * (1 - h)
        at = lambda i: (lane == base + i).astype(F32)
        ones_q = at(3) + at(4) + at(5)
        ones_k = at(0) + at(1) + at(2)
        hi_k = jnp.where(row >= n_pad, hi, -NEG_INF)
        qa_ref[...] = (q_ref[...] * hmask + hi * at(0) + mid * at(1) + lo * at(2) + ones_q).astype(BF16)
        ka_ref[...] = (k_ref[...] * hmask + ones_k - hi_k * at(3) - mid * at(4) - lo * at(5)).astype(BF16)
        for qi in range(n_qtiles):
            q0 = qi * tq
            qa = qa_ref[q0:q0 + tq, :]
            s_diag = jnp.where(causal, _dot_nt(qa, ka_ref[q0:q0 + tq, :]), NEG_INF)
            m = jnp.max(s_diag, axis=-1, keepdims=True)
            if qi > 0:
                s_past = _dot_nt(qa, ka_ref[0:q0, :])
                m = jnp.maximum(m, jnp.max(s_past, axis=-1, keepdims=True))
            p_diag = jnp.exp(s_diag - m)
            l = jnp.sum(p_diag, axis=-1, keepdims=True)
            out = _dot(p_diag, v_ref[q0:q0 + tq, :])
            if qi > 0:
                p_past = jnp.exp(s_past - m)
                l = l + jnp.sum(p_past, axis=-1, keepdims=True)
                out = out + _dot(p_past, v_ref[0:q0, :])
            out = out / l * hmask
            if h == 0:
                o_ref[q0:q0 + tq, :] = out
            else:
                o_ref[q0:q0 + tq, :] += out


def _foxattn(qn, kn, zq, cum, bsz, t_len, n_pad, n_qtiles):
    seq = lambda j0: pl.BlockSpec((t_len, LANES), lambda b, p: (b, j0 + p))
    return pl.pallas_call(
        functools.partial(_foxattn_kernel, n_pad, n_qtiles),
        grid=(bsz, N_HEADS // 2),
        in_specs=[seq(0), seq(0), seq(2 * WIDTH // LANES),
                  pl.BlockSpec((t_len, LANES), lambda b, p: (b, 0))],
        out_specs=seq(0),
        out_shape=jax.ShapeDtypeStruct((bsz * t_len, WIDTH), F32),
        scratch_shapes=[pltpu.VMEM((t_len, LANES), BF16), pltpu.VMEM((t_len, LANES), BF16)],
        compiler_params=_params(("parallel", "parallel")),
        name="fox_attn",
    )(qn, kn, zq, cum)


def _dec_consts():
    t = np.arange(PAGE)
    suf = (t[:, None] > t[None, :]).astype(np.float32)
    bdm = (np.arange(WIDTH)[None, :] // HEAD_DIM == np.arange(8)[:, None]).astype(np.float32)
    eye = (np.arange(LANES)[None, :] == np.arange(8)[:, None]).astype(np.float32)
    return suf, bdm, eye


def _dec_kernel(n_group, pt_ref, q_ref, kn_ref, vn_ref, lfn_ref, suf_ref, bdm_ref, eye_ref, *rest):
    k_refs = rest[0:n_group]
    v_refs = rest[n_group:2 * n_group]
    lf_refs = rest[2 * n_group:3 * n_group]
    o_ref = rest[3 * n_group]
    m_ref, l_ref, acc_ref, carry_ref = rest[3 * n_group + 1:]
    j = pl.program_id(1)
    bdm = bdm_ref[...]
    qbd = q_ref[...] * bdm
    lf_new = jnp.sum(eye_ref[...] * lfn_ref[...], axis=-1, keepdims=True)

    @pl.when(j == 0)
    def _():
        m_ref[...] = jnp.broadcast_to(jnp.sum(qbd * kn_ref[...], axis=-1, keepdims=True), (8, LANES))
        l_ref[...] = jnp.ones((8, LANES), F32)
        acc_ref[...] = vn_ref[...] * bdm
        carry_ref[...] = jnp.zeros((8, LANES), F32)

    carry = carry_ref[:, 0:1]
    qb = qbd.astype(BF16)
    scores = []
    for g in range(n_group):
        lf_t = lf_refs[g][...]
        suffix = _dot_x(lf_t, suf_ref[...], na=3)
        kt = k_refs[g][...].reshape(WIDTH, PAGE)
        scores.append(_dot(qb, kt) + (lf_new + carry + suffix))
        carry = carry + jnp.sum(lf_t, axis=-1, keepdims=True)
    carry_ref[...] = jnp.broadcast_to(carry, (8, LANES))
    s_all = jnp.concatenate(scores, axis=1)
    m_old = m_ref[:, 0:1]
    m_new = jnp.maximum(m_old, jnp.max(s_all, axis=-1, keepdims=True))
    alpha = jnp.exp(m_old - m_new)
    p = jnp.exp(s_all - m_new)
    l_ref[...] = jnp.broadcast_to(alpha * l_ref[:, 0:1] + jnp.sum(p, axis=-1, keepdims=True), (8, LANES))
    m_ref[...] = jnp.broadcast_to(m_new, (8, LANES))
    acc = alpha * acc_ref[...]
    pb = p.astype(BF16)
    for g in range(n_group):
        vt = v_refs[g][...].reshape(WIDTH, PAGE)
        acc = acc + _dot_nt(pb[:, g * PAGE:(g + 1) * PAGE], vt)
    acc_ref[...] = acc

    @pl.when(j == pl.num_programs(1) - 1)
    def _():
        o_ref[...] = jnp.sum(acc_ref[...] * bdm / l_ref[:, 0:1], axis=0, keepdims=True)


def _dec(page_table, qn, kn, vn, lfn, cache_k, cache_v, cache_lf, n_group):
    bsz, n_pages = page_table.shape
    kt = jnp.transpose(cache_k, (0, 2, 3, 1))
    vt = jnp.transpose(cache_v, (0, 2, 3, 1))
    lft = jnp.transpose(cache_lf, (0, 2, 1))
    consts = [jnp.asarray(c) for c in _dec_consts()]
    n_steps = n_pages // n_group
    tok = lambda n: pl.BlockSpec((None, 1, n), lambda b, j, pt: (b, 0, 0))
    const = lambda c: pl.BlockSpec(c.shape, lambda b, j, pt: (0,) * c.ndim, pipeline_mode=pl.Buffered(1))

    def page(shape, g):
        def index(b, j, pt):
            return (pt[b * n_pages + (n_pages - 1 - (j * n_group + g))],) + (0,) * len(shape)
        return pl.BlockSpec((None,) + shape, index)

    grid_spec = pltpu.PrefetchScalarGridSpec(
        num_scalar_prefetch=1,
        grid=(bsz, n_steps),
        in_specs=([tok(WIDTH), tok(WIDTH), tok(WIDTH), tok(LANES)] + [const(c) for c in consts]
                  + [page((N_HEADS, HEAD_DIM, PAGE), g) for g in range(n_group)]
                  + [page((N_HEADS, HEAD_DIM, PAGE), g) for g in range(n_group)]
                  + [page((N_HEADS, PAGE), g) for g in range(n_group)]),
        out_specs=tok(WIDTH),
        scratch_shapes=[pltpu.VMEM((8, LANES), F32), pltpu.VMEM((8, LANES), F32),
                        pltpu.VMEM((8, WIDTH), F32), pltpu.VMEM((8, LANES), F32)],
    )
    r3 = lambda x: x.reshape(bsz, 1, x.shape[-1])
    return pl.pallas_call(
        functools.partial(_dec_kernel, n_group),
        grid_spec=grid_spec,
        out_shape=jax.ShapeDtypeStruct((bsz, 1, WIDTH), F32),
        compiler_params=_params(("parallel", "arbitrary")),
        name="fox_decode",
    )(page_table.reshape(-1), r3(qn), r3(kn), r3(vn), r3(lfn), *consts,
      *([kt] * n_group), *([vt] * n_group), *([lft] * n_group)).reshape(bsz, WIDTH)


def _merge_kernel(x_ref, y_ref, bonus_ref, g_ref, fox_ref, zg_ref, gng_ref, gnb_ref, ones_bd,
                  wa_ref, wb_ref, wo_ref, o_ref):
    y = y_ref[...]
    inv = 1.0 / HEAD_DIM
    mean = _head_sum(y, ones_bd[...]) * inv
    d = y - mean
    var = _head_sum(d * d, ones_bd[...]) * inv
    yn = d * lax.rsqrt(var + GN_EPS) * gng_ref[...] + gnb_ref[...]
    rw = (yn + bonus_ref[...]) * g_ref[...]
    zg = zg_ref[...]
    merged = (_sigmoid(zg[:, 0:D_MODEL]) * _dot(rw, wa_ref[...])
              + _sigmoid(zg[:, D_MODEL:]) * _dot(fox_ref[...], wb_ref[...]))
    o_ref[...] = x_ref[...] + _dot(merged, wo_ref[...])


def _merge(x1, y, bonus, g, fox, zg, gn_g, gn_b, ones_bd, w_a, w_b, w_o, tm, keep=None):
    m = x1.shape[0]
    out_row = pl.BlockSpec((tm, D_MODEL), lambda i: (i, 0))
    if keep is None:
        m_out = m
        row = lambda n: pl.BlockSpec((tm, n), lambda i: (i, 0))
    else:
        seq_len, start = keep
        assert (seq_len - start) % tm == 0 and start % 8 == 0
        per_seq = (seq_len - start) // tm
        m_out = (m // seq_len) * per_seq * tm
        row = lambda n: pl.BlockSpec(
            (pl.Element(tm), pl.Element(n)),
            lambda i: (pl.multiple_of((i // per_seq) * seq_len + start + (i % per_seq) * tm, 8), 0))
    return pl.pallas_call(
        _merge_kernel,
        grid=(m_out // tm,),
        in_specs=[row(D_MODEL), row(WIDTH), row(WIDTH), row(WIDTH), row(WIDTH), row(2 * D_MODEL),
                  _resident((1, WIDTH)), _resident((1, WIDTH)), _resident((WIDTH, WIDTH)),
                  _resident((WIDTH, D_MODEL)), _resident((WIDTH, D_MODEL)), _resident((D_MODEL, D_MODEL))],
        out_specs=out_row,
        out_shape=jax.ShapeDtypeStruct((m_out, D_MODEL), F32),
        compiler_params=_params(("parallel",)),
        name="merge",
    )(x1, y, bonus, g, fox, zg, gn_g, gn_b, ones_bd, w_a, w_b, w_o)


def _to_blockdiag(state):
    bsz = state.shape[0]
    s = state.reshape(bsz, 2, 4, HEAD_DIM, HEAD_DIM)
    eye = jnp.eye(4, dtype=state.dtype)
    out = jnp.einsum("bqhvk,hg->bqhvgk", s, eye)
    return out.reshape(bsz, 2, QUAD, QUAD)


def _from_blockdiag(sbd):
    bsz = sbd.shape[0]
    s = sbd.reshape(bsz, 2, 4, HEAD_DIM, 4, HEAD_DIM)
    idx = jnp.arange(4)
    s = s[:, :, idx, :, idx, :]
    return jnp.moveaxis(s, 0, 2).reshape(bsz, N_HEADS, HEAD_DIM, HEAD_DIM)


def _layer_weights(l, ffn1_norm, ffn1_w_gate, ffn1_w_up, ffn1_w_down, mix_norm, w_in, mu_shift, w0,
                   w_decay_up, a0, w_aaa_up, w_gate_up, k_k, k_a, r_k, gn_g, gn_b, q_norm, k_norm, b_f,
                   w_a, w_b, w_o, ffn2_norm, ffn2_w_gate, ffn2_w_up, ffn2_w_down):
    row = lambda x: x.reshape(1, -1).astype(F32)
    bf = lambda x: x.astype(BF16)
    wi = w_in[l]
    w_f = jnp.pad(wi[:, R_COLS + QKV_COLS:R_COLS + QKV_COLS + N_HEADS], ((0, 0), (0, LANES - N_HEADS)))
    return dict(
        ffn1=(row(ffn1_norm[l]), bf(ffn1_w_gate[l]), bf(ffn1_w_up[l]), bf(ffn1_w_down[l])),
        ffn2=(row(ffn2_norm[l]), bf(ffn2_w_gate[l]), bf(ffn2_w_up[l]), bf(ffn2_w_down[l])),
        mix_norm=row(mix_norm[l]),
        w_r=bf(wi[:, 0:R_COLS]), w_q=bf(wi[:, R_COLS:R_COLS + QKV_COLS]), w_f=bf(w_f),
        w_g=bf(wi[:, R_COLS + QKV_COLS + N_HEADS:]),
        prep=dict(mu=row(mu_shift[l]), w0=row(w0[l]), wdu=bf(w_decay_up[l]), a0=row(a0[l]),
                  wau=bf(w_aaa_up[l]), wgu=bf(w_gate_up[l]), k_k=row(k_k[l]), k_a=row(k_a[l]),
                  r_k=row(r_k[l]), ones_bd=jnp.asarray(_np_ones_bd(WIDTH, HEAD_DIM), BF16)),
        gn_g=row(gn_g[l]), gn_b=row(gn_b[l]),
        q_norm=row(jnp.tile(q_norm[l], N_HEADS)), k_norm=row(jnp.tile(k_norm[l], N_HEADS)),
        b_f=jnp.pad(row(b_f[l]), ((0, 0), (0, LANES - N_HEADS))),
        w_a=bf(w_a[l]), w_b=bf(w_b[l]), w_o=bf(w_o[l]),
    )


def _prompt_layer(x1, w, bsz, t_len, n_pad, last):
    ones_bd = w["prep"]["ones_bd"]
    zr, zq, zf, zg = _proj(x1, w["mix_norm"], w["w_r"], w["w_q"], w["w_f"], w["w_g"], tm=272)
    r, lw, k, v, a, b, g, bonus = _prep(zr, None, w["prep"], tm=544, seq_len=t_len)
    seq = lambda arr: arr.reshape(bsz, t_len, WIDTH)
    s0 = jnp.zeros((bsz, 2, QUAD, QUAD), F32)
    y, s_t = _rec([seq(r), seq(lw), seq(k), seq(v), seq(a), seq(b)], s0, c=64, first_chunk=n_pad // 64)
    qn, kn, lf, cum = _foxprep(zq, zf, w["q_norm"], w["k_norm"], w["b_f"], ones_bd,
                               rows_per_step=t_len, n_steps=bsz, n_pad=n_pad, with_cumsum=True)
    fox = _foxattn(qn, kn, zq, cum, bsz, t_len, n_pad, n_qtiles=4)
    x2 = _merge(x1, y.reshape(bsz * t_len, WIDTH), bonus, g, fox, zg, w["gn_g"], w["gn_b"], ones_bd,
                w["w_a"], w["w_b"], w["w_o"], tm=256 if last else 272,
                keep=(t_len, n_pad + N_META) if last else None)
    x3 = _ffn(x2, *w["ffn2"], tm=512 if last else 544)
    return x3, zr, zq, kn, lf, s_t


def _sample_layer(x1, w, shift_prev, wkv0, page_table, cache_k, cache_v, cache_lf):
    bsz = x1.shape[0]
    ones_bd = w["prep"]["ones_bd"]
    zr, zq, zf, zg = _proj(x1, w["mix_norm"], w["w_r"], w["w_q"], w["w_f"], w["w_g"], tm=bsz)
    prepped = _prep(zr, shift_prev, w["prep"], tm=bsz, seq_len=1)
    g, bonus = prepped[6], prepped[7]
    tok = jnp.stack(prepped[:6], axis=1).reshape(bsz, 6, N_HEADS, 1, HEAD_DIM)
    y, s_t = _step(tok, wkv0.astype(F32), nb=8)
    y = y.reshape(bsz, WIDTH)
    qn, kn, lf = _foxprep(zq, zf, w["q_norm"], w["k_norm"], w["b_f"], ones_bd,
                          rows_per_step=bsz, n_steps=1, n_pad=0, with_cumsum=False)
    vn = zq[:, 2 * WIDTH:]
    fox = _dec(page_table, qn, kn, vn, lf, cache_k, cache_v, cache_lf, n_group=32)
    x2 = _merge(x1, y, bonus, g, fox, zg, w["gn_g"], w["gn_b"], ones_bd, w["w_a"], w["w_b"], w["w_o"],
                tm=bsz)
    x3 = _ffn(x2, *w["ffn2"], tm=bsz)
    return x3, zr, kn, vn, lf, s_t


def kernel(x_prompt, x_sample, cache_k, cache_v, cache_logf, state_wkv, state_shift, page_table, meta_tokens, ffn1_norm, ffn1_w_gate, ffn1_w_up, ffn1_w_down, mix_norm, w_in, mu_shift, w0, w_decay_up, a0, w_aaa_up, w_gate_up, k_k, k_a, r_k, gn_g, gn_b, q_norm, k_norm, b_f, w_a, w_b, w_o, ffn2_norm, ffn2_w_gate, ffn2_w_up, ffn2_w_down):
    depth = w_in.shape[0]
    bp, seq, _ = x_prompt.shape
    bs = x_sample.shape[0]
    l_tok = seq + N_META
    n_pad = (-l_tok) % LANES
    t_len = l_tok + n_pad
    head = jnp.concatenate([jnp.zeros((n_pad, D_MODEL), F32), meta_tokens.astype(F32)], axis=0)
    h_p = x_prompt.reshape(bp * seq, D_MODEL).astype(F32)
    h_s = x_sample.reshape(bs, D_MODEL).astype(F32)
    rows_p, rows_s = [], []
    for l in range(depth):
        w = _layer_weights(l, ffn1_norm, ffn1_w_gate, ffn1_w_up, ffn1_w_down, mix_norm, w_in, mu_shift, w0,
                           w_decay_up, a0, w_aaa_up, w_gate_up, k_k, k_a, r_k, gn_g, gn_b, q_norm, k_norm,
                           b_f, w_a, w_b, w_o, ffn2_norm, ffn2_w_gate, ffn2_w_up, ffn2_w_down)
        if l == 0:
            small = _ffn(jnp.concatenate([h_s, head], axis=0), *w["ffn1"], tm=bs + n_pad + N_META)
            x1_s, head1 = small[:bs], small[bs:]
            x1_p = _ffn(h_p, *w["ffn1"], tm=512, scatter=(t_len, n_pad + N_META, bp * t_len))
            x1_p = _place_rows(x1_p, head1, t_len)
        else:
            x1_s = _ffn(h_s, *w["ffn1"], tm=bs)
            x1_p = _ffn(h_p, *w["ffn1"], tm=544)
        h_p, zr, zq, kn, lf, s_t = _prompt_layer(x1_p, w, bp, t_len, n_pad, last=(l == depth - 1))
        real = lambda arr, n: arr.reshape(bp, t_len, n)[:, n_pad:]
        rows_p.append((real(kn, WIDTH).reshape(bp, l_tok, N_HEADS, HEAD_DIM),
                       real(zq, QKV_COLS)[:, :, 2 * WIDTH:].reshape(bp, l_tok, N_HEADS, HEAD_DIM),
                       real(lf, LANES)[:, :, :N_HEADS],
                       _from_blockdiag(s_t),
                       zr.reshape(bp, t_len, R_COLS)[:, -1]))
        h_s, zr_s, kn_s, vn_s, lf_s, s_ts = _sample_layer(
            x1_s, w, state_shift[l].astype(F32), state_wkv[l], page_table, cache_k[l], cache_v[l],
            cache_logf[l])
        rows_s.append((kn_s.reshape(bs, 1, N_HEADS, HEAD_DIM), vn_s.reshape(bs, 1, N_HEADS, HEAD_DIM),
                       lf_s[:, :N_HEADS].reshape(bs, 1, N_HEADS), s_ts, zr_s))
    y_prompt = h_p.reshape(bp, seq, D_MODEL)
    y_sample = h_s.reshape(bs, 1, D_MODEL)
    stk = lambda rows, i: jnp.stack([r[i] for r in rows], axis=0)
    return (y_prompt, y_sample,
            stk(rows_p, 0), stk(rows_p, 1), stk(rows_p, 2), stk(rows_p, 3), stk(rows_p, 4),
            stk(rows_s, 0), stk(rows_s, 1), stk(rows_s, 2), stk(rows_s, 3), stk(rows_s, 4))
```

```python
import functools

import numpy as np
import jax
import jax.numpy as jnp
from jax import lax
from jax.experimental import pallas as pl
from jax.experimental.pallas import tpu as pltpu

F32 = jnp.float32
BF16 = jnp.bfloat16

D_MODEL = 1024
HEAD_DIM = 64
N_HEADS = 8
WIDTH = N_HEADS * HEAD_DIM
N_META = 16
DECAY_LORA = 64
AAA_LORA = 64
GATE_LORA = 160
FFN_DIM = 2816
R_COLS = 3 * WIDTH + DECAY_LORA + AAA_LORA + GATE_LORA
QKV_COLS = 3 * WIDTH
NORM_EPS = 1e-6
GN_EPS = HEAD_DIM * 1e-5
NEG_INF = -1e30
PAGE = 128
LANES = 128
QUAD = 4 * HEAD_DIM
FFN_CHUNK = 256
VMEM_LIMIT = 56 * 1024 * 1024
REC_VMEM_LIMIT = 62 * 1024 * 1024


def _dot(a, b):
    return jnp.dot(a.astype(BF16), b.astype(BF16), preferred_element_type=F32)


def _dot_nt(a, b):
    return lax.dot_general(a.astype(BF16), b.astype(BF16), (((1,), (1,)), ((), ())),
                           preferred_element_type=F32)


def _dot_tn(a, b):
    return lax.dot_general(a.astype(BF16), b.astype(BF16), (((0,), (0,)), ((), ())),
                           preferred_element_type=F32)


def _split(x, n):
    parts = []
    r = x
    for i in range(n):
        p = r.astype(BF16)
        parts.append(p)
        if i + 1 < n:
            r = r - p.astype(F32)
    return parts


def _dot_x(a, b, dot=_dot, na=1, nb=1):
    pa = _split(a, na) if na > 1 else [a]
    pb = _split(b, nb) if nb > 1 else [b]
    out = None
    for i, x in enumerate(pa):
        for j, y in enumerate(pb):
            if i + j >= max(na, nb):
                continue
            t = dot(x, y)
            out = t if out is None else out + t
    return out


def _sigmoid(x):
    return 1.0 / (1.0 + jnp.exp(-x))


def _rms(x, g):
    ms = jnp.mean(x * x, axis=-1, keepdims=True)
    return x * lax.rsqrt(ms + NORM_EPS) * g


def _head_sum(x, ones_bd):
    return _dot_x(x, ones_bd, na=2)


def _params(sem, vmem=VMEM_LIMIT):
    return pltpu.CompilerParams(dimension_semantics=sem, vmem_limit_bytes=vmem)


def _resident(shape):
    nd = len(shape)
    return pl.BlockSpec(shape, lambda *_: (0,) * nd, pipeline_mode=pl.Buffered(1))


def _np_ones_bd(n, blk):
    i = np.arange(n)
    return (i[:, None] // blk == i[None, :] // blk).astype(np.float32)


def _ffn_kernel(x_ref, g_ref, wg_ref, wu_ref, wd_ref, o_ref, acc_ref):
    x = x_ref[...]
    xn = _rms(x, g_ref[...]).astype(BF16)
    for c in range(FFN_DIM // FFN_CHUNK):
        sl = slice(c * FFN_CHUNK, (c + 1) * FFN_CHUNK)
        gate = jnp.dot(xn, wg_ref[:, sl], preferred_element_type=F32)
        up = jnp.dot(xn, wu_ref[:, sl], preferred_element_type=F32)
        h = (gate * _sigmoid(gate) * up).astype(BF16)
        part = jnp.dot(h, wd_ref[sl, :], preferred_element_type=F32)
        if c == 0:
            acc_ref[...] = part
        else:
            acc_ref[...] += part
    o_ref[...] = x + 0.5 * acc_ref[...]


def _ffn(x, norm_g, w_gate, w_up, w_down, tm, scatter=None):
    m = x.shape[0]
    row = pl.BlockSpec((tm, D_MODEL), lambda i: (i, 0))
    out_row, m_out = row, m
    if scatter is not None:
        seq_len, start, m_out = scatter
        assert (seq_len - start) % tm == 0 and start % 8 == 0 and seq_len % 8 == 0
        per_seq = (seq_len - start) // tm
        out_row = pl.BlockSpec(
            (pl.Element(tm), pl.Element(D_MODEL)),
            lambda i: (pl.multiple_of((i // per_seq) * seq_len + start + (i % per_seq) * tm, 8), 0))
    return pl.pallas_call(
        _ffn_kernel,
        grid=(m // tm,),
        in_specs=[row, _resident((1, D_MODEL)), _resident((D_MODEL, FFN_DIM)),
                  _resident((D_MODEL, FFN_DIM)), _resident((FFN_DIM, D_MODEL))],
        out_specs=out_row,
        out_shape=jax.ShapeDtypeStruct((m_out, D_MODEL), F32),
        scratch_shapes=[pltpu.VMEM((tm, D_MODEL), F32)],
        compiler_params=_params(("parallel",)),
        name="ffn",
    )(x, norm_g, w_gate, w_up, w_down)


def _place_kernel(rows_ref, big_ref, o_ref):
    del big_ref
    o_ref[...] = rows_ref[...]


def _place_rows(big, rows, seq_len):
    n = rows.shape[0]
    assert n % 8 == 0 and seq_len % 8 == 0
    return pl.pallas_call(
        _place_kernel,
        grid=(big.shape[0] // seq_len,),
        in_specs=[_resident(rows.shape), pl.BlockSpec(memory_space=pl.ANY)],
        out_specs=pl.BlockSpec((pl.Element(n), pl.Element(D_MODEL)),
                               lambda b: (pl.multiple_of(b * seq_len, 8), 0)),
        out_shape=jax.ShapeDtypeStruct(big.shape, big.dtype),
        input_output_aliases={1: 0},
        compiler_params=_params(("arbitrary",)),
        name="place_rows",
    )(rows, big)


LORA_COLS = 3 * LANES


def _proj_kernel(x_ref, g_ref, wr_ref, wl_ref, wq_ref, wf_ref, wg_ref, zr_ref, zl_ref, zq_ref, zf_ref, zg_ref):
    xn = _rms(x_ref[...], g_ref[...]).astype(BF16)
    zr_ref[...] = jnp.dot(xn, wr_ref[...], preferred_element_type=F32)
    zl_ref[...] = jnp.dot(xn, wl_ref[...], preferred_element_type=F32)
    zq_ref[...] = jnp.dot(xn, wq_ref[...], preferred_element_type=F32)
    zf_ref[...] = jnp.dot(xn, wf_ref[...], preferred_element_type=F32)
    zg_ref[...] = jnp.dot(xn, wg_ref[...], preferred_element_type=F32)


def _proj(x, norm_g, w_r, w_l, w_q, w_f, w_g, tm):
    m = x.shape[0]
    row = lambda n: pl.BlockSpec((tm, n), lambda i: (i, 0))
    widths = [R_COLS, LORA_COLS, QKV_COLS, LANES, 2 * D_MODEL]
    return pl.pallas_call(
        _proj_kernel,
        grid=(m // tm,),
        in_specs=[row(D_MODEL), _resident((1, D_MODEL))] + [_resident((D_MODEL, n)) for n in widths],
        out_specs=[row(n) for n in widths],
        out_shape=[jax.ShapeDtypeStruct((m, n), F32) for n in widths],
        compiler_params=_params(("parallel",)),
        name="in_proj",
    )(x, norm_g, w_r, w_l, w_q, w_f, w_g)


def _prep_math(z, zprev, mu, w0, wdu, a0, wau, wgu, k_k, k_a, r_k, ones_bd, outs):
    r_ref, lw_ref, k_ref, v_ref, a_ref, b_ref, g_ref, bonus_ref = outs
    zs = z + (zprev - z) * mu
    r = zs[:, 0:WIDTH]
    k = zs[:, WIDTH:2 * WIDTH]
    v = zs[:, 2 * WIDTH:3 * WIDTH]
    o = 3 * WIDTH
    d_lo = zs[:, o:o + DECAY_LORA]
    a_lo = zs[:, o + DECAY_LORA:o + DECAY_LORA + AAA_LORA]
    g_lo = zs[:, o + DECAY_LORA + AAA_LORA:R_COLS]
    wpre = w0 + _dot(jnp.tanh(d_lo), wdu)
    y = -wpre
    softplus = jnp.maximum(y, 0.0) + jnp.log(1.0 + jnp.exp(-jnp.abs(y)))
    w_log = -softplus - 0.5
    lw_ref[...] = -jnp.exp(w_log)
    a = _sigmoid(a0 + _dot(a_lo, wau))
    g_ref[...] = _dot(_sigmoid(g_lo), wgu)
    kk = k * k_k
    k2 = k * (1.0 + (a - 1.0) * k_a)
    kk = kk * lax.rsqrt(_head_sum(kk * kk, ones_bd) + 1e-12)
    r_ref[...] = r
    k_ref[...] = k2
    v_ref[...] = v
    a_ref[...] = -kk
    b_ref[...] = kk * a
    bonus_ref[...] = _head_sum(r * k2 * r_k, ones_bd) * v


def _prep_kernel(z_ref, zp_ref, mu, w0, wdu, a0, wau, wgu, k_k, k_a, r_k, ones_bd, *outs):
    _prep_math(z_ref[...], zp_ref[...], mu[...], w0[...], wdu[...], a0[...], wau[...], wgu[...],
               k_k[...], k_a[...], r_k[...], ones_bd[...], outs)


def _prep(zr, zprev, pw, tm):
    m = zr.shape[0]
    row = lambda n: pl.BlockSpec((tm, n), lambda i: (i, 0))
    consts = [pw["mu"], pw["w0"], pw["wdu"], pw["a0"], pw["wau"], pw["wgu"], pw["k_k"], pw["k_a"],
              pw["r_k"], pw["ones_bd"]]
    return pl.pallas_call(
        _prep_kernel,
        grid=(m // tm,),
        in_specs=[row(R_COLS), row(R_COLS)] + [_resident(c.shape) for c in consts],
        out_specs=[row(WIDTH)] * 8,
        out_shape=[jax.ShapeDtypeStruct((m, WIDTH), F32)] * 8,
        compiler_params=_params(("parallel",)),
        name="rwkv_prep",
    )(zr, zprev, *consts)


def _rec_masks(c):
    t = np.arange(c)[:, None]
    hs = np.arange(4 * c)[None, :]
    s = hs % c
    strict = (s < t).astype(np.float32)
    incl = (s <= t).astype(np.float32)
    eye = (s == t).astype(np.float32)
    rows = np.arange(4 * c)[:, None]
    bd = (rows // c == hs // c).astype(np.float32)
    tri = (np.arange(c)[None, :] <= np.arange(c)[:, None]).astype(np.float32)
    lane = np.arange(QUAD)[None, :] // HEAD_DIM
    head = (lane == rows // c).astype(np.float32)
    sbd = _np_ones_bd(QUAD, HEAD_DIM)
    return strict, incl, eye, bd, tri, head, sbd


def _stack_heads(x, head):
    return jnp.concatenate([x.astype(BF16)] * 4, axis=0) * head


def _chunk_tokens(rows, zr_ref, zk_ref, zv_ref, zl_ref, vec_ref, mul_ref, wdu_ref, wau_ref, wgu_ref, ones_bd,
                  g_ref, bonus_ref):
    start, size = rows
    halo = pl.ds(pl.multiple_of(start - 8, 8), size + 8)

    def shifted(ref, mu):
        z = ref[halo, :]
        cur, prev = z[8:], z[7:size + 7]
        return cur + (prev - cur) * mu

    vec = vec_ref[...]
    r = shifted(zr_ref, vec[0:1])
    k = shifted(zk_ref, vec[1:2])
    v = shifted(zv_ref, vec[2:3])
    lora = shifted(zl_ref, mul_ref[...])
    d_lo = lora[:, 0:DECAY_LORA]
    a_lo = lora[:, DECAY_LORA:DECAY_LORA + AAA_LORA]
    g_lo = lora[:, DECAY_LORA + AAA_LORA:DECAY_LORA + AAA_LORA + GATE_LORA]
    y = -(vec[3:4] + _dot(jnp.tanh(d_lo), wdu_ref[...]))
    softplus = jnp.maximum(y, 0.0) + jnp.log(1.0 + jnp.exp(-jnp.abs(y)))
    lw = -jnp.exp(-softplus - 0.5)
    a_lr = _sigmoid(vec[4:5] + _dot(a_lo, wau_ref[...]))
    out_rows = pl.ds(pl.multiple_of(start, 8), size)
    g_ref[out_rows, :] = _dot(_sigmoid(g_lo), wgu_ref[...])
    kk = k * vec[5:6]
    k2 = k * (1.0 + (a_lr - 1.0) * vec[6:7])
    kk = kk * lax.rsqrt(_head_sum(kk * kk, ones_bd) + 1e-12)
    bonus_ref[out_rows, :] = _head_sum(r * k2 * vec[7:8], ones_bd) * v
    return r, lw, k2, v, -kk, kk * a_lr


def _rec_kernel(c, first_chunk, n_doubling, group, zr_ref, zk_ref, zv_ref, zl_ref, s0_ref,
                vec_ref, mul_ref, wdu_ref, wau_ref, wgu_ref, onesbd_ref,
                strict_ref, incl_ref, eye_ref, bd_ref, tri_ref, head_ref, sbd_ref,
                y_ref, g_ref, bonus_ref, sT_ref, s_ref, ry_scr, yc_scr, tr_scr, cs_scr, wl_scr):
    t_len = zr_ref.shape[0]
    n_iter = t_len // c - first_chunk
    for ref in (y_ref, g_ref, bonus_ref):
        ref[0:first_chunk * c, :] = jnp.zeros((first_chunk * c, QUAD), F32)

    def to_bd(p_row):
        return jnp.concatenate([p_row.astype(BF16)] * 4, axis=0) * bd_ref[...]

    def chunk_rows(ci):
        return pl.ds(pl.multiple_of(ci * c, c), c)

    def prepare(gi, carry, tick=lambda: None):
        strict = strict_ref[...]
        incl2 = jnp.concatenate([incl_ref[...]] * 2, axis=1)
        head = head_ref[...]
        slots = [gi * group + j for j in range(group)]
        loaded = []
        for slot in slots:
            loaded.append(_chunk_tokens(((slot + first_chunk) * c, c), zr_ref, zk_ref, zv_ref, zl_ref,
                                        vec_ref, mul_ref, wdu_ref, wau_ref, wgu_ref, onesbd_ref[...],
                                        g_ref, bonus_ref))
        cums = [_dot_x(tri_ref[...], ld[1], nb=3) for ld in loaded]
        tick()
        work = []
        for slot, (r, lw, k, v, a, b), cum in zip(slots, loaded, cums):
            cum_last = cum[c - 1:c, :]
            e_inv = jnp.exp(-cum)
            at = (a * jnp.exp(cum - lw)).astype(BF16)
            rt = r * jnp.exp(cum)
            e_last = jnp.exp(cum_last - cum)
            wl_scr[slot] = jnp.broadcast_to(jnp.exp(cum_last), (8, QUAD))
            rhs = jnp.concatenate([_stack_heads(b * e_inv, head), _stack_heads(k * e_inv, head)], axis=0)
            gmat = _dot_nt(jnp.concatenate([at, rt.astype(BF16)], axis=0), rhs)
            work.append(dict(at=at, rt=rt, v=v, bc=(b * e_last).astype(BF16), kc=(k * e_last).astype(BF16),
                             gmat=gmat))
        tick()
        p_rows, t_rows = [], []
        for w in work:
            gmat = w.pop("gmat")
            n_row = gmat[0:c, 0:4 * c] * strict
            w["v_st"] = _stack_heads(w["v"], head)
            w["xv"] = _dot(gmat[0:c, 4 * c:8 * c] * strict, w["v_st"])
            w["n_r"] = (gmat[c:2 * c, :] * incl2).astype(BF16)
            p_rows.append(n_row)
            t_rows.append(eye_ref[...] + n_row)
        tick()
        for _ in range(n_doubling):
            t_bds = [to_bd(t) for t in t_rows]
            p_rows = [_dot(p, to_bd(p)) for p in p_rows]
            tick()
            t_rows = [t + _dot(p, t_bd) for t, p, t_bd in zip(t_rows, p_rows, t_bds)]
        tick()
        for slot, w, t_row in zip(slots, work, t_rows):
            both = jnp.concatenate([_stack_heads(w["at"], head), _stack_heads(w["xv"], head)], axis=1)
            t_both = _dot(t_row, both)
            ta, txv = t_both[:, 0:QUAD], t_both[:, QUAD:]
            n_rb = w["n_r"][:, 0:4 * c]
            ry_scr[slot] = (w["rt"] + _dot(n_rb, _stack_heads(ta, head))).astype(BF16)
            yc_scr[slot] = _dot(w["n_r"], jnp.concatenate([_stack_heads(txv, head), w["v_st"]],
                                                          axis=0))
            tr_scr[slot] = (_dot_tn(ta, w["bc"]) * sbd_ref[...]).astype(BF16)
            cs_scr[slot] = (_dot_tn(jnp.concatenate([txv, w["v"]], axis=0),
                                    jnp.concatenate([w["bc"], w["kc"]], axis=0)) * sbd_ref[...]).astype(BF16)
        return carry

    s_ref[...] = s0_ref[...]

    def advance(slot):
        rows = chunk_rows(slot + first_chunk)
        s_old = s_ref[...]
        s_bf = s_old.astype(BF16)
        y_ref[rows, :] = _dot_nt(ry_scr[slot], s_bf) + yc_scr[slot]
        s_ref[...] = s_old * wl_scr[slot][0:1, :] + _dot(s_bf, tr_scr[slot]) + cs_scr[slot]

    def both_passes(gi, carry):
        pending = list(range(group))

        def tick():
            if pending:
                advance((gi - 1) * group + pending.pop(0))

        prepare(gi, carry, tick)
        while pending:
            tick()
        return carry

    n_groups = n_iter // group
    prepare(0, 0)
    lax.fori_loop(1, n_groups, both_passes, 0)
    for j in range(group):
        advance((n_groups - 1) * group + j)
    sT_ref[...] = s_ref[...]


def _rec(zr, zl, s0, pw, c, first_chunk):
    bsz, t_len, _ = zr.shape
    assert first_chunk * c >= 8
    strict, incl, eye, bd, tri, head, sbd = (jnp.asarray(m) for m in _rec_masks(c))
    masks = [strict, incl, eye, bd.astype(BF16), tri, head.astype(BF16), sbd]
    n_doubling = max(int(np.ceil(np.log2(c))) - 1, 0)
    n_iter = t_len // c - first_chunk
    group = next(g for g in (11, 3, 1) if n_iter % g == 0)
    cols = lambda j0: pl.BlockSpec((None, t_len, QUAD), lambda b, q: (b, 0, j0 + q))
    seq = cols(0)
    st = pl.BlockSpec((None, None, QUAD, QUAD), lambda b, q: (b, q, 0, 0))
    quad_cols = lambda rows: pl.BlockSpec((rows, QUAD), lambda b, q: (0, q))
    mu = pw["mu"]
    vec = jnp.concatenate([mu[:, 0:WIDTH], mu[:, WIDTH:2 * WIDTH], mu[:, 2 * WIDTH:3 * WIDTH], pw["w0"],
                           pw["a0"], pw["k_k"], pw["k_a"], pw["r_k"]], axis=0)
    mu_l = jnp.pad(mu[:, 3 * WIDTH:], ((0, 0), (0, zl.shape[-1] - (R_COLS - 3 * WIDTH))))
    return pl.pallas_call(
        functools.partial(_rec_kernel, c, first_chunk, n_doubling, group),
        grid=(bsz, 2),
        in_specs=([cols(0), cols(2), cols(4), pl.BlockSpec((None, t_len, zl.shape[-1]), lambda b, q: (b, 0, 0)),
                   st, quad_cols(8), _resident(mu_l.shape), quad_cols(DECAY_LORA), quad_cols(AAA_LORA),
                   quad_cols(GATE_LORA), _resident((QUAD, QUAD))] + [_resident(m.shape) for m in masks]),
        out_specs=[seq, seq, seq, st],
        out_shape=[jax.ShapeDtypeStruct((bsz, t_len, WIDTH), F32)] * 3
                  + [jax.ShapeDtypeStruct((bsz, 2, QUAD, QUAD), F32)],
        scratch_shapes=[pltpu.VMEM((QUAD, QUAD), F32),
                        pltpu.VMEM((n_iter, c, QUAD), BF16), pltpu.VMEM((n_iter, c, QUAD), F32),
                        pltpu.VMEM((n_iter, QUAD, QUAD), BF16), pltpu.VMEM((n_iter, QUAD, QUAD), BF16),
                        pltpu.VMEM((n_iter, 8, QUAD), F32)],
        compiler_params=_params(("parallel", "parallel"), vmem=REC_VMEM_LIMIT),
        name="rwkv_rec",
    )(zr, zr, zr, zl, s0, vec, mu_l, pw["wdu"], pw["wau"], pw["wgu"], sbd.astype(BF16), *masks)


def _step_kernel(tok_ref, s_ref, y_ref, so_ref):
    s = s_ref[...]
    r, lw, k, v, a, b = (tok_ref[:, i] for i in range(6))
    eye = (lax.broadcasted_iota(jnp.int32, (HEAD_DIM, HEAD_DIM), 0)
           == lax.broadcasted_iota(jnp.int32, (HEAD_DIM, HEAD_DIM), 1)).astype(F32)
    u = jnp.sum(s * a, axis=-1, keepdims=True)
    v_col = jnp.sum(eye * v, axis=-1, keepdims=True)
    s_new = s * jnp.exp(lw) + u * b + v_col * k
    so_ref[...] = s_new
    y_col = jnp.sum(s_new * r, axis=-1, keepdims=True)
    y_ref[...] = jnp.sum(y_col * eye, axis=-2, keepdims=True)


def _step(tok, state, nb):
    bsz = state.shape[0]
    return pl.pallas_call(
        _step_kernel,
        grid=(bsz // nb,),
        in_specs=[pl.BlockSpec((nb, 6, N_HEADS, 1, HEAD_DIM), lambda i: (i, 0, 0, 0, 0)),
                  pl.BlockSpec((nb, N_HEADS, HEAD_DIM, HEAD_DIM), lambda i: (i, 0, 0, 0))],
        out_specs=[pl.BlockSpec((nb, N_HEADS, 1, HEAD_DIM), lambda i: (i, 0, 0, 0)),
                   pl.BlockSpec((nb, N_HEADS, HEAD_DIM, HEAD_DIM), lambda i: (i, 0, 0, 0))],
        out_shape=[jax.ShapeDtypeStruct((bsz, N_HEADS, 1, HEAD_DIM), F32),
                   jax.ShapeDtypeStruct((bsz, N_HEADS, HEAD_DIM, HEAD_DIM), F32)],
        compiler_params=_params(("parallel",)),
        name="rwkv_step",
    )(tok, state)


def _foxprep_kernel(n_pad, with_cumsum, q_ref, k_ref, f_ref, qn_ref, kn_ref, bf_ref, ones_bd, tri_ref,
                    qo_ref, ko_ref, lf_ref, *cum_refs):
    q = q_ref[...]
    k = k_ref[...]
    inv = 1.0 / HEAD_DIM
    qo_ref[...] = (q * lax.rsqrt(_head_sum(q * q, ones_bd[...]) * inv + NORM_EPS) * qn_ref[...]
                   * (HEAD_DIM ** -0.5))
    ko_ref[...] = k * lax.rsqrt(_head_sum(k * k, ones_bd[...]) * inv + NORM_EPS) * kn_ref[...]
    x = f_ref[...] + bf_ref[...]
    lf = jnp.minimum(x, 0.0) - jnp.log(1.0 + jnp.exp(-jnp.abs(x)))
    lane = lax.broadcasted_iota(jnp.int32, lf.shape, 1)
    lf = jnp.where(lane < N_HEADS, lf, 0.0)
    if with_cumsum:
        row = lax.broadcasted_iota(jnp.int32, lf.shape, 0)
        lf = jnp.where(row >= n_pad, lf, 0.0)
    lf_ref[...] = lf
    if with_cumsum:
        (c_ref,) = cum_refs
        t_len = lf.shape[0]
        carry = jnp.zeros((1, LANES), F32)
        for blk in range(t_len // LANES):
            rows = slice(blk * LANES, (blk + 1) * LANES)
            cb = _dot_x(tri_ref[...], lf[rows, :], nb=3) + carry
            c_ref[rows, :] = cb
            carry = cb[LANES - 1:LANES, :]


def _foxprep(zq, zf, qn_w, kn_w, bf_row, ones_bd, rows_per_step, n_steps, n_pad, with_cumsum):
    m = zq.shape[0]
    tm = rows_per_step
    tri = jnp.asarray(np.tril(np.ones((LANES, LANES), np.float32)))
    col = lambda j: pl.BlockSpec((tm, WIDTH), lambda i: (i, j))
    row = lambda n: pl.BlockSpec((tm, n), lambda i: (i, 0))
    out_specs = [row(WIDTH), row(WIDTH), row(LANES)]
    out_shape = [jax.ShapeDtypeStruct((m, WIDTH), F32), jax.ShapeDtypeStruct((m, WIDTH), F32),
                 jax.ShapeDtypeStruct((m, LANES), F32)]
    if with_cumsum:
        out_specs += [row(LANES)]
        out_shape += [jax.ShapeDtypeStruct((m, LANES), F32)]
    return pl.pallas_call(
        functools.partial(_foxprep_kernel, n_pad, with_cumsum),
        grid=(n_steps,),
        in_specs=[col(0), col(1), row(LANES), _resident((1, WIDTH)), _resident((1, WIDTH)),
                  _resident((1, LANES)), _resident((WIDTH, WIDTH)), _resident((LANES, LANES))],
        out_specs=out_specs,
        out_shape=out_shape,
        compiler_params=_params(("parallel",)),
        name="fox_prep",
    )(zq, zq, zf, qn_w, kn_w, bf_row, ones_bd, tri)


def _foxattn_kernel(n_pad, n_qtiles, q_ref, k_ref, v_ref, c_ref, o_ref, qa_ref, ka_ref, va_ref):
    t_len = q_ref.shape[0]
    tq = t_len // n_qtiles
    pair = pl.program_id(1)
    lane = lax.broadcasted_iota(jnp.int32, (1, LANES), 1)
    row = lax.broadcasted_iota(jnp.int32, (t_len, 1), 0)
    causal = (lax.broadcasted_iota(jnp.int32, (tq, tq), 1) <= lax.broadcasted_iota(jnp.int32, (tq, tq), 0))
    for h in range(2):
        hmask = (lane // HEAD_DIM == h).astype(F32)
        sel = (lane == 2 * pair + h).astype(F32)
        c = jnp.sum(c_ref[...] * sel, axis=-1, keepdims=True)
        hi = c.astype(BF16).astype(F32)
        mid = (c - hi).astype(BF16).astype(F32)
        lo = (c - hi - mid).astype(BF16).astype(F32)
        base = HEAD_DIM * (1 - h)
        at = lambda i: (lane == base + i).astype(F32)
        ones_q = at(3) + at(4) + at(5)
        ones_k = at(0) + at(1) + at(2)
        hi_k = jnp.where(row >= n_pad, hi, -NEG_INF)
        qa_ref[...] = (q_ref[...] * hmask + hi * at(0) + mid * at(1) + lo * at(2) + ones_q).astype(BF16)
        ka_ref[...] = (k_ref[...] * hmask + ones_k - hi_k * at(3) - mid * at(4) - lo * at(5)).astype(BF16)
        va_ref[...] = (v_ref[...] * hmask + at(0)).astype(BF16)
        for qi in range(n_qtiles):
            q0 = qi * tq
            qa = qa_ref[q0:q0 + tq, :]
            s_diag = jnp.where(causal, _dot_nt(qa, ka_ref[q0:q0 + tq, :]), NEG_INF)
            m = jnp.max(s_diag, axis=-1, keepdims=True)
            if qi > 0:
                s_past = _dot_nt(qa, ka_ref[0:q0, :])
                m = jnp.maximum(m, jnp.max(s_past, axis=-1, keepdims=True))
            out = _dot(jnp.exp(s_diag - m), va_ref[q0:q0 + tq, :])
            if qi > 0:
                out = out + _dot(jnp.exp(s_past - m), va_ref[0:q0, :])
            l = jnp.sum(out * at(0), axis=-1, keepdims=True)
            out = out / l * hmask
            if h == 0:
                o_ref[q0:q0 + tq, :] = out
            else:
                o_ref[q0:q0 + tq, :] += out


def _foxattn(qn, kn, zq, cum, bsz, t_len, n_pad, n_qtiles):
    seq = lambda j0: pl.BlockSpec((t_len, LANES), lambda b, p: (b, j0 + p))
    return pl.pallas_call(
        functools.partial(_foxattn_kernel, n_pad, n_qtiles),
        grid=(bsz, N_HEADS // 2),
        in_specs=[seq(0), seq(0), seq(2 * WIDTH // LANES),
                  pl.BlockSpec((t_len, LANES), lambda b, p: (b, 0))],
        out_specs=seq(0),
        out_shape=jax.ShapeDtypeStruct((bsz * t_len, WIDTH), F32),
        scratch_shapes=[pltpu.VMEM((t_len, LANES), BF16)] * 3,
        compiler_params=_params(("parallel", "parallel")),
        name="fox_attn",
    )(qn, kn, zq, cum)


def _dec_consts():
    t = np.arange(PAGE)
    suf = (t[:, None] > t[None, :]).astype(np.float32)
    bdm = (np.arange(WIDTH)[None, :] // HEAD_DIM == np.arange(8)[:, None]).astype(np.float32)
    eye = (np.arange(LANES)[None, :] == np.arange(8)[:, None]).astype(np.float32)
    return suf, bdm, eye


def _dec_kernel(n_group, pt_ref, q_ref, kn_ref, vn_ref, lfn_ref, suf_ref, bdm_ref, eye_ref, *rest):
    k_refs = rest[0:n_group]
    v_refs = rest[n_group:2 * n_group]
    lf_refs = rest[2 * n_group:3 * n_group]
    o_ref = rest[3 * n_group]
    m_ref, l_ref, acc_ref, carry_ref = rest[3 * n_group + 1:]
    j = pl.program_id(1)
    bdm = bdm_ref[...]
    qbd = q_ref[...] * bdm
    lf_new = jnp.sum(eye_ref[...] * lfn_ref[...], axis=-1, keepdims=True)

    @pl.when(j == 0)
    def _():
        m_ref[...] = jnp.broadcast_to(jnp.sum(qbd * kn_ref[...], axis=-1, keepdims=True), (8, LANES))
        l_ref[...] = jnp.ones((8, LANES), F32)
        acc_ref[...] = vn_ref[...] * bdm
        carry_ref[...] = jnp.zeros((8, LANES), F32)

    carry = carry_ref[:, 0:1]
    qb = qbd.astype(BF16)
    scores = []
    for g in range(n_group):
        lf_t = lf_refs[g][...]
        suffix = _dot_x(lf_t, suf_ref[...], na=3)
        kt = k_refs[g][...].reshape(WIDTH, PAGE)
        scores.append(_dot(qb, kt) + (lf_new + carry + suffix))
        carry = carry + jnp.sum(lf_t, axis=-1, keepdims=True)
    carry_ref[...] = jnp.broadcast_to(carry, (8, LANES))
    s_all = jnp.concatenate(scores, axis=1)
    m_old = m_ref[:, 0:1]
    m_new = jnp.maximum(m_old, jnp.max(s_all, axis=-1, keepdims=True))
    alpha = jnp.exp(m_old - m_new)
    p = jnp.exp(s_all - m_new)
    l_ref[...] = jnp.broadcast_to(alpha * l_ref[:, 0:1] + jnp.sum(p, axis=-1, keepdims=True), (8, LANES))
    m_ref[...] = jnp.broadcast_to(m_new, (8, LANES))
    acc = alpha * acc_ref[...]
    pb = p.astype(BF16)
    for g in range(n_group):
        vt = v_refs[g][...].reshape(WIDTH, PAGE)
        acc = acc + _dot_nt(pb[:, g * PAGE:(g + 1) * PAGE], vt)
    acc_ref[...] = acc

    @pl.when(j == pl.num_programs(1) - 1)
    def _():
        o_ref[...] = jnp.sum(acc_ref[...] * bdm / l_ref[:, 0:1], axis=0, keepdims=True)


def _dec(page_table, qn, kn, vn, lfn, cache_k, cache_v, cache_lf, n_group):
    bsz, n_pages = page_table.shape
    kt = jnp.transpose(cache_k, (0, 2, 3, 1))
    vt = jnp.transpose(cache_v, (0, 2, 3, 1))
    lft = jnp.transpose(cache_lf, (0, 2, 1))
    consts = [jnp.asarray(c) for c in _dec_consts()]
    n_steps = n_pages // n_group
    tok = lambda n: pl.BlockSpec((None, 1, n), lambda b, j, pt: (b, 0, 0))
    const = lambda c: pl.BlockSpec(c.shape, lambda b, j, pt: (0,) * c.ndim, pipeline_mode=pl.Buffered(1))

    def page(shape, g):
        def index(b, j, pt):
            return (pt[b * n_pages + (n_pages - 1 - (j * n_group + g))],) + (0,) * len(shape)
        return pl.BlockSpec((None,) + shape, index)

    grid_spec = pltpu.PrefetchScalarGridSpec(
        num_scalar_prefetch=1,
        grid=(bsz, n_steps),
        in_specs=([tok(WIDTH), tok(WIDTH), tok(WIDTH), tok(LANES)] + [const(c) for c in consts]
                  + [page((N_HEADS, HEAD_DIM, PAGE), g) for g in range(n_group)]
                  + [page((N_HEADS, HEAD_DIM, PAGE), g) for g in range(n_group)]
                  + [page((N_HEADS, PAGE), g) for g in range(n_group)]),
        out_specs=tok(WIDTH),
        scratch_shapes=[pltpu.VMEM((8, LANES), F32), pltpu.VMEM((8, LANES), F32),
                        pltpu.VMEM((8, WIDTH), F32), pltpu.VMEM((8, LANES), F32)],
    )
    r3 = lambda x: x.reshape(bsz, 1, x.shape[-1])
    return pl.pallas_call(
        functools.partial(_dec_kernel, n_group),
        grid_spec=grid_spec,
        out_shape=jax.ShapeDtypeStruct((bsz, 1, WIDTH), F32),
        compiler_params=_params(("parallel", "arbitrary")),
        name="fox_decode",
    )(page_table.reshape(-1), r3(qn), r3(kn), r3(vn), r3(lfn), *consts,
      *([kt] * n_group), *([vt] * n_group), *([lft] * n_group)).reshape(bsz, WIDTH)


def _merge_kernel(x_ref, y_ref, bonus_ref, g_ref, fox_ref, zg_ref, gng_ref, gnb_ref, ones_bd,
                  wa_ref, wb_ref, wo_ref, o_ref):
    y = y_ref[...]
    inv = 1.0 / HEAD_DIM
    mean = _head_sum(y, ones_bd[...]) * inv
    d = y - mean
    var = _head_sum(d * d, ones_bd[...]) * inv
    yn = d * lax.rsqrt(var + GN_EPS) * gng_ref[...] + gnb_ref[...]
    rw = (yn + bonus_ref[...]) * g_ref[...]
    zg = zg_ref[...]
    merged = (_sigmoid(zg[:, 0:D_MODEL]) * _dot(rw, wa_ref[...])
              + _sigmoid(zg[:, D_MODEL:]) * _dot(fox_ref[...], wb_ref[...]))
    o_ref[...] = x_ref[...] + _dot(merged, wo_ref[...])


def _merge(x1, y, bonus, g, fox, zg, gn_g, gn_b, ones_bd, w_a, w_b, w_o, tm, keep=None):
    m = x1.shape[0]
    out_row = pl.BlockSpec((tm, D_MODEL), lambda i: (i, 0))
    if keep is None:
        m_out = m
        row = lambda n: pl.BlockSpec((tm, n), lambda i: (i, 0))
    else:
        seq_len, start = keep
        assert (seq_len - start) % tm == 0 and start % 8 == 0
        per_seq = (seq_len - start) // tm
        m_out = (m // seq_len) * per_seq * tm
        row = lambda n: pl.BlockSpec(
            (pl.Element(tm), pl.Element(n)),
            lambda i: (pl.multiple_of((i // per_seq) * seq_len + start + (i % per_seq) * tm, 8), 0))
    return pl.pallas_call(
        _merge_kernel,
        grid=(m_out // tm,),
        in_specs=[row(D_MODEL), row(WIDTH), row(WIDTH), row(WIDTH), row(WIDTH), row(2 * D_MODEL),
                  _resident((1, WIDTH)), _resident((1, WIDTH)), _resident((WIDTH, WIDTH)),
                  _resident((WIDTH, D_MODEL)), _resident((WIDTH, D_MODEL)), _resident((D_MODEL, D_MODEL))],
        out_specs=out_row,
        out_shape=jax.ShapeDtypeStruct((m_out, D_MODEL), F32),
        compiler_params=_params(("parallel",)),
        name="merge",
    )(x1, y, bonus, g, fox, zg, gn_g, gn_b, ones_bd, w_a, w_b, w_o)


def _to_blockdiag(state):
    bsz = state.shape[0]
    s = state.reshape(bsz, 2, 4, HEAD_DIM, HEAD_DIM)
    eye = jnp.eye(4, dtype=state.dtype)
    out = jnp.einsum("bqhvk,hg->bqhvgk", s, eye)
    return out.reshape(bsz, 2, QUAD, QUAD)


def _from_blockdiag(sbd):
    bsz = sbd.shape[0]
    s = sbd.reshape(bsz, 2, 4, HEAD_DIM, 4, HEAD_DIM)
    idx = jnp.arange(4)
    s = s[:, :, idx, :, idx, :]
    return jnp.moveaxis(s, 0, 2).reshape(bsz, N_HEADS, HEAD_DIM, HEAD_DIM)


def _layer_weights(l, ffn1_norm, ffn1_w_gate, ffn1_w_up, ffn1_w_down, mix_norm, w_in, mu_shift, w0,
                   w_decay_up, a0, w_aaa_up, w_gate_up, k_k, k_a, r_k, gn_g, gn_b, q_norm, k_norm, b_f,
                   w_a, w_b, w_o, ffn2_norm, ffn2_w_gate, ffn2_w_up, ffn2_w_down):
    row = lambda x: x.reshape(1, -1).astype(F32)
    bf = lambda x: x.astype(BF16)
    wi = w_in[l]
    w_f = jnp.pad(wi[:, R_COLS + QKV_COLS:R_COLS + QKV_COLS + N_HEADS], ((0, 0), (0, LANES - N_HEADS)))
    return dict(
        ffn1=(row(ffn1_norm[l]), bf(ffn1_w_gate[l]), bf(ffn1_w_up[l]), bf(ffn1_w_down[l])),
        ffn2=(row(ffn2_norm[l]), bf(ffn2_w_gate[l]), bf(ffn2_w_up[l]), bf(ffn2_w_down[l])),
        mix_norm=row(mix_norm[l]),
        w_r=bf(wi[:, 0:R_COLS]),
        w_l=bf(jnp.pad(wi[:, 3 * WIDTH:R_COLS], ((0, 0), (0, LORA_COLS - (R_COLS - 3 * WIDTH))))),
        w_q=bf(wi[:, R_COLS:R_COLS + QKV_COLS]), w_f=bf(w_f),
        w_g=bf(wi[:, R_COLS + QKV_COLS + N_HEADS:]),
        prep=dict(mu=row(mu_shift[l]), w0=row(w0[l]), wdu=bf(w_decay_up[l]), a0=row(a0[l]),
                  wau=bf(w_aaa_up[l]), wgu=bf(w_gate_up[l]), k_k=row(k_k[l]), k_a=row(k_a[l]),
                  r_k=row(r_k[l]), ones_bd=jnp.asarray(_np_ones_bd(WIDTH, HEAD_DIM), BF16)),
        gn_g=row(gn_g[l]), gn_b=row(gn_b[l]),
        q_norm=row(jnp.tile(q_norm[l], N_HEADS)), k_norm=row(jnp.tile(k_norm[l], N_HEADS)),
        b_f=jnp.pad(row(b_f[l]), ((0, 0), (0, LANES - N_HEADS))),
        w_a=bf(w_a[l]), w_b=bf(w_b[l]), w_o=bf(w_o[l]),
    )


def _prompt_layer(x1, w, bsz, t_len, n_pad, last):
    ones_bd = w["prep"]["ones_bd"]
    zr, zl, zq, zf, zg = _proj(x1, w["mix_norm"], w["w_r"], w["w_l"], w["w_q"], w["w_f"], w["w_g"], tm=544)
    s0 = jnp.zeros((bsz, 2, QUAD, QUAD), F32)
    flat = lambda arr: arr.reshape(bsz * t_len, WIDTH)
    y, g, bonus, s_t = _rec(zr.reshape(bsz, t_len, R_COLS), zl.reshape(bsz, t_len, LORA_COLS), s0, w["prep"],
                            c=64, first_chunk=n_pad // 64)
    y, g, bonus = flat(y), flat(g), flat(bonus)
    qn, kn, lf, cum = _foxprep(zq, zf, w["q_norm"], w["k_norm"], w["b_f"], ones_bd,
                               rows_per_step=t_len, n_steps=bsz, n_pad=n_pad, with_cumsum=True)
    fox = _foxattn(qn, kn, zq, cum, bsz, t_len, n_pad, n_qtiles=4)
    x2 = _merge(x1, y, bonus, g, fox, zg, w["gn_g"], w["gn_b"], ones_bd,
                w["w_a"], w["w_b"], w["w_o"], tm=512 if last else 544,
                keep=(t_len, n_pad + N_META) if last else None)
    x3 = _ffn(x2, *w["ffn2"], tm=512 if last else 544)
    return x3, zr, zq, kn, lf, s_t


def _sample_layer(x1, w, shift_prev, wkv0, page_table, cache_k, cache_v, cache_lf):
    bsz = x1.shape[0]
    ones_bd = w["prep"]["ones_bd"]
    zr, _, zq, zf, zg = _proj(x1, w["mix_norm"], w["w_r"], w["w_l"], w["w_q"], w["w_f"], w["w_g"], tm=bsz)
    prepped = _prep(zr, shift_prev, w["prep"], tm=bsz)
    g, bonus = prepped[6], prepped[7]
    tok = jnp.stack(prepped[:6], axis=1).reshape(bsz, 6, N_HEADS, 1, HEAD_DIM)
    y, s_t = _step(tok, wkv0.astype(F32), nb=8)
    y = y.reshape(bsz, WIDTH)
    qn, kn, lf = _foxprep(zq, zf, w["q_norm"], w["k_norm"], w["b_f"], ones_bd,
                          rows_per_step=bsz, n_steps=1, n_pad=0, with_cumsum=False)
    vn = zq[:, 2 * WIDTH:]
    fox = _dec(page_table, qn, kn, vn, lf, cache_k, cache_v, cache_lf, n_group=32)
    x2 = _merge(x1, y, bonus, g, fox, zg, w["gn_g"], w["gn_b"], ones_bd, w["w_a"], w["w_b"], w["w_o"],
                tm=bsz)
    x3 = _ffn(x2, *w["ffn2"], tm=bsz)
    return x3, zr, kn, vn, lf, s_t


def kernel(x_prompt, x_sample, cache_k, cache_v, cache_logf, state_wkv, state_shift, page_table, meta_tokens, ffn1_norm, ffn1_w_gate, ffn1_w_up, ffn1_w_down, mix_norm, w_in, mu_shift, w0, w_decay_up, a0, w_aaa_up, w_gate_up, k_k, k_a, r_k, gn_g, gn_b, q_norm, k_norm, b_f, w_a, w_b, w_o, ffn2_norm, ffn2_w_gate, ffn2_w_up, ffn2_w_down):
    depth = w_in.shape[0]
    bp, seq, _ = x_prompt.shape
    bs = x_sample.shape[0]
    l_tok = seq + N_META
    n_pad = (-l_tok) % LANES
    t_len = l_tok + n_pad
    head = jnp.concatenate([jnp.zeros((n_pad, D_MODEL), F32), meta_tokens.astype(F32)], axis=0)
    h_p = x_prompt.reshape(bp * seq, D_MODEL).astype(F32)
    h_s = x_sample.reshape(bs, D_MODEL).astype(F32)
    rows_p, rows_s = [], []
    for l in range(depth):
        w = _layer_weights(l, ffn1_norm, ffn1_w_gate, ffn1_w_up, ffn1_w_down, mix_norm, w_in, mu_shift, w0,
                           w_decay_up, a0, w_aaa_up, w_gate_up, k_k, k_a, r_k, gn_g, gn_b, q_norm, k_norm,
                           b_f, w_a, w_b, w_o, ffn2_norm, ffn2_w_gate, ffn2_w_up, ffn2_w_down)
        if l == 0:
            small = _ffn(jnp.concatenate([h_s, head], axis=0), *w["ffn1"], tm=bs + n_pad + N_META)
            x1_s, head1 = small[:bs], small[bs:]
            x1_p = _ffn(h_p, *w["ffn1"], tm=512, scatter=(t_len, n_pad + N_META, bp * t_len))
            x1_p = _place_rows(x1_p, head1, t_len)
        else:
            x1_s = _ffn(h_s, *w["ffn1"], tm=bs)
            x1_p = _ffn(h_p, *w["ffn1"], tm=544)
        h_p, zr, zq, kn, lf, s_t = _prompt_layer(x1_p, w, bp, t_len, n_pad, last=(l == depth - 1))
        real = lambda arr, n: arr.reshape(bp, t_len, n)[:, n_pad:]
        rows_p.append((real(kn, WIDTH).reshape(bp, l_tok, N_HEADS, HEAD_DIM),
                       real(zq, QKV_COLS)[:, :, 2 * WIDTH:].reshape(bp, l_tok, N_HEADS, HEAD_DIM),
                       real(lf, LANES)[:, :, :N_HEADS],
                       _from_blockdiag(s_t),
                       zr.reshape(bp, t_len, R_COLS)[:, -1]))
        h_s, zr_s, kn_s, vn_s, lf_s, s_ts = _sample_layer(
            x1_s, w, state_shift[l].astype(F32), state_wkv[l], page_table, cache_k[l], cache_v[l],
            cache_logf[l])
        rows_s.append((kn_s.reshape(bs, 1, N_HEADS, HEAD_DIM), vn_s.reshape(bs, 1, N_HEADS, HEAD_DIM),
                       lf_s[:, :N_HEADS].reshape(bs, 1, N_HEADS), s_ts, zr_s))
    y_prompt = h_p.reshape(bp, seq, D_MODEL)
    y_sample = h_s.reshape(bs, 1, D_MODEL)
    stk = lambda rows, i: jnp.stack([r[i] for r in rows], axis=0)
    return (y_prompt, y_sample,
            stk(rows_p, 0), stk(rows_p, 1), stk(rows_p, 2), stk(rows_p, 3), stk(rows_p, 4),
            stk(rows_s, 0), stk(rows_s, 1), stk(rows_s, 2), stk(rows_s, 3), stk(rows_s, 4))
```

```python
import functools

import numpy as np
import jax
import jax.numpy as jnp
from jax import lax
from jax.experimental import pallas as pl
from jax.experimental.pallas import tpu as pltpu

F32 = jnp.float32
BF16 = jnp.bfloat16

D_MODEL = 1024
HEAD_DIM = 64
N_HEADS = 8
WIDTH = N_HEADS * HEAD_DIM
N_META = 16
DECAY_LORA = 64
AAA_LORA = 64
GATE_LORA = 160
FFN_DIM = 2816
R_COLS = 3 * WIDTH + DECAY_LORA + AAA_LORA + GATE_LORA
QKV_COLS = 3 * WIDTH
NORM_EPS = 1e-6
GN_EPS = HEAD_DIM * 1e-5
NEG_INF = -1e30
PAGE = 128
LANES = 128
QUAD = 4 * HEAD_DIM
FFN_CHUNK = 256
VMEM_LIMIT = 56 * 1024 * 1024
REC_VMEM_LIMIT = 62 * 1024 * 1024


def _dot(a, b):
    return jnp.dot(a.astype(BF16), b.astype(BF16), preferred_element_type=F32)


def _dot_nt(a, b):
    return lax.dot_general(a.astype(BF16), b.astype(BF16), (((1,), (1,)), ((), ())),
                           preferred_element_type=F32)


def _dot_tn(a, b):
    return lax.dot_general(a.astype(BF16), b.astype(BF16), (((0,), (0,)), ((), ())),
                           preferred_element_type=F32)


def _split(x, n):
    parts = []
    r = x
    for i in range(n):
        p = r.astype(BF16)
        parts.append(p)
        if i + 1 < n:
            r = r - p.astype(F32)
    return parts


def _dot_x(a, b, dot=_dot, na=1, nb=1):
    pa = _split(a, na) if na > 1 else [a]
    pb = _split(b, nb) if nb > 1 else [b]
    out = None
    for i, x in enumerate(pa):
        for j, y in enumerate(pb):
            if i + j >= max(na, nb):
                continue
            t = dot(x, y)
            out = t if out is None else out + t
    return out


def _sigmoid(x):
    return 1.0 / (1.0 + jnp.exp(-x))


def _rms(x, g):
    ms = jnp.mean(x * x, axis=-1, keepdims=True)
    return x * lax.rsqrt(ms + NORM_EPS) * g


def _head_sum(x, ones_bd):
    return _dot_x(x, ones_bd, na=2)


def _params(sem, vmem=VMEM_LIMIT):
    return pltpu.CompilerParams(dimension_semantics=sem, vmem_limit_bytes=vmem)


def _resident(shape):
    nd = len(shape)
    return pl.BlockSpec(shape, lambda *_: (0,) * nd, pipeline_mode=pl.Buffered(1))


def _np_ones_bd(n, blk):
    i = np.arange(n)
    return (i[:, None] // blk == i[None, :] // blk).astype(np.float32)


def _ffn_kernel(x_ref, g_ref, wg_ref, wu_ref, wd_ref, o_ref, acc_ref):
    x = x_ref[...]
    xn = _rms(x, g_ref[...]).astype(BF16)
    for c in range(FFN_DIM // FFN_CHUNK):
        sl = slice(c * FFN_CHUNK, (c + 1) * FFN_CHUNK)
        gate = jnp.dot(xn, wg_ref[:, sl], preferred_element_type=F32)
        up = jnp.dot(xn, wu_ref[:, sl], preferred_element_type=F32)
        h = (gate * _sigmoid(gate) * up).astype(BF16)
        part = jnp.dot(h, wd_ref[sl, :], preferred_element_type=F32)
        if c == 0:
            acc_ref[...] = part
        else:
            acc_ref[...] += part
    o_ref[...] = x + 0.5 * acc_ref[...]


def _ffn(x, norm_g, w_gate, w_up, w_down, tm, scatter=None):
    m = x.shape[0]
    row = pl.BlockSpec((tm, D_MODEL), lambda i: (i, 0))
    out_row, m_out = row, m
    if scatter is not None:
        seq_len, start, m_out = scatter
        assert (seq_len - start) % tm == 0 and start % 8 == 0 and seq_len % 8 == 0
        per_seq = (seq_len - start) // tm
        out_row = pl.BlockSpec(
            (pl.Element(tm), pl.Element(D_MODEL)),
            lambda i: (pl.multiple_of((i // per_seq) * seq_len + start + (i % per_seq) * tm, 8), 0))
    return pl.pallas_call(
        _ffn_kernel,
        grid=(m // tm,),
        in_specs=[row, _resident((1, D_MODEL)), _resident((D_MODEL, FFN_DIM)),
                  _resident((D_MODEL, FFN_DIM)), _resident((FFN_DIM, D_MODEL))],
        out_specs=out_row,
        out_shape=jax.ShapeDtypeStruct((m_out, D_MODEL), F32),
        scratch_shapes=[pltpu.VMEM((tm, D_MODEL), F32)],
        compiler_params=_params(("parallel",)),
        name="ffn",
    )(x, norm_g, w_gate, w_up, w_down)


def _place_kernel(rows_ref, big_ref, o_ref):
    del big_ref
    o_ref[...] = rows_ref[...]


def _place_rows(big, rows, seq_len):
    n = rows.shape[0]
    assert n % 8 == 0 and seq_len % 8 == 0
    return pl.pallas_call(
        _place_kernel,
        grid=(big.shape[0] // seq_len,),
        in_specs=[_resident(rows.shape), pl.BlockSpec(memory_space=pl.ANY)],
        out_specs=pl.BlockSpec((pl.Element(n), pl.Element(D_MODEL)),
                               lambda b: (pl.multiple_of(b * seq_len, 8), 0)),
        out_shape=jax.ShapeDtypeStruct(big.shape, big.dtype),
        input_output_aliases={1: 0},
        compiler_params=_params(("arbitrary",)),
        name="place_rows",
    )(rows, big)


LORA_COLS = 3 * LANES


def _proj_kernel(x_ref, g_ref, wr_ref, wl_ref, wq_ref, wf_ref, wg_ref, zr_ref, zl_ref, zq_ref, zf_ref, zg_ref):
    xn = _rms(x_ref[...], g_ref[...]).astype(BF16)
    zr_ref[...] = jnp.dot(xn, wr_ref[...], preferred_element_type=F32)
    zl_ref[...] = jnp.dot(xn, wl_ref[...], preferred_element_type=F32)
    zq_ref[...] = jnp.dot(xn, wq_ref[...], preferred_element_type=F32)
    zf_ref[...] = jnp.dot(xn, wf_ref[...], preferred_element_type=F32)
    zg_ref[...] = jnp.dot(xn, wg_ref[...], preferred_element_type=F32)


def _proj(x, norm_g, w_r, w_l, w_q, w_f, w_g, tm):
    m = x.shape[0]
    row = lambda n: pl.BlockSpec((tm, n), lambda i: (i, 0))
    widths = [R_COLS, LORA_COLS, QKV_COLS, LANES, 2 * D_MODEL]
    return pl.pallas_call(
        _proj_kernel,
        grid=(m // tm,),
        in_specs=[row(D_MODEL), _resident((1, D_MODEL))] + [_resident((D_MODEL, n)) for n in widths],
        out_specs=[row(n) for n in widths],
        out_shape=[jax.ShapeDtypeStruct((m, n), F32) for n in widths],
        compiler_params=_params(("parallel",)),
        name="in_proj",
    )(x, norm_g, w_r, w_l, w_q, w_f, w_g)


def _prep_math(z, zprev, mu, w0, wdu, a0, wau, wgu, k_k, k_a, r_k, ones_bd, outs):
    r_ref, lw_ref, k_ref, v_ref, a_ref, b_ref, g_ref, bonus_ref = outs
    zs = z + (zprev - z) * mu
    r = zs[:, 0:WIDTH]
    k = zs[:, WIDTH:2 * WIDTH]
    v = zs[:, 2 * WIDTH:3 * WIDTH]
    o = 3 * WIDTH
    d_lo = zs[:, o:o + DECAY_LORA]
    a_lo = zs[:, o + DECAY_LORA:o + DECAY_LORA + AAA_LORA]
    g_lo = zs[:, o + DECAY_LORA + AAA_LORA:R_COLS]
    wpre = w0 + _dot(jnp.tanh(d_lo), wdu)
    y = -wpre
    softplus = jnp.maximum(y, 0.0) + jnp.log(1.0 + jnp.exp(-jnp.abs(y)))
    w_log = -softplus - 0.5
    lw_ref[...] = -jnp.exp(w_log)
    a = _sigmoid(a0 + _dot(a_lo, wau))
    g_ref[...] = _dot(_sigmoid(g_lo), wgu)
    kk = k * k_k
    k2 = k * (1.0 + (a - 1.0) * k_a)
    kk = kk * lax.rsqrt(_head_sum(kk * kk, ones_bd) + 1e-12)
    r_ref[...] = r
    k_ref[...] = k2
    v_ref[...] = v
    a_ref[...] = -kk
    b_ref[...] = kk * a
    bonus_ref[...] = _head_sum(r * k2 * r_k, ones_bd) * v


def _prep_kernel(z_ref, zp_ref, mu, w0, wdu, a0, wau, wgu, k_k, k_a, r_k, ones_bd, *outs):
    _prep_math(z_ref[...], zp_ref[...], mu[...], w0[...], wdu[...], a0[...], wau[...], wgu[...],
               k_k[...], k_a[...], r_k[...], ones_bd[...], outs)


def _prep(zr, zprev, pw, tm):
    m = zr.shape[0]
    row = lambda n: pl.BlockSpec((tm, n), lambda i: (i, 0))
    consts = [pw["mu"], pw["w0"], pw["wdu"], pw["a0"], pw["wau"], pw["wgu"], pw["k_k"], pw["k_a"],
              pw["r_k"], pw["ones_bd"]]
    return pl.pallas_call(
        _prep_kernel,
        grid=(m // tm,),
        in_specs=[row(R_COLS), row(R_COLS)] + [_resident(c.shape) for c in consts],
        out_specs=[row(WIDTH)] * 8,
        out_shape=[jax.ShapeDtypeStruct((m, WIDTH), F32)] * 8,
        compiler_params=_params(("parallel",)),
        name="rwkv_prep",
    )(zr, zprev, *consts)


def _rec_masks(c):
    t = np.arange(c)[:, None]
    hs = np.arange(4 * c)[None, :]
    s = hs % c
    strict = (s < t).astype(np.float32)
    incl = (s <= t).astype(np.float32)
    eye = (s == t).astype(np.float32)
    rows = np.arange(4 * c)[:, None]
    bd = (rows // c == hs // c).astype(np.float32)
    tri = (np.arange(c)[None, :] <= np.arange(c)[:, None]).astype(np.float32)
    lane = np.arange(QUAD)[None, :] // HEAD_DIM
    head = (lane == rows // c).astype(np.float32)
    sbd = _np_ones_bd(QUAD, HEAD_DIM)
    return strict, incl, eye, bd, tri, head, sbd


def _stack_heads(x, head):
    return jnp.concatenate([x.astype(BF16)] * 4, axis=0) * head


def _chunk_tokens(rows, zr_ref, zk_ref, zv_ref, zl_ref, vec_ref, mul_ref, wdu_ref, wau_ref, wgu_ref, ones_bd,
                  g_ref, bonus_ref):
    start, size = rows
    halo = pl.ds(pl.multiple_of(start - 8, 8), size + 8)

    def shifted(ref, mu):
        z = ref[halo, :]
        cur, prev = z[8:], pltpu.roll(z, 1, 0)[8:]
        return cur + (prev - cur) * mu

    vec = vec_ref[...]
    r = shifted(zr_ref, vec[0:1])
    k = shifted(zk_ref, vec[1:2])
    v = shifted(zv_ref, vec[2:3])
    lora = shifted(zl_ref, mul_ref[...])
    d_lo = lora[:, 0:DECAY_LORA]
    a_lo = lora[:, DECAY_LORA:DECAY_LORA + AAA_LORA]
    g_lo = lora[:, DECAY_LORA + AAA_LORA:DECAY_LORA + AAA_LORA + GATE_LORA]
    y = -(vec[3:4] + _dot(jnp.tanh(d_lo), wdu_ref[...]))
    softplus = jnp.maximum(y, 0.0) + jnp.log(1.0 + jnp.exp(-jnp.abs(y)))
    lw = -jnp.exp(-softplus - 0.5)
    a_lr = _sigmoid(vec[4:5] + _dot(a_lo, wau_ref[...]))
    out_rows = pl.ds(pl.multiple_of(start, 8), size)
    g_ref[out_rows, :] = _dot(_sigmoid(g_lo), wgu_ref[...])
    kk = k * vec[5:6]
    k2 = k * (1.0 + (a_lr - 1.0) * vec[6:7])
    kk = kk * lax.rsqrt(_head_sum(kk * kk, ones_bd) + 1e-12)
    bonus_ref[out_rows, :] = _head_sum(r * k2 * vec[7:8], ones_bd) * v
    return r, lw, k2, v, -kk, kk * a_lr


def _rec_kernel(c, first_chunk, n_doubling, group, zr_ref, zk_ref, zv_ref, zl_ref, s0_ref,
                vec_ref, mul_ref, wdu_ref, wau_ref, wgu_ref, onesbd_ref,
                strict_ref, incl_ref, eye_ref, bd_ref, tri_ref, head_ref, sbd_ref,
                y_ref, g_ref, bonus_ref, sT_ref, s_ref, ry_scr, yc_scr, tr_scr, cs_scr, wl_scr):
    t_len = zr_ref.shape[0]
    n_iter = t_len // c - first_chunk
    for ref in (y_ref, g_ref, bonus_ref):
        ref[0:first_chunk * c, :] = jnp.zeros((first_chunk * c, QUAD), F32)

    def to_bd(p_row):
        return jnp.concatenate([p_row.astype(BF16)] * 4, axis=0) * bd_ref[...]

    def chunk_rows(ci):
        return pl.ds(pl.multiple_of(ci * c, c), c)

    def prepare(gi, carry, tick=lambda: None):
        strict = strict_ref[...]
        incl2 = jnp.concatenate([incl_ref[...]] * 2, axis=1)
        head = head_ref[...]
        slots = [gi * group + j for j in range(group)]
        loaded = []
        for slot in slots:
            loaded.append(_chunk_tokens(((slot + first_chunk) * c, c), zr_ref, zk_ref, zv_ref, zl_ref,
                                        vec_ref, mul_ref, wdu_ref, wau_ref, wgu_ref, onesbd_ref[...],
                                        g_ref, bonus_ref))
        cums = [_dot_x(tri_ref[...], ld[1], nb=3) for ld in loaded]
        tick()
        work = []
        for slot, (r, lw, k, v, a, b), cum in zip(slots, loaded, cums):
            cum_last = cum[c - 1:c, :]
            e_inv = jnp.exp(-cum)
            at = (a * jnp.exp(cum - lw)).astype(BF16)
            rt = r * jnp.exp(cum)
            e_last = jnp.exp(cum_last - cum)
            wl_scr[slot] = jnp.broadcast_to(jnp.exp(cum_last), (8, QUAD))
            rhs = jnp.concatenate([_stack_heads(b * e_inv, head), _stack_heads(k * e_inv, head)], axis=0)
            gmat = _dot_nt(jnp.concatenate([at, rt.astype(BF16)], axis=0), rhs)
            work.append(dict(at=at, rt=rt, v=v, bc=(b * e_last).astype(BF16), kc=(k * e_last).astype(BF16),
                             gmat=gmat))
        tick()
        p_rows, t_rows = [], []
        for w in work:
            gmat = w.pop("gmat")
            n_row = gmat[0:c, 0:4 * c] * strict
            w["v_st"] = _stack_heads(w["v"], head)
            w["xv"] = _dot(gmat[0:c, 4 * c:8 * c] * strict, w["v_st"])
            w["n_r"] = (gmat[c:2 * c, :] * incl2).astype(BF16)
            p_rows.append(n_row)
            t_rows.append(eye_ref[...] + n_row)
        tick()
        for _ in range(n_doubling):
            t_bds = [to_bd(t) for t in t_rows]
            p_rows = [_dot(p, to_bd(p)) for p in p_rows]
            tick()
            t_rows = [t + _dot(p, t_bd) for t, p, t_bd in zip(t_rows, p_rows, t_bds)]
        tick()
        for slot, w, t_row in zip(slots, work, t_rows):
            both = jnp.concatenate([_stack_heads(w["at"], head), _stack_heads(w["xv"], head)], axis=1)
            t_both = _dot(t_row, both)
            ta, txv = t_both[:, 0:QUAD], t_both[:, QUAD:]
            n_rb = w["n_r"][:, 0:4 * c]
            ry_scr[slot] = (w["rt"] + _dot(n_rb, _stack_heads(ta, head))).astype(BF16)
            yc_scr[slot] = _dot(w["n_r"], jnp.concatenate([_stack_heads(txv, head), w["v_st"]],
                                                          axis=0))
            tr_scr[slot] = (_dot_tn(ta, w["bc"]) * sbd_ref[...]).astype(BF16)
            cs_scr[slot] = (_dot_tn(jnp.concatenate([txv, w["v"]], axis=0),
                                    jnp.concatenate([w["bc"], w["kc"]], axis=0)) * sbd_ref[...]).astype(BF16)
        return carry

    s_ref[...] = s0_ref[...]

    def advance(slot):
        rows = chunk_rows(slot + first_chunk)
        s_old = s_ref[...]
        s_bf = s_old.astype(BF16)
        y_ref[rows, :] = _dot_nt(ry_scr[slot], s_bf) + yc_scr[slot]
        s_ref[...] = s_old * wl_scr[slot][0:1, :] + _dot(s_bf, tr_scr[slot]) + cs_scr[slot]

    def both_passes(gi, carry):
        pending = list(range(group))

        def tick():
            if pending:
                advance((gi - 1) * group + pending.pop(0))

        prepare(gi, carry, tick)
        while pending:
            tick()
        return carry

    n_groups = n_iter // group
    prepare(0, 0)
    lax.fori_loop(1, n_groups, both_passes, 0)
    for j in range(group):
        advance((n_groups - 1) * group + j)
    sT_ref[...] = s_ref[...]


def _rec(zr, zl, s0, pw, c, first_chunk):
    bsz, t_len, _ = zr.shape
    assert first_chunk * c >= 8
    strict, incl, eye, bd, tri, head, sbd = (jnp.asarray(m) for m in _rec_masks(c))
    masks = [strict, incl, eye, bd.astype(BF16), tri, head.astype(BF16), sbd]
    n_doubling = max(int(np.ceil(np.log2(c))) - 1, 0)
    n_iter = t_len // c - first_chunk
    group = next(g for g in (11, 3, 1) if n_iter % g == 0)
    cols = lambda j0: pl.BlockSpec((None, t_len, QUAD), lambda b, q: (b, 0, j0 + q))
    seq = cols(0)
    st = pl.BlockSpec((None, None, QUAD, QUAD), lambda b, q: (b, q, 0, 0))
    quad_cols = lambda rows: pl.BlockSpec((rows, QUAD), lambda b, q: (0, q))
    mu = pw["mu"]
    vec = jnp.concatenate([mu[:, 0:WIDTH], mu[:, WIDTH:2 * WIDTH], mu[:, 2 * WIDTH:3 * WIDTH], pw["w0"],
                           pw["a0"], pw["k_k"], pw["k_a"], pw["r_k"]], axis=0)
    mu_l = jnp.pad(mu[:, 3 * WIDTH:], ((0, 0), (0, zl.shape[-1] - (R_COLS - 3 * WIDTH))))
    return pl.pallas_call(
        functools.partial(_rec_kernel, c, first_chunk, n_doubling, group),
        grid=(bsz, 2),
        in_specs=([cols(0), cols(2), cols(4), pl.BlockSpec((None, t_len, zl.shape[-1]), lambda b, q: (b, 0, 0)),
                   st, quad_cols(8), _resident(mu_l.shape), quad_cols(DECAY_LORA), quad_cols(AAA_LORA),
                   quad_cols(GATE_LORA), _resident((QUAD, QUAD))] + [_resident(m.shape) for m in masks]),
        out_specs=[seq, seq, seq, st],
        out_shape=[jax.ShapeDtypeStruct((bsz, t_len, WIDTH), F32)] * 3
                  + [jax.ShapeDtypeStruct((bsz, 2, QUAD, QUAD), F32)],
        scratch_shapes=[pltpu.VMEM((QUAD, QUAD), F32),
                        pltpu.VMEM((n_iter, c, QUAD), BF16), pltpu.VMEM((n_iter, c, QUAD), F32),
                        pltpu.VMEM((n_iter, QUAD, QUAD), BF16), pltpu.VMEM((n_iter, QUAD, QUAD), BF16),
                        pltpu.VMEM((n_iter, 8, QUAD), F32)],
        compiler_params=_params(("parallel", "parallel"), vmem=REC_VMEM_LIMIT),
        name="rwkv_rec",
    )(zr, zr, zr, zl, s0, vec, mu_l, pw["wdu"], pw["wau"], pw["wgu"], sbd.astype(BF16), *masks)


def _step_kernel(tok_ref, s_ref, y_ref, so_ref):
    s = s_ref[...]
    r, lw, k, v, a, b = (tok_ref[:, i] for i in range(6))
    eye = (lax.broadcasted_iota(jnp.int32, (HEAD_DIM, HEAD_DIM), 0)
           == lax.broadcasted_iota(jnp.int32, (HEAD_DIM, HEAD_DIM), 1)).astype(F32)
    u = jnp.sum(s * a, axis=-1, keepdims=True)
    v_col = jnp.sum(eye * v, axis=-1, keepdims=True)
    s_new = s * jnp.exp(lw) + u * b + v_col * k
    so_ref[...] = s_new
    y_col = jnp.sum(s_new * r, axis=-1, keepdims=True)
    y_ref[...] = jnp.sum(y_col * eye, axis=-2, keepdims=True)


def _step(tok, state, nb):
    bsz = state.shape[0]
    return pl.pallas_call(
        _step_kernel,
        grid=(bsz // nb,),
        in_specs=[pl.BlockSpec((nb, 6, N_HEADS, 1, HEAD_DIM), lambda i: (i, 0, 0, 0, 0)),
                  pl.BlockSpec((nb, N_HEADS, HEAD_DIM, HEAD_DIM), lambda i: (i, 0, 0, 0))],
        out_specs=[pl.BlockSpec((nb, N_HEADS, 1, HEAD_DIM), lambda i: (i, 0, 0, 0)),
                   pl.BlockSpec((nb, N_HEADS, HEAD_DIM, HEAD_DIM), lambda i: (i, 0, 0, 0))],
        out_shape=[jax.ShapeDtypeStruct((bsz, N_HEADS, 1, HEAD_DIM), F32),
                   jax.ShapeDtypeStruct((bsz, N_HEADS, HEAD_DIM, HEAD_DIM), F32)],
        compiler_params=_params(("parallel",)),
        name="rwkv_step",
    )(tok, state)


def _log_forget(f_ref, bf_ref):
    x = f_ref[...] + bf_ref[...]
    lf = jnp.minimum(x, 0.0) - jnp.log(1.0 + jnp.exp(-jnp.abs(x)))
    lane = lax.broadcasted_iota(jnp.int32, lf.shape, 1)
    return jnp.where(lane < N_HEADS, lf, 0.0)


def _foxprep_kernel(q_ref, k_ref, f_ref, qn_ref, kn_ref, bf_ref, ones_bd, qo_ref, ko_ref, lf_ref):
    q = q_ref[...]
    k = k_ref[...]
    inv = 1.0 / HEAD_DIM
    qo_ref[...] = (q * lax.rsqrt(_head_sum(q * q, ones_bd[...]) * inv + NORM_EPS) * qn_ref[...]
                   * (HEAD_DIM ** -0.5))
    ko_ref[...] = k * lax.rsqrt(_head_sum(k * k, ones_bd[...]) * inv + NORM_EPS) * kn_ref[...]
    lf_ref[...] = _log_forget(f_ref, bf_ref)


def _logf_cumsum_kernel(n_pad, f_ref, bf_ref, tri_ref, lf_ref, c_ref):
    lf = _log_forget(f_ref, bf_ref)
    row = lax.broadcasted_iota(jnp.int32, lf.shape, 0)
    lf = jnp.where(row >= n_pad, lf, 0.0)
    lf_ref[...] = lf
    carry = jnp.zeros((1, LANES), F32)
    for blk in range(lf.shape[0] // LANES):
        rows = slice(blk * LANES, (blk + 1) * LANES)
        cb = _dot_x(tri_ref[...], lf[rows, :], nb=3) + carry
        c_ref[rows, :] = cb
        carry = cb[LANES - 1:LANES, :]


def _logf_cumsum(zf, bf_row, t_len, n_pad):
    m = zf.shape[0]
    tri = jnp.asarray(np.tril(np.ones((LANES, LANES), np.float32)))
    row = pl.BlockSpec((t_len, LANES), lambda i: (i, 0))
    return pl.pallas_call(
        functools.partial(_logf_cumsum_kernel, n_pad),
        grid=(m // t_len,),
        in_specs=[row, _resident((1, LANES)), _resident((LANES, LANES))],
        out_specs=[row, row],
        out_shape=[jax.ShapeDtypeStruct((m, LANES), F32)] * 2,
        compiler_params=_params(("parallel",)),
        name="logf_cumsum",
    )(zf, bf_row, tri)


def _foxprep(zq, zf, qn_w, kn_w, bf_row, ones_bd):
    m = zq.shape[0]
    col = lambda j: pl.BlockSpec((m, WIDTH), lambda i: (0, j))
    row = lambda n: pl.BlockSpec((m, n), lambda i: (0, 0))
    return pl.pallas_call(
        _foxprep_kernel,
        grid=(1,),
        in_specs=[col(0), col(1), row(LANES), _resident((1, WIDTH)), _resident((1, WIDTH)),
                  _resident((1, LANES)), _resident((WIDTH, WIDTH))],
        out_specs=[row(WIDTH), row(WIDTH), row(LANES)],
        out_shape=[jax.ShapeDtypeStruct((m, WIDTH), F32), jax.ShapeDtypeStruct((m, WIDTH), F32),
                   jax.ShapeDtypeStruct((m, LANES), F32)],
        compiler_params=_params(("arbitrary",)),
        name="fox_prep",
    )(zq, zq, zf, qn_w, kn_w, bf_row, ones_bd)


def _foxattn_kernel(n_pad, n_qtiles, qr_ref, kr_ref, v_ref, c_ref, qw_ref, kw_ref, ones_ref,
                    o_ref, ko_ref, vo_ref, q_ref, k_ref, qa_ref, ka_ref, va_ref):
    t_len = qr_ref.shape[0]
    tq = t_len // n_qtiles
    pair = pl.program_id(1)
    inv = 1.0 / HEAD_DIM
    q_raw = qr_ref[...]
    k_raw = kr_ref[...]
    q_ref[...] = (q_raw * lax.rsqrt(_head_sum(q_raw * q_raw, ones_ref[...]) * inv + NORM_EPS) * qw_ref[...]
                  * (HEAD_DIM ** -0.5))
    k_ref[...] = k_raw * lax.rsqrt(_head_sum(k_raw * k_raw, ones_ref[...]) * inv + NORM_EPS) * kw_ref[...]
    ko_ref[...] = k_ref[n_pad:, :]
    vo_ref[...] = v_ref[n_pad:, :]
    lane = lax.broadcasted_iota(jnp.int32, (1, LANES), 1)
    row = lax.broadcasted_iota(jnp.int32, (t_len, 1), 0)
    causal = (lax.broadcasted_iota(jnp.int32, (tq, tq), 1) <= lax.broadcasted_iota(jnp.int32, (tq, tq), 0))
    for h in range(2):
        hmask = (lane // HEAD_DIM == h).astype(F32)
        sel = (lane == 2 * pair + h).astype(F32)
        c = jnp.sum(c_ref[...] * sel, axis=-1, keepdims=True)
        hi = c.astype(BF16).astype(F32)
        mid = (c - hi).astype(BF16).astype(F32)
        lo = (c - hi - mid).astype(BF16).astype(F32)
        base = HEAD_DIM * (1 - h)
        at = lambda i: (lane == base + i).astype(F32)
        ones_q = at(3) + at(4) + at(5)
        ones_k = at(0) + at(1) + at(2)
        hi_k = jnp.where(row >= n_pad, hi, -NEG_INF)
        qa_ref[...] = (q_ref[...] * hmask + hi * at(0) + mid * at(1) + lo * at(2) + ones_q).astype(BF16)
        ka_ref[...] = (k_ref[...] * hmask + ones_k - hi_k * at(3) - mid * at(4) - lo * at(5)).astype(BF16)
        va_ref[...] = (v_ref[...] * hmask + at(0)).astype(BF16)
        for qi in range(n_qtiles):
            q0 = qi * tq
            qa = qa_ref[q0:q0 + tq, :]
            s_diag = jnp.where(causal, _dot_nt(qa, ka_ref[q0:q0 + tq, :]), NEG_INF)
            m = jnp.max(s_diag, axis=-1, keepdims=True)
            if qi > 0:
                s_past = _dot_nt(qa, ka_ref[0:q0, :])
                m = jnp.maximum(m, jnp.max(s_past, axis=-1, keepdims=True))
            out = _dot(jnp.exp(s_diag - m), va_ref[q0:q0 + tq, :])
            if qi > 0:
                out = out + _dot(jnp.exp(s_past - m), va_ref[0:q0, :])
            l = jnp.sum(out * at(0), axis=-1, keepdims=True)
            out = out / l * hmask
            if h == 0:
                o_ref[q0:q0 + tq, :] = out
            else:
                o_ref[q0:q0 + tq, :] += out


def _foxattn(zq, cum, qn_w, kn_w, bsz, t_len, n_pad, n_qtiles):
    seq = lambda j0: pl.BlockSpec((t_len, LANES), lambda b, p: (b, j0 + p))
    real = pl.BlockSpec((None, t_len - n_pad, LANES), lambda b, p: (b, 0, p))
    pair_w = pl.BlockSpec((1, LANES), lambda b, p: (0, p))
    ones = jnp.asarray(_np_ones_bd(LANES, HEAD_DIM), BF16)
    blocks = WIDTH // LANES
    return pl.pallas_call(
        functools.partial(_foxattn_kernel, n_pad, n_qtiles),
        grid=(bsz, N_HEADS // 2),
        in_specs=[seq(0), seq(blocks), seq(2 * blocks), pl.BlockSpec((t_len, LANES), lambda b, p: (b, 0)),
                  pair_w, pair_w, _resident((LANES, LANES))],
        out_specs=[seq(0), real, real],
        out_shape=[jax.ShapeDtypeStruct((bsz * t_len, WIDTH), F32),
                   jax.ShapeDtypeStruct((bsz, t_len - n_pad, WIDTH), F32),
                   jax.ShapeDtypeStruct((bsz, t_len - n_pad, WIDTH), F32)],
        scratch_shapes=[pltpu.VMEM((t_len, LANES), F32)] * 2 + [pltpu.VMEM((t_len, LANES), BF16)] * 3,
        compiler_params=_params(("parallel", "parallel")),
        name="fox_attn",
    )(zq, zq, zq, cum, qn_w, kn_w, ones)


def _dec_consts():
    t = np.arange(PAGE)
    suf = (t[:, None] > t[None, :]).astype(np.float32)
    bdm = (np.arange(WIDTH)[None, :] // HEAD_DIM == np.arange(8)[:, None]).astype(np.float32)
    eye = (np.arange(LANES)[None, :] == np.arange(8)[:, None]).astype(np.float32)
    return suf, bdm, eye


def _dec_kernel(n_group, pt_ref, q_ref, kn_ref, vn_ref, lfn_ref, suf_ref, bdm_ref, eye_ref, *rest):
    k_refs = rest[0:n_group]
    v_refs = rest[n_group:2 * n_group]
    lf_refs = rest[2 * n_group:3 * n_group]
    o_ref = rest[3 * n_group]
    m_ref, l_ref, acc_ref, carry_ref = rest[3 * n_group + 1:]
    j = pl.program_id(1)
    bdm = bdm_ref[...]
    qbd = q_ref[...] * bdm
    lf_new = jnp.sum(eye_ref[...] * lfn_ref[...], axis=-1, keepdims=True)

    @pl.when(j == 0)
    def _():
        m_ref[...] = jnp.broadcast_to(jnp.sum(qbd * kn_ref[...], axis=-1, keepdims=True), (8, LANES))
        l_ref[...] = jnp.ones((8, LANES), F32)
        acc_ref[...] = vn_ref[...] * bdm
        carry_ref[...] = jnp.zeros((8, LANES), F32)

    carry = carry_ref[:, 0:1]
    qb = qbd.astype(BF16)
    scores = []
    for g in range(n_group):
        lf_t = lf_refs[g][...]
        suffix = _dot_x(lf_t, suf_ref[...], na=3)
        kt = k_refs[g][...].reshape(WIDTH, PAGE)
        scores.append(_dot(qb, kt) + (lf_new + carry + suffix))
        carry = carry + jnp.sum(lf_t, axis=-1, keepdims=True)
    carry_ref[...] = jnp.broadcast_to(carry, (8, LANES))
    s_all = jnp.concatenate(scores, axis=1)
    m_old = m_ref[:, 0:1]
    m_new = jnp.maximum(m_old, jnp.max(s_all, axis=-1, keepdims=True))
    alpha = jnp.exp(m_old - m_new)
    p = jnp.exp(s_all - m_new)
    l_ref[...] = jnp.broadcast_to(alpha * l_ref[:, 0:1] + jnp.sum(p, axis=-1, keepdims=True), (8, LANES))
    m_ref[...] = jnp.broadcast_to(m_new, (8, LANES))
    acc = alpha * acc_ref[...]
    pb = p.astype(BF16)
    for g in range(n_group):
        vt = v_refs[g][...].reshape(WIDTH, PAGE)
        acc = acc + _dot_nt(pb[:, g * PAGE:(g + 1) * PAGE], vt)
    acc_ref[...] = acc

    @pl.when(j == pl.num_programs(1) - 1)
    def _():
        o_ref[...] = jnp.sum(acc_ref[...] * bdm / l_ref[:, 0:1], axis=0, keepdims=True)


def _dec(page_table, qn, kn, vn, lfn, cache_k, cache_v, cache_lf, n_group):
    bsz, n_pages = page_table.shape
    kt = jnp.transpose(cache_k, (0, 2, 3, 1))
    vt = jnp.transpose(cache_v, (0, 2, 3, 1))
    lft = jnp.transpose(cache_lf, (0, 2, 1))
    consts = [jnp.asarray(c) for c in _dec_consts()]
    n_steps = n_pages // n_group
    tok = lambda n: pl.BlockSpec((None, 1, n), lambda b, j, pt: (b, 0, 0))
    const = lambda c: pl.BlockSpec(c.shape, lambda b, j, pt: (0,) * c.ndim, pipeline_mode=pl.Buffered(1))

    def page(shape, g):
        def index(b, j, pt):
            return (pt[b * n_pages + (n_pages - 1 - (j * n_group + g))],) + (0,) * len(shape)
        return pl.BlockSpec((None,) + shape, index)

    grid_spec = pltpu.PrefetchScalarGridSpec(
        num_scalar_prefetch=1,
        grid=(bsz, n_steps),
        in_specs=([tok(WIDTH), tok(WIDTH), tok(WIDTH), tok(LANES)] + [const(c) for c in consts]
                  + [page((N_HEADS, HEAD_DIM, PAGE), g) for g in range(n_group)]
                  + [page((N_HEADS, HEAD_DIM, PAGE), g) for g in range(n_group)]
                  + [page((N_HEADS, PAGE), g) for g in range(n_group)]),
        out_specs=tok(WIDTH),
        scratch_shapes=[pltpu.VMEM((8, LANES), F32), pltpu.VMEM((8, LANES), F32),
                        pltpu.VMEM((8, WIDTH), F32), pltpu.VMEM((8, LANES), F32)],
    )
    r3 = lambda x: x.reshape(bsz, 1, x.shape[-1])
    return pl.pallas_call(
        functools.partial(_dec_kernel, n_group),
        grid_spec=grid_spec,
        out_shape=jax.ShapeDtypeStruct((bsz, 1, WIDTH), F32),
        compiler_params=_params(("parallel", "arbitrary")),
        name="fox_decode",
    )(page_table.reshape(-1), r3(qn), r3(kn), r3(vn), r3(lfn), *consts,
      *([kt] * n_group), *([vt] * n_group), *([lft] * n_group)).reshape(bsz, WIDTH)


def _merge_kernel(x_ref, y_ref, bonus_ref, g_ref, fox_ref, zg_ref, gng_ref, gnb_ref, ones_bd,
                  wa_ref, wb_ref, wo_ref, o_ref):
    y = y_ref[...]
    inv = 1.0 / HEAD_DIM
    mean = _head_sum(y, ones_bd[...]) * inv
    d = y - mean
    var = _head_sum(d * d, ones_bd[...]) * inv
    yn = d * lax.rsqrt(var + GN_EPS) * gng_ref[...] + gnb_ref[...]
    rw = (yn + bonus_ref[...]) * g_ref[...]
    zg = zg_ref[...]
    merged = (_sigmoid(zg[:, 0:D_MODEL]) * _dot(rw, wa_ref[...])
              + _sigmoid(zg[:, D_MODEL:]) * _dot(fox_ref[...], wb_ref[...]))
    o_ref[...] = x_ref[...] + _dot(merged, wo_ref[...])


def _merge(x1, y, bonus, g, fox, zg, gn_g, gn_b, ones_bd, w_a, w_b, w_o, tm, keep=None):
    m = x1.shape[0]
    out_row = pl.BlockSpec((tm, D_MODEL), lambda i: (i, 0))
    if keep is None:
        m_out = m
        row = lambda n: pl.BlockSpec((tm, n), lambda i: (i, 0))
    else:
        seq_len, start = keep
        assert (seq_len - start) % tm == 0 and start % 8 == 0
        per_seq = (seq_len - start) // tm
        m_out = (m // seq_len) * per_seq * tm
        row = lambda n: pl.BlockSpec(
            (pl.Element(tm), pl.Element(n)),
            lambda i: (pl.multiple_of((i // per_seq) * seq_len + start + (i % per_seq) * tm, 8), 0))
    return pl.pallas_call(
        _merge_kernel,
        grid=(m_out // tm,),
        in_specs=[row(D_MODEL), row(WIDTH), row(WIDTH), row(WIDTH), row(WIDTH), row(2 * D_MODEL),
                  _resident((1, WIDTH)), _resident((1, WIDTH)), _resident((WIDTH, WIDTH)),
                  _resident((WIDTH, D_MODEL)), _resident((WIDTH, D_MODEL)), _resident((D_MODEL, D_MODEL))],
        out_specs=out_row,
        out_shape=jax.ShapeDtypeStruct((m_out, D_MODEL), F32),
        compiler_params=_params(("parallel",)),
        name="merge",
    )(x1, y, bonus, g, fox, zg, gn_g, gn_b, ones_bd, w_a, w_b, w_o)


def _to_blockdiag(state):
    bsz = state.shape[0]
    s = state.reshape(bsz, 2, 4, HEAD_DIM, HEAD_DIM)
    eye = jnp.eye(4, dtype=state.dtype)
    out = jnp.einsum("bqhvk,hg->bqhvgk", s, eye)
    return out.reshape(bsz, 2, QUAD, QUAD)


def _from_blockdiag(sbd):
    bsz = sbd.shape[0]
    s = sbd.reshape(bsz, 2, 4, HEAD_DIM, 4, HEAD_DIM)
    idx = jnp.arange(4)
    s = s[:, :, idx, :, idx, :]
    return jnp.moveaxis(s, 0, 2).reshape(bsz, N_HEADS, HEAD_DIM, HEAD_DIM)


def _layer_weights(l, ffn1_norm, ffn1_w_gate, ffn1_w_up, ffn1_w_down, mix_norm, w_in, mu_shift, w0,
                   w_decay_up, a0, w_aaa_up, w_gate_up, k_k, k_a, r_k, gn_g, gn_b, q_norm, k_norm, b_f,
                   w_a, w_b, w_o, ffn2_norm, ffn2_w_gate, ffn2_w_up, ffn2_w_down):
    row = lambda x: x.reshape(1, -1).astype(F32)
    bf = lambda x: x.astype(BF16)
    wi = w_in[l]
    w_f = jnp.pad(wi[:, R_COLS + QKV_COLS:R_COLS + QKV_COLS + N_HEADS], ((0, 0), (0, LANES - N_HEADS)))
    return dict(
        ffn1=(row(ffn1_norm[l]), bf(ffn1_w_gate[l]), bf(ffn1_w_up[l]), bf(ffn1_w_down[l])),
        ffn2=(row(ffn2_norm[l]), bf(ffn2_w_gate[l]), bf(ffn2_w_up[l]), bf(ffn2_w_down[l])),
        mix_norm=row(mix_norm[l]),
        w_r=bf(wi[:, 0:R_COLS]),
        w_l=bf(jnp.pad(wi[:, 3 * WIDTH:R_COLS], ((0, 0), (0, LORA_COLS - (R_COLS - 3 * WIDTH))))),
        w_q=bf(wi[:, R_COLS:R_COLS + QKV_COLS]), w_f=bf(w_f),
        w_g=bf(wi[:, R_COLS + QKV_COLS + N_HEADS:]),
        prep=dict(mu=row(mu_shift[l]), w0=row(w0[l]), wdu=bf(w_decay_up[l]), a0=row(a0[l]),
                  wau=bf(w_aaa_up[l]), wgu=bf(w_gate_up[l]), k_k=row(k_k[l]), k_a=row(k_a[l]),
                  r_k=row(r_k[l]), ones_bd=jnp.asarray(_np_ones_bd(WIDTH, HEAD_DIM), BF16)),
        gn_g=row(gn_g[l]), gn_b=row(gn_b[l]),
        q_norm=row(jnp.tile(q_norm[l], N_HEADS)), k_norm=row(jnp.tile(k_norm[l], N_HEADS)),
        b_f=jnp.pad(row(b_f[l]), ((0, 0), (0, LANES - N_HEADS))),
        w_a=bf(w_a[l]), w_b=bf(w_b[l]), w_o=bf(w_o[l]),
    )


def _prompt_layer(x1, w, bsz, t_len, n_pad, last):
    ones_bd = w["prep"]["ones_bd"]
    zr, zl, zq, zf, zg = _proj(x1, w["mix_norm"], w["w_r"], w["w_l"], w["w_q"], w["w_f"], w["w_g"], tm=272)
    s0 = jnp.zeros((bsz, 2, QUAD, QUAD), F32)
    flat = lambda arr: arr.reshape(bsz * t_len, WIDTH)
    y, g, bonus, s_t = _rec(zr.reshape(bsz, t_len, R_COLS), zl.reshape(bsz, t_len, LORA_COLS), s0, w["prep"],
                            c=64, first_chunk=n_pad // 64)
    y, g, bonus = flat(y), flat(g), flat(bonus)
    lf, cum = _logf_cumsum(zf, w["b_f"], t_len, n_pad)
    fox, kn, vv = _foxattn(zq, cum, w["q_norm"], w["k_norm"], bsz, t_len, n_pad, n_qtiles=4)
    x2 = _merge(x1, y, bonus, g, fox, zg, w["gn_g"], w["gn_b"], ones_bd,
                w["w_a"], w["w_b"], w["w_o"], tm=512 if last else 544,
                keep=(t_len, n_pad + N_META) if last else None)
    x3 = _ffn(x2, *w["ffn2"], tm=1024 if last else 1088)
    return x3, zr, kn, vv, lf, s_t


def _sample_layer(x1, w, shift_prev, wkv0, page_table, cache_k, cache_v, cache_lf):
    bsz = x1.shape[0]
    ones_bd = w["prep"]["ones_bd"]
    zr, _, zq, zf, zg = _proj(x1, w["mix_norm"], w["w_r"], w["w_l"], w["w_q"], w["w_f"], w["w_g"], tm=bsz)
    prepped = _prep(zr, shift_prev, w["prep"], tm=bsz)
    g, bonus = prepped[6], prepped[7]
    tok = jnp.stack(prepped[:6], axis=1).reshape(bsz, 6, N_HEADS, 1, HEAD_DIM)
    y, s_t = _step(tok, wkv0.astype(F32), nb=8)
    y = y.reshape(bsz, WIDTH)
    qn, kn, lf = _foxprep(zq, zf, w["q_norm"], w["k_norm"], w["b_f"], ones_bd)
    vn = zq[:, 2 * WIDTH:]
    fox = _dec(page_table, qn, kn, vn, lf, cache_k, cache_v, cache_lf, n_group=32)
    x2 = _merge(x1, y, bonus, g, fox, zg, w["gn_g"], w["gn_b"], ones_bd, w["w_a"], w["w_b"], w["w_o"],
                tm=bsz)
    x3 = _ffn(x2, *w["ffn2"], tm=bsz)
    return x3, zr, kn, vn, lf, s_t


def kernel(x_prompt, x_sample, cache_k, cache_v, cache_logf, state_wkv, state_shift, page_table, meta_tokens, ffn1_norm, ffn1_w_gate, ffn1_w_up, ffn1_w_down, mix_norm, w_in, mu_shift, w0, w_decay_up, a0, w_aaa_up, w_gate_up, k_k, k_a, r_k, gn_g, gn_b, q_norm, k_norm, b_f, w_a, w_b, w_o, ffn2_norm, ffn2_w_gate, ffn2_w_up, ffn2_w_down):
    depth = w_in.shape[0]
    bp, seq, _ = x_prompt.shape
    bs = x_sample.shape[0]
    l_tok = seq + N_META
    n_pad = (-l_tok) % LANES
    t_len = l_tok + n_pad
    head = jnp.concatenate([jnp.zeros((n_pad, D_MODEL), F32), meta_tokens.astype(F32)], axis=0)
    h_p = x_prompt.reshape(bp * seq, D_MODEL).astype(F32)
    h_s = x_sample.reshape(bs, D_MODEL).astype(F32)
    rows_p, rows_s = [], []
    for l in range(depth):
        w = _layer_weights(l, ffn1_norm, ffn1_w_gate, ffn1_w_up, ffn1_w_down, mix_norm, w_in, mu_shift, w0,
                           w_decay_up, a0, w_aaa_up, w_gate_up, k_k, k_a, r_k, gn_g, gn_b, q_norm, k_norm,
                           b_f, w_a, w_b, w_o, ffn2_norm, ffn2_w_gate, ffn2_w_up, ffn2_w_down)
        if l == 0:
            small = _ffn(jnp.concatenate([h_s, head], axis=0), *w["ffn1"], tm=bs + n_pad + N_META)
            x1_s, head1 = small[:bs], small[bs:]
            x1_p = _ffn(h_p, *w["ffn1"], tm=1024, scatter=(t_len, n_pad + N_META, bp * t_len))
            x1_p = _place_rows(x1_p, head1, t_len)
        else:
            x1_s = _ffn(h_s, *w["ffn1"], tm=bs)
            x1_p = _ffn(h_p, *w["ffn1"], tm=544)
        h_p, zr, kn, vv, lf, s_t = _prompt_layer(x1_p, w, bp, t_len, n_pad, last=(l == depth - 1))
        rows_p.append((kn.reshape(bp, l_tok, N_HEADS, HEAD_DIM), vv.reshape(bp, l_tok, N_HEADS, HEAD_DIM),
                       lf.reshape(bp, t_len, LANES)[:, n_pad:, :N_HEADS],
                       _from_blockdiag(s_t),
                       zr.reshape(bp, t_len, R_COLS)[:, -1]))
        h_s, zr_s, kn_s, vn_s, lf_s, s_ts = _sample_layer(
            x1_s, w, state_shift[l].astype(F32), state_wkv[l], page_table, cache_k[l], cache_v[l],
            cache_logf[l])
        rows_s.append((kn_s.reshape(bs, 1, N_HEADS, HEAD_DIM), vn_s.reshape(bs, 1, N_HEADS, HEAD_DIM),
                       lf_s[:, :N_HEADS].reshape(bs, 1, N_HEADS), s_ts, zr_s))
    y_prompt = h_p.reshape(bp, seq, D_MODEL)
    y_sample = h_s.reshape(bs, 1, D_MODEL)
    stk = lambda rows, i: jnp.stack([r[i] for r in rows], axis=0)
    return (y_prompt, y_sample,
            stk(rows_p, 0), stk(rows_p, 1), stk(rows_p, 2), stk(rows_p, 3), stk(rows_p, 4),
            stk(rows_s, 0), stk(rows_s, 1), stk(rows_s, 2), stk(rows_s, 3), stk(rows_s, 4))
```

```python
import functools

import numpy as np
import jax
import jax.numpy as jnp
from jax import lax
from jax.experimental import pallas as pl
from jax.experimental.pallas import tpu as pltpu

F32 = jnp.float32
BF16 = jnp.bfloat16

D_MODEL = 1024
HEAD_DIM = 64
N_HEADS = 8
WIDTH = N_HEADS * HEAD_DIM
N_META = 16
DECAY_LORA = 64
AAA_LORA = 64
GATE_LORA = 160
FFN_DIM = 2816
R_COLS = 3 * WIDTH + DECAY_LORA + AAA_LORA + GATE_LORA
QKV_COLS = 3 * WIDTH
NORM_EPS = 1e-6
GN_EPS = HEAD_DIM * 1e-5
NEG_INF = -1e30
PAGE = 128
LANES = 128
QUAD = 4 * HEAD_DIM
FFN_CHUNK = 256
VMEM_LIMIT = 56 * 1024 * 1024
REC_VMEM_LIMIT = 62 * 1024 * 1024


def _dot(a, b):
    return jnp.dot(a.astype(BF16), b.astype(BF16), preferred_element_type=F32)


def _dot_nt(a, b):
    return lax.dot_general(a.astype(BF16), b.astype(BF16), (((1,), (1,)), ((), ())),
                           preferred_element_type=F32)


def _dot_tn(a, b):
    return lax.dot_general(a.astype(BF16), b.astype(BF16), (((0,), (0,)), ((), ())),
                           preferred_element_type=F32)


def _split(x, n):
    parts = []
    r = x
    for i in range(n):
        p = r.astype(BF16)
        parts.append(p)
        if i + 1 < n:
            r = r - p.astype(F32)
    return parts


def _dot_x(a, b, dot=_dot, na=1, nb=1):
    pa = _split(a, na) if na > 1 else [a]
    pb = _split(b, nb) if nb > 1 else [b]
    out = None
    for i, x in enumerate(pa):
        for j, y in enumerate(pb):
            if i + j >= max(na, nb):
                continue
            t = dot(x, y)
            out = t if out is None else out + t
    return out


def _sigmoid(x):
    return 1.0 / (1.0 + jnp.exp(-x))


def _rms(x, g):
    ms = jnp.mean(x * x, axis=-1, keepdims=True)
    return x * lax.rsqrt(ms + NORM_EPS) * g


def _head_sum(x, ones_bd):
    return _dot_x(x, ones_bd, na=2)


def _params(sem, vmem=VMEM_LIMIT):
    return pltpu.CompilerParams(dimension_semantics=sem, vmem_limit_bytes=vmem)


def _resident(shape):
    nd = len(shape)
    return pl.BlockSpec(shape, lambda *_: (0,) * nd, pipeline_mode=pl.Buffered(1))


def _np_ones_bd(n, blk):
    i = np.arange(n)
    return (i[:, None] // blk == i[None, :] // blk).astype(np.float32)


def _ffn_kernel(x_ref, g_ref, wg_ref, wu_ref, wd_ref, o_ref, acc_ref):
    x = x_ref[...]
    xn = _rms(x, g_ref[...]).astype(BF16)
    for c in range(FFN_DIM // FFN_CHUNK):
        sl = slice(c * FFN_CHUNK, (c + 1) * FFN_CHUNK)
        gate = _dot(xn, wg_ref[:, sl])
        up = _dot(xn, wu_ref[:, sl])
        h = (gate * _sigmoid(gate) * up).astype(BF16)
        part = _dot(h, wd_ref[sl, :])
        if c == 0:
            acc_ref[...] = part
        else:
            acc_ref[...] += part
    o_ref[...] = x + 0.5 * acc_ref[...]


def _ffn(x, norm_g, w_gate, w_up, w_down, tm, scatter=None):
    m = x.shape[0]
    row = pl.BlockSpec((tm, D_MODEL), lambda i: (i, 0))
    out_row, m_out = row, m
    if scatter is not None:
        seq_len, start, m_out = scatter
        assert (seq_len - start) % tm == 0 and start % 8 == 0 and seq_len % 8 == 0
        per_seq = (seq_len - start) // tm
        out_row = pl.BlockSpec(
            (pl.Element(tm), pl.Element(D_MODEL)),
            lambda i: (pl.multiple_of((i // per_seq) * seq_len + start + (i % per_seq) * tm, 8), 0))
    return pl.pallas_call(
        _ffn_kernel,
        grid=(m // tm,),
        in_specs=[row, _resident((1, D_MODEL)), _resident((D_MODEL, FFN_DIM)),
                  _resident((D_MODEL, FFN_DIM)), _resident((FFN_DIM, D_MODEL))],
        out_specs=out_row,
        out_shape=jax.ShapeDtypeStruct((m_out, D_MODEL), F32),
        scratch_shapes=[pltpu.VMEM((tm, D_MODEL), F32)],
        compiler_params=_params(("parallel",)),
        name="ffn",
    )(x, norm_g, w_gate, w_up, w_down)


def _place_kernel(rows_ref, big_ref, o_ref):
    del big_ref
    o_ref[...] = rows_ref[...]


def _place_rows(big, rows, seq_len):
    n = rows.shape[0]
    assert n % 8 == 0 and seq_len % 8 == 0
    return pl.pallas_call(
        _place_kernel,
        grid=(big.shape[0] // seq_len,),
        in_specs=[_resident(rows.shape), pl.BlockSpec(memory_space=pl.ANY)],
        out_specs=pl.BlockSpec((pl.Element(n), pl.Element(D_MODEL)),
                               lambda b: (pl.multiple_of(b * seq_len, 8), 0)),
        out_shape=jax.ShapeDtypeStruct(big.shape, big.dtype),
        input_output_aliases={1: 0},
        compiler_params=_params(("arbitrary",)),
        name="place_rows",
    )(rows, big)


LORA_COLS = 3 * LANES


def _proj_kernel(x_ref, g_ref, wr_ref, wl_ref, wq_ref, wf_ref, wg_ref, zr_ref, zl_ref, zq_ref, zf_ref, zg_ref):
    xn = _rms(x_ref[...], g_ref[...]).astype(BF16)
    zr_ref[...] = jnp.dot(xn, wr_ref[...], preferred_element_type=F32)
    zl_ref[...] = jnp.dot(xn, wl_ref[...], preferred_element_type=F32)
    zq_ref[...] = jnp.dot(xn, wq_ref[...], preferred_element_type=F32)
    zf_ref[...] = jnp.dot(xn, wf_ref[...], preferred_element_type=F32)
    zg_ref[...] = jnp.dot(xn, wg_ref[...], preferred_element_type=F32)


def _proj(x, norm_g, w_r, w_l, w_q, w_f, w_g, tm):
    m = x.shape[0]
    row = lambda n: pl.BlockSpec((tm, n), lambda i: (i, 0))
    widths = [R_COLS, LORA_COLS, QKV_COLS, LANES, 2 * D_MODEL]
    return pl.pallas_call(
        _proj_kernel,
        grid=(m // tm,),
        in_specs=[row(D_MODEL), _resident((1, D_MODEL))] + [_resident((D_MODEL, n)) for n in widths],
        out_specs=[row(n) for n in widths],
        out_shape=[jax.ShapeDtypeStruct((m, n), F32) for n in widths],
        compiler_params=_params(("parallel",)),
        name="in_proj",
    )(x, norm_g, w_r, w_l, w_q, w_f, w_g)


def _prep_math(z, zprev, mu, w0, wdu, a0, wau, wgu, k_k, k_a, r_k, ones_bd, outs):
    r_ref, lw_ref, k_ref, v_ref, a_ref, b_ref, g_ref, bonus_ref = outs
    zs = z + (zprev - z) * mu
    r = zs[:, 0:WIDTH]
    k = zs[:, WIDTH:2 * WIDTH]
    v = zs[:, 2 * WIDTH:3 * WIDTH]
    o = 3 * WIDTH
    d_lo = zs[:, o:o + DECAY_LORA]
    a_lo = zs[:, o + DECAY_LORA:o + DECAY_LORA + AAA_LORA]
    g_lo = zs[:, o + DECAY_LORA + AAA_LORA:R_COLS]
    wpre = w0 + _dot(jnp.tanh(d_lo), wdu)
    y = -wpre
    softplus = jnp.maximum(y, 0.0) + jnp.log(1.0 + jnp.exp(-jnp.abs(y)))
    w_log = -softplus - 0.5
    lw_ref[...] = -jnp.exp(w_log)
    a = _sigmoid(a0 + _dot(a_lo, wau))
    g_ref[...] = _dot(_sigmoid(g_lo), wgu)
    kk = k * k_k
    k2 = k * (1.0 + (a - 1.0) * k_a)
    kk = kk * lax.rsqrt(_head_sum(kk * kk, ones_bd) + 1e-12)
    r_ref[...] = r
    k_ref[...] = k2
    v_ref[...] = v
    a_ref[...] = -kk
    b_ref[...] = kk * a
    bonus_ref[...] = _head_sum(r * k2 * r_k, ones_bd) * v


def _prep_kernel(z_ref, zp_ref, mu, w0, wdu, a0, wau, wgu, k_k, k_a, r_k, ones_bd, *outs):
    _prep_math(z_ref[...], zp_ref[...], mu[...], w0[...], wdu[...], a0[...], wau[...], wgu[...],
               k_k[...], k_a[...], r_k[...], ones_bd[...], outs)


def _prep(zr, zprev, pw, tm):
    m = zr.shape[0]
    row = lambda n: pl.BlockSpec((tm, n), lambda i: (i, 0))
    consts = [pw["mu"], pw["w0"], pw["wdu"], pw["a0"], pw["wau"], pw["wgu"], pw["k_k"], pw["k_a"],
              pw["r_k"], pw["ones_bd"]]
    return pl.pallas_call(
        _prep_kernel,
        grid=(m // tm,),
        in_specs=[row(R_COLS), row(R_COLS)] + [_resident(c.shape) for c in consts],
        out_specs=[row(WIDTH)] * 8,
        out_shape=[jax.ShapeDtypeStruct((m, WIDTH), F32)] * 8,
        compiler_params=_params(("parallel",)),
        name="rwkv_prep",
    )(zr, zprev, *consts)


def _rec_masks(c):
    t = np.arange(c)[:, None]
    hs = np.arange(4 * c)[None, :]
    s = hs % c
    strict = (s < t).astype(np.float32)
    incl = (s <= t).astype(np.float32)
    eye = (s == t).astype(np.float32)
    rows = np.arange(4 * c)[:, None]
    bd = (rows // c == hs // c).astype(np.float32)
    tri = (np.arange(c)[None, :] <= np.arange(c)[:, None]).astype(np.float32)
    lane = np.arange(QUAD)[None, :] // HEAD_DIM
    head = (lane == rows // c).astype(np.float32)
    sbd = _np_ones_bd(QUAD, HEAD_DIM)
    return strict, incl, eye, bd, tri, head, sbd


def _stack_heads(x, head):
    return jnp.concatenate([x.astype(BF16)] * 4, axis=0) * head


def _chunk_tokens(rows, zr_ref, zk_ref, zv_ref, zl_ref, vec_ref, mul_ref, wdu_ref, wau_ref, wgu_ref, ones_bd,
                  g_ref, bonus_ref):
    start, size = rows
    halo = pl.ds(pl.multiple_of(start - 8, 8), size + 8)

    def shifted(ref, mu):
        z = ref[halo, :]
        cur, prev = z[8:], pltpu.roll(z, 1, 0)[8:]
        return cur + (prev - cur) * mu

    vec = vec_ref[...]
    r = shifted(zr_ref, vec[0:1])
    k = shifted(zk_ref, vec[1:2])
    v = shifted(zv_ref, vec[2:3])
    lora = shifted(zl_ref, mul_ref[...])
    d_lo = lora[:, 0:DECAY_LORA]
    a_lo = lora[:, DECAY_LORA:DECAY_LORA + AAA_LORA]
    g_lo = lora[:, DECAY_LORA + AAA_LORA:DECAY_LORA + AAA_LORA + GATE_LORA]
    y = -(vec[3:4] + _dot(jnp.tanh(d_lo), wdu_ref[...]))
    softplus = jnp.maximum(y, 0.0) + jnp.log(1.0 + jnp.exp(-jnp.abs(y)))
    lw = -jnp.exp(-softplus - 0.5)
    a_lr = _sigmoid(vec[4:5] + _dot(a_lo, wau_ref[...]))
    out_rows = pl.ds(pl.multiple_of(start, 8), size)
    g_ref[out_rows, :] = _dot(_sigmoid(g_lo), wgu_ref[...])
    kk = k * vec[5:6]
    k2 = k * (1.0 + (a_lr - 1.0) * vec[6:7])
    kk = kk * lax.rsqrt(_head_sum(kk * kk, ones_bd) + 1e-12)
    bonus_ref[out_rows, :] = _head_sum(r * k2 * vec[7:8], ones_bd) * v
    return r, lw, k2, v, -kk, kk * a_lr


def _rec_kernel(c, first_chunk, n_doubling, group, zr_ref, zk_ref, zv_ref, zl_ref, s0_ref,
                vec_ref, mul_ref, wdu_ref, wau_ref, wgu_ref, onesbd_ref,
                strict_ref, incl_ref, eye_ref, bd_ref, tri_ref, head_ref, sbd_ref,
                y_ref, g_ref, bonus_ref, sT_ref, s_ref, ry_scr, yc_scr, tr_scr, cs_scr, wl_scr):
    t_len = zr_ref.shape[0]
    n_iter = t_len // c - first_chunk
    for ref in (y_ref, g_ref, bonus_ref):
        ref[0:first_chunk * c, :] = jnp.zeros((first_chunk * c, QUAD), F32)

    def to_bd(p_row):
        return jnp.concatenate([p_row.astype(BF16)] * 4, axis=0) * bd_ref[...]

    def chunk_rows(ci):
        return pl.ds(pl.multiple_of(ci * c, c), c)

    def prepare(gi, carry, tick=lambda: None):
        strict = strict_ref[...]
        incl2 = jnp.concatenate([incl_ref[...]] * 2, axis=1)
        head = head_ref[...]
        slots = [gi * group + j for j in range(group)]
        loaded = []
        for slot in slots:
            loaded.append(_chunk_tokens(((slot + first_chunk) * c, c), zr_ref, zk_ref, zv_ref, zl_ref,
                                        vec_ref, mul_ref, wdu_ref, wau_ref, wgu_ref, onesbd_ref[...],
                                        g_ref, bonus_ref))
        cums = [_dot_x(tri_ref[...], ld[1], nb=3) for ld in loaded]
        tick()
        work = []
        for slot, (r, lw, k, v, a, b), cum in zip(slots, loaded, cums):
            cum_last = cum[c - 1:c, :]
            e_inv = jnp.exp(-cum)
            at = (a * jnp.exp(cum - lw)).astype(BF16)
            rt = r * jnp.exp(cum)
            e_last = jnp.exp(cum_last - cum)
            wl_scr[slot] = jnp.broadcast_to(jnp.exp(cum_last), (8, QUAD))
            rhs = jnp.concatenate([_stack_heads(b * e_inv, head), _stack_heads(k * e_inv, head)], axis=0)
            gmat = _dot_nt(jnp.concatenate([at, rt.astype(BF16)], axis=0), rhs)
            work.append(dict(at=at, rt=rt, v=v, bc=(b * e_last).astype(BF16), kc=(k * e_last).astype(BF16),
                             gmat=gmat))
        tick()
        p_rows, t_rows = [], []
        for w in work:
            gmat = w.pop("gmat")
            n_row = gmat[0:c, 0:4 * c] * strict
            w["v_st"] = _stack_heads(w["v"], head)
            w["xv"] = _dot(gmat[0:c, 4 * c:8 * c] * strict, w["v_st"])
            w["n_r"] = (gmat[c:2 * c, :] * incl2).astype(BF16)
            p_rows.append(n_row)
            t_rows.append(eye_ref[...] + n_row)
        tick()
        for _ in range(n_doubling):
            t_bds = [to_bd(t) for t in t_rows]
            p_rows = [_dot(p, to_bd(p)) for p in p_rows]
            tick()
            t_rows = [t + _dot(p, t_bd) for t, p, t_bd in zip(t_rows, p_rows, t_bds)]
        tick()
        for slot, w, t_row in zip(slots, work, t_rows):
            both = jnp.concatenate([_stack_heads(w["at"], head), _stack_heads(w["xv"], head)], axis=1)
            t_both = _dot(t_row, both)
            ta, txv = t_both[:, 0:QUAD], t_both[:, QUAD:]
            n_rb = w["n_r"][:, 0:4 * c]
            ry_scr[slot] = (w["rt"] + _dot(n_rb, _stack_heads(ta, head))).astype(BF16)
            yc_scr[slot] = _dot(w["n_r"], jnp.concatenate([_stack_heads(txv, head), w["v_st"]],
                                                          axis=0))
            tr_scr[slot] = (_dot_tn(ta, w["bc"]) * sbd_ref[...]).astype(BF16)
            cs_scr[slot] = (_dot_tn(jnp.concatenate([txv, w["v"]], axis=0),
                                    jnp.concatenate([w["bc"], w["kc"]], axis=0)) * sbd_ref[...]).astype(BF16)
        return carry

    s_ref[...] = s0_ref[...]

    def advance(slot):
        rows = chunk_rows(slot + first_chunk)
        s_old = s_ref[...]
        s_bf = s_old.astype(BF16)
        y_ref[rows, :] = _dot_nt(ry_scr[slot], s_bf) + yc_scr[slot]
        s_ref[...] = s_old * wl_scr[slot][0:1, :] + _dot(s_bf, tr_scr[slot]) + cs_scr[slot]

    def both_passes(gi, carry):
        pending = list(range(group))

        def tick():
            if pending:
                advance((gi - 1) * group + pending.pop(0))

        prepare(gi, carry, tick)
        while pending:
            tick()
        return carry

    n_groups = n_iter // group
    prepare(0, 0)
    lax.fori_loop(1, n_groups, both_passes, 0)
    for j in range(group):
        advance((n_groups - 1) * group + j)
    sT_ref[...] = s_ref[...]


def _rec(zr, zl, s0, pw, c, first_chunk):
    bsz, t_len, _ = zr.shape
    assert first_chunk * c >= 8
    strict, incl, eye, bd, tri, head, sbd = (jnp.asarray(m) for m in _rec_masks(c))
    masks = [strict, incl, eye, bd.astype(BF16), tri, head.astype(BF16), sbd]
    n_doubling = max(int(np.ceil(np.log2(c))) - 1, 0)
    n_iter = t_len // c - first_chunk
    group = next(g for g in (11, 3, 1) if n_iter % g == 0)
    cols = lambda j0: pl.BlockSpec((None, t_len, QUAD), lambda b, q: (b, 0, j0 + q))
    seq = cols(0)
    st = pl.BlockSpec((None, None, QUAD, QUAD), lambda b, q: (b, q, 0, 0))
    quad_cols = lambda rows: pl.BlockSpec((rows, QUAD), lambda b, q: (0, q))
    mu = pw["mu"]
    vec = jnp.concatenate([mu[:, 0:WIDTH], mu[:, WIDTH:2 * WIDTH], mu[:, 2 * WIDTH:3 * WIDTH], pw["w0"],
                           pw["a0"], pw["k_k"], pw["k_a"], pw["r_k"]], axis=0)
    mu_l = jnp.pad(mu[:, 3 * WIDTH:], ((0, 0), (0, zl.shape[-1] - (R_COLS - 3 * WIDTH))))
    return pl.pallas_call(
        functools.partial(_rec_kernel, c, first_chunk, n_doubling, group),
        grid=(bsz, 2),
        in_specs=([cols(0), cols(2), cols(4), pl.BlockSpec((None, t_len, zl.shape[-1]), lambda b, q: (b, 0, 0)),
                   st, quad_cols(8), _resident(mu_l.shape), quad_cols(DECAY_LORA), quad_cols(AAA_LORA),
                   quad_cols(GATE_LORA), _resident((QUAD, QUAD))] + [_resident(m.shape) for m in masks]),
        out_specs=[seq, seq, seq, st],
        out_shape=[jax.ShapeDtypeStruct((bsz, t_len, WIDTH), F32)] * 3
                  + [jax.ShapeDtypeStruct((bsz, 2, QUAD, QUAD), F32)],
        scratch_shapes=[pltpu.VMEM((QUAD, QUAD), F32),
                        pltpu.VMEM((n_iter, c, QUAD), BF16), pltpu.VMEM((n_iter, c, QUAD), F32),
                        pltpu.VMEM((n_iter, QUAD, QUAD), BF16), pltpu.VMEM((n_iter, QUAD, QUAD), BF16),
                        pltpu.VMEM((n_iter, 8, QUAD), F32)],
        compiler_params=_params(("parallel", "parallel"), vmem=REC_VMEM_LIMIT),
        name="rwkv_rec",
    )(zr, zr, zr, zl, s0, vec, mu_l, pw["wdu"], pw["wau"], pw["wgu"], sbd.astype(BF16), *masks)


def _step_kernel(tok_ref, s_ref, y_ref, so_ref):
    s = s_ref[...]
    r, lw, k, v, a, b = (tok_ref[:, i] for i in range(6))
    eye = (lax.broadcasted_iota(jnp.int32, (HEAD_DIM, HEAD_DIM), 0)
           == lax.broadcasted_iota(jnp.int32, (HEAD_DIM, HEAD_DIM), 1)).astype(F32)
    u = jnp.sum(s * a, axis=-1, keepdims=True)
    v_col = jnp.sum(eye * v, axis=-1, keepdims=True)
    s_new = s * jnp.exp(lw) + u * b + v_col * k
    so_ref[...] = s_new
    y_col = jnp.sum(s_new * r, axis=-1, keepdims=True)
    y_ref[...] = jnp.sum(y_col * eye, axis=-2, keepdims=True)


def _step(tok, state, nb):
    bsz = state.shape[0]
    return pl.pallas_call(
        _step_kernel,
        grid=(bsz // nb,),
        in_specs=[pl.BlockSpec((nb, 6, N_HEADS, 1, HEAD_DIM), lambda i: (i, 0, 0, 0, 0)),
                  pl.BlockSpec((nb, N_HEADS, HEAD_DIM, HEAD_DIM), lambda i: (i, 0, 0, 0))],
        out_specs=[pl.BlockSpec((nb, N_HEADS, 1, HEAD_DIM), lambda i: (i, 0, 0, 0)),
                   pl.BlockSpec((nb, N_HEADS, HEAD_DIM, HEAD_DIM), lambda i: (i, 0, 0, 0))],
        out_shape=[jax.ShapeDtypeStruct((bsz, N_HEADS, 1, HEAD_DIM), F32),
                   jax.ShapeDtypeStruct((bsz, N_HEADS, HEAD_DIM, HEAD_DIM), F32)],
        compiler_params=_params(("parallel",)),
        name="rwkv_step",
    )(tok, state)


def _log_forget(f_ref, bf_ref):
    x = f_ref[...] + bf_ref[...]
    lf = jnp.minimum(x, 0.0) - jnp.log(1.0 + jnp.exp(-jnp.abs(x)))
    lane = lax.broadcasted_iota(jnp.int32, lf.shape, 1)
    return jnp.where(lane < N_HEADS, lf, 0.0)


def _foxprep_kernel(q_ref, k_ref, f_ref, qn_ref, kn_ref, bf_ref, ones_bd, qo_ref, ko_ref, lf_ref):
    q = q_ref[...]
    k = k_ref[...]
    inv = 1.0 / HEAD_DIM
    qo_ref[...] = (q * lax.rsqrt(_head_sum(q * q, ones_bd[...]) * inv + NORM_EPS) * qn_ref[...]
                   * (HEAD_DIM ** -0.5))
    ko_ref[...] = k * lax.rsqrt(_head_sum(k * k, ones_bd[...]) * inv + NORM_EPS) * kn_ref[...]
    lf_ref[...] = _log_forget(f_ref, bf_ref)


def _logf_cumsum_kernel(n_pad, f_ref, bf_ref, tri_ref, lf_ref, c_ref):
    lf = _log_forget(f_ref, bf_ref)
    row = lax.broadcasted_iota(jnp.int32, lf.shape, 0)
    lf = jnp.where(row >= n_pad, lf, 0.0)
    lf_ref[...] = lf
    carry = jnp.zeros((1, LANES), F32)
    for blk in range(lf.shape[0] // LANES):
        rows = slice(blk * LANES, (blk + 1) * LANES)
        cb = _dot_x(tri_ref[...], lf[rows, :], nb=3) + carry
        c_ref[rows, :] = cb
        carry = cb[LANES - 1:LANES, :]


def _logf_cumsum(zf, bf_row, t_len, n_pad):
    m = zf.shape[0]
    tri = jnp.asarray(np.tril(np.ones((LANES, LANES), np.float32)))
    row = pl.BlockSpec((t_len, LANES), lambda i: (i, 0))
    return pl.pallas_call(
        functools.partial(_logf_cumsum_kernel, n_pad),
        grid=(m // t_len,),
        in_specs=[row, _resident((1, LANES)), _resident((LANES, LANES))],
        out_specs=[row, row],
        out_shape=[jax.ShapeDtypeStruct((m, LANES), F32)] * 2,
        compiler_params=_params(("parallel",)),
        name="logf_cumsum",
    )(zf, bf_row, tri)


def _foxprep(zq, zf, qn_w, kn_w, bf_row, ones_bd):
    m = zq.shape[0]
    col = lambda j: pl.BlockSpec((m, WIDTH), lambda i: (0, j))
    row = lambda n: pl.BlockSpec((m, n), lambda i: (0, 0))
    return pl.pallas_call(
        _foxprep_kernel,
        grid=(1,),
        in_specs=[col(0), col(1), row(LANES), _resident((1, WIDTH)), _resident((1, WIDTH)),
                  _resident((1, LANES)), _resident((WIDTH, WIDTH))],
        out_specs=[row(WIDTH), row(WIDTH), row(LANES)],
        out_shape=[jax.ShapeDtypeStruct((m, WIDTH), F32), jax.ShapeDtypeStruct((m, WIDTH), F32),
                   jax.ShapeDtypeStruct((m, LANES), F32)],
        compiler_params=_params(("arbitrary",)),
        name="fox_prep",
    )(zq, zq, zf, qn_w, kn_w, bf_row, ones_bd)


def _foxattn_kernel(n_pad, n_qtiles, qr_ref, kr_ref, v_ref, c_ref, qw_ref, kw_ref, ones_ref,
                    o_ref, ko_ref, vo_ref, q_ref, k_ref, qa_ref, ka_ref, va_ref):
    t_len = qr_ref.shape[0]
    tq = t_len // n_qtiles
    pair = pl.program_id(1)
    inv = 1.0 / HEAD_DIM
    q_raw = qr_ref[...]
    k_raw = kr_ref[...]
    q_ref[...] = (q_raw * lax.rsqrt(_head_sum(q_raw * q_raw, ones_ref[...]) * inv + NORM_EPS) * qw_ref[...]
                  * (HEAD_DIM ** -0.5))
    k_ref[...] = k_raw * lax.rsqrt(_head_sum(k_raw * k_raw, ones_ref[...]) * inv + NORM_EPS) * kw_ref[...]
    ko_ref[...] = k_ref[n_pad:, :]
    vo_ref[...] = v_ref[n_pad:, :]
    lane = lax.broadcasted_iota(jnp.int32, (1, LANES), 1)
    row = lax.broadcasted_iota(jnp.int32, (t_len, 1), 0)
    causal = (lax.broadcasted_iota(jnp.int32, (tq, tq), 1) <= lax.broadcasted_iota(jnp.int32, (tq, tq), 0))
    for h in range(2):
        hmask = (lane // HEAD_DIM == h).astype(F32)
        sel = (lane == 2 * pair + h).astype(F32)
        c = jnp.sum(c_ref[...] * sel, axis=-1, keepdims=True)
        hi = c.astype(BF16).astype(F32)
        mid = (c - hi).astype(BF16).astype(F32)
        lo = (c - hi - mid).astype(BF16).astype(F32)
        base = HEAD_DIM * (1 - h)
        at = lambda i: (lane == base + i).astype(F32)
        ones_q = at(3) + at(4) + at(5)
        ones_k = at(0) + at(1) + at(2)
        hi_k = jnp.where(row >= n_pad, hi, -NEG_INF)
        qa_ref[...] = (q_ref[...] * hmask + hi * at(0) + mid * at(1) + lo * at(2) + ones_q).astype(BF16)
        ka_ref[...] = (k_ref[...] * hmask + ones_k - hi_k * at(3) - mid * at(4) - lo * at(5)).astype(BF16)
        va_ref[...] = (v_ref[...] * hmask + at(0)).astype(BF16)
        for qi in range(n_qtiles):
            q0 = qi * tq
            qa = qa_ref[q0:q0 + tq, :]
            s_diag = jnp.where(causal, _dot_nt(qa, ka_ref[q0:q0 + tq, :]), NEG_INF)
            m = jnp.max(s_diag, axis=-1, keepdims=True)
            if qi > 0:
                s_past = _dot_nt(qa, ka_ref[0:q0, :])
                m = jnp.maximum(m, jnp.max(s_past, axis=-1, keepdims=True))
            out = _dot(jnp.exp(s_diag - m), va_ref[q0:q0 + tq, :])
            if qi > 0:
                out = out + _dot(jnp.exp(s_past - m), va_ref[0:q0, :])
            l = jnp.sum(out * at(0), axis=-1, keepdims=True)
            out = out / l * hmask
            if h == 0:
                o_ref[q0:q0 + tq, :] = out
            else:
                o_ref[q0:q0 + tq, :] += out


def _foxattn(zq, cum, qn_w, kn_w, bsz, t_len, n_pad, n_qtiles):
    seq = lambda j0: pl.BlockSpec((t_len, LANES), lambda b, p: (b, j0 + p))
    real = pl.BlockSpec((None, t_len - n_pad, LANES), lambda b, p: (b, 0, p))
    pair_w = pl.BlockSpec((1, LANES), lambda b, p: (0, p))
    ones = jnp.asarray(_np_ones_bd(LANES, HEAD_DIM), BF16)
    blocks = WIDTH // LANES
    return pl.pallas_call(
        functools.partial(_foxattn_kernel, n_pad, n_qtiles),
        grid=(bsz, N_HEADS // 2),
        in_specs=[seq(0), seq(blocks), seq(2 * blocks), pl.BlockSpec((t_len, LANES), lambda b, p: (b, 0)),
                  pair_w, pair_w, _resident((LANES, LANES))],
        out_specs=[seq(0), real, real],
        out_shape=[jax.ShapeDtypeStruct((bsz * t_len, WIDTH), F32),
                   jax.ShapeDtypeStruct((bsz, t_len - n_pad, WIDTH), F32),
                   jax.ShapeDtypeStruct((bsz, t_len - n_pad, WIDTH), F32)],
        scratch_shapes=[pltpu.VMEM((t_len, LANES), F32)] * 2 + [pltpu.VMEM((t_len, LANES), BF16)] * 3,
        compiler_params=_params(("parallel", "parallel")),
        name="fox_attn",
    )(zq, zq, zq, cum, qn_w, kn_w, ones)


def _dec_consts():
    t = np.arange(PAGE)
    suf = (t[:, None] > t[None, :]).astype(np.float32)
    bdm = (np.arange(WIDTH)[None, :] // HEAD_DIM == np.arange(8)[:, None]).astype(np.float32)
    eye = (np.arange(LANES)[None, :] == np.arange(8)[:, None]).astype(np.float32)
    return suf, bdm, eye


def _dec_kernel(n_group, pt_ref, q_ref, kn_ref, vn_ref, lfn_ref, suf_ref, bdm_ref, eye_ref, *rest):
    k_refs = rest[0:n_group]
    v_refs = rest[n_group:2 * n_group]
    lf_refs = rest[2 * n_group:3 * n_group]
    o_ref = rest[3 * n_group]
    m_ref, l_ref, acc_ref, carry_ref = rest[3 * n_group + 1:]
    j = pl.program_id(1)
    bdm = bdm_ref[...]
    qbd = q_ref[...] * bdm
    lf_new = jnp.sum(eye_ref[...] * lfn_ref[...], axis=-1, keepdims=True)

    @pl.when(j == 0)
    def _():
        m_ref[...] = jnp.broadcast_to(jnp.sum(qbd * kn_ref[...], axis=-1, keepdims=True), (8, LANES))
        l_ref[...] = jnp.ones((8, LANES), F32)
        acc_ref[...] = vn_ref[...] * bdm
        carry_ref[...] = jnp.zeros((8, LANES), F32)

    carry = carry_ref[:, 0:1]
    qb = qbd.astype(BF16)
    scores = []
    for g in range(n_group):
        lf_t = lf_refs[g][...]
        suffix = _dot_x(lf_t, suf_ref[...], na=3)
        kt = k_refs[g][...].reshape(WIDTH, PAGE)
        scores.append(_dot(qb, kt) + (lf_new + carry + suffix))
        carry = carry + jnp.sum(lf_t, axis=-1, keepdims=True)
    carry_ref[...] = jnp.broadcast_to(carry, (8, LANES))
    s_all = jnp.concatenate(scores, axis=1)
    m_old = m_ref[:, 0:1]
    m_new = jnp.maximum(m_old, jnp.max(s_all, axis=-1, keepdims=True))
    alpha = jnp.exp(m_old - m_new)
    p = jnp.exp(s_all - m_new)
    l_ref[...] = jnp.broadcast_to(alpha * l_ref[:, 0:1] + jnp.sum(p, axis=-1, keepdims=True), (8, LANES))
    m_ref[...] = jnp.broadcast_to(m_new, (8, LANES))
    acc = alpha * acc_ref[...]
    pb = p.astype(BF16)
    for g in range(n_group):
        vt = v_refs[g][...].reshape(WIDTH, PAGE)
        acc = acc + _dot_nt(pb[:, g * PAGE:(g + 1) * PAGE], vt)
    acc_ref[...] = acc

    @pl.when(j == pl.num_programs(1) - 1)
    def _():
        o_ref[...] = jnp.sum(acc_ref[...] * bdm / l_ref[:, 0:1], axis=0, keepdims=True)


def _dec(page_table, qn, kn, vn, lfn, cache_k, cache_v, cache_lf, n_group):
    bsz, n_pages = page_table.shape
    kt = jnp.transpose(cache_k, (0, 2, 3, 1))
    vt = jnp.transpose(cache_v, (0, 2, 3, 1))
    lft = jnp.transpose(cache_lf, (0, 2, 1))
    consts = [jnp.asarray(c) for c in _dec_consts()]
    n_steps = n_pages // n_group
    tok = lambda n: pl.BlockSpec((None, 1, n), lambda b, j, pt: (b, 0, 0))
    const = lambda c: pl.BlockSpec(c.shape, lambda b, j, pt: (0,) * c.ndim, pipeline_mode=pl.Buffered(1))

    def page(shape, g):
        def index(b, j, pt):
            return (pt[b * n_pages + (n_pages - 1 - (j * n_group + g))],) + (0,) * len(shape)
        return pl.BlockSpec((None,) + shape, index)

    grid_spec = pltpu.PrefetchScalarGridSpec(
        num_scalar_prefetch=1,
        grid=(bsz, n_steps),
        in_specs=([tok(WIDTH), tok(WIDTH), tok(WIDTH), tok(LANES)] + [const(c) for c in consts]
                  + [page((N_HEADS, HEAD_DIM, PAGE), g) for g in range(n_group)]
                  + [page((N_HEADS, HEAD_DIM, PAGE), g) for g in range(n_group)]
                  + [page((N_HEADS, PAGE), g) for g in range(n_group)]),
        out_specs=tok(WIDTH),
        scratch_shapes=[pltpu.VMEM((8, LANES), F32), pltpu.VMEM((8, LANES), F32),
                        pltpu.VMEM((8, WIDTH), F32), pltpu.VMEM((8, LANES), F32)],
    )
    r3 = lambda x: x.reshape(bsz, 1, x.shape[-1])
    return pl.pallas_call(
        functools.partial(_dec_kernel, n_group),
        grid_spec=grid_spec,
        out_shape=jax.ShapeDtypeStruct((bsz, 1, WIDTH), F32),
        compiler_params=_params(("parallel", "arbitrary")),
        name="fox_decode",
    )(page_table.reshape(-1), r3(qn), r3(kn), r3(vn), r3(lfn), *consts,
      *([kt] * n_group), *([vt] * n_group), *([lft] * n_group)).reshape(bsz, WIDTH)


def _merge_kernel(x_ref, y_ref, bonus_ref, g_ref, fox_ref, zg_ref, gng_ref, gnb_ref, ones_bd,
                  wa_ref, wb_ref, wo_ref, o_ref):
    y = y_ref[...]
    inv = 1.0 / HEAD_DIM
    mean = _head_sum(y, ones_bd[...]) * inv
    d = y - mean
    var = _head_sum(d * d, ones_bd[...]) * inv
    yn = d * lax.rsqrt(var + GN_EPS) * gng_ref[...] + gnb_ref[...]
    rw = (yn + bonus_ref[...]) * g_ref[...]
    zg = zg_ref[...]
    merged = (_sigmoid(zg[:, 0:D_MODEL]) * _dot(rw, wa_ref[...])
              + _sigmoid(zg[:, D_MODEL:]) * _dot(fox_ref[...], wb_ref[...]))
    o_ref[...] = x_ref[...] + _dot(merged, wo_ref[...])


def _merge(x1, y, bonus, g, fox, zg, gn_g, gn_b, ones_bd, w_a, w_b, w_o, tm, keep=None):
    m = x1.shape[0]
    out_row = pl.BlockSpec((tm, D_MODEL), lambda i: (i, 0))
    if keep is None:
        m_out = m
        row = lambda n: pl.BlockSpec((tm, n), lambda i: (i, 0))
    else:
        seq_len, start = keep
        assert (seq_len - start) % tm == 0 and start % 8 == 0
        per_seq = (seq_len - start) // tm
        m_out = (m // seq_len) * per_seq * tm
        row = lambda n: pl.BlockSpec(
            (pl.Element(tm), pl.Element(n)),
            lambda i: (pl.multiple_of((i // per_seq) * seq_len + start + (i % per_seq) * tm, 8), 0))
    return pl.pallas_call(
        _merge_kernel,
        grid=(m_out // tm,),
        in_specs=[row(D_MODEL), row(WIDTH), row(WIDTH), row(WIDTH), row(WIDTH), row(2 * D_MODEL),
                  _resident((1, WIDTH)), _resident((1, WIDTH)), _resident((WIDTH, WIDTH)),
                  _resident((WIDTH, D_MODEL)), _resident((WIDTH, D_MODEL)), _resident((D_MODEL, D_MODEL))],
        out_specs=out_row,
        out_shape=jax.ShapeDtypeStruct((m_out, D_MODEL), F32),
        compiler_params=_params(("parallel",)),
        name="merge",
    )(x1, y, bonus, g, fox, zg, gn_g, gn_b, ones_bd, w_a, w_b, w_o)


def _to_blockdiag(state):
    bsz = state.shape[0]
    s = state.reshape(bsz, 2, 4, HEAD_DIM, HEAD_DIM)
    eye = jnp.eye(4, dtype=state.dtype)
    out = jnp.einsum("bqhvk,hg->bqhvgk", s, eye)
    return out.reshape(bsz, 2, QUAD, QUAD)


def _from_blockdiag(sbd):
    bsz = sbd.shape[0]
    s = sbd.reshape(bsz, 2, 4, HEAD_DIM, 4, HEAD_DIM)
    idx = jnp.arange(4)
    s = s[:, :, idx, :, idx, :]
    return jnp.moveaxis(s, 0, 2).reshape(bsz, N_HEADS, HEAD_DIM, HEAD_DIM)


def _layer_weights(l, ffn1_norm, ffn1_w_gate, ffn1_w_up, ffn1_w_down, mix_norm, w_in, mu_shift, w0,
                   w_decay_up, a0, w_aaa_up, w_gate_up, k_k, k_a, r_k, gn_g, gn_b, q_norm, k_norm, b_f,
                   w_a, w_b, w_o, ffn2_norm, ffn2_w_gate, ffn2_w_up, ffn2_w_down):
    row = lambda x: x.reshape(1, -1).astype(F32)
    bf = lambda x: x.astype(BF16)
    wi = w_in[l]
    w_f = jnp.pad(wi[:, R_COLS + QKV_COLS:R_COLS + QKV_COLS + N_HEADS], ((0, 0), (0, LANES - N_HEADS)))
    return dict(
        ffn1=(row(ffn1_norm[l]), ffn1_w_gate[l], ffn1_w_up[l], ffn1_w_down[l]),
        ffn2=(row(ffn2_norm[l]), ffn2_w_gate[l], ffn2_w_up[l], ffn2_w_down[l]),
        mix_norm=row(mix_norm[l]),
        w_r=bf(wi[:, 0:R_COLS]),
        w_l=bf(jnp.pad(wi[:, 3 * WIDTH:R_COLS], ((0, 0), (0, LORA_COLS - (R_COLS - 3 * WIDTH))))),
        w_q=bf(wi[:, R_COLS:R_COLS + QKV_COLS]), w_f=bf(w_f),
        w_g=bf(wi[:, R_COLS + QKV_COLS + N_HEADS:]),
        prep=dict(mu=row(mu_shift[l]), w0=row(w0[l]), wdu=bf(w_decay_up[l]), a0=row(a0[l]),
                  wau=bf(w_aaa_up[l]), wgu=bf(w_gate_up[l]), k_k=row(k_k[l]), k_a=row(k_a[l]),
                  r_k=row(r_k[l]), ones_bd=jnp.asarray(_np_ones_bd(WIDTH, HEAD_DIM), BF16)),
        gn_g=row(gn_g[l]), gn_b=row(gn_b[l]),
        q_norm=row(jnp.tile(q_norm[l], N_HEADS)), k_norm=row(jnp.tile(k_norm[l], N_HEADS)),
        b_f=jnp.pad(row(b_f[l]), ((0, 0), (0, LANES - N_HEADS))),
        w_a=bf(w_a[l]), w_b=bf(w_b[l]), w_o=bf(w_o[l]),
    )


def _prompt_layer(x1, w, bsz, t_len, n_pad, last):
    ones_bd = w["prep"]["ones_bd"]
    zr, zl, zq, zf, zg = _proj(x1, w["mix_norm"], w["w_r"], w["w_l"], w["w_q"], w["w_f"], w["w_g"], tm=272)
    s0 = jnp.zeros((bsz, 2, QUAD, QUAD), F32)
    flat = lambda arr: arr.reshape(bsz * t_len, WIDTH)
    y, g, bonus, s_t = _rec(zr.reshape(bsz, t_len, R_COLS), zl.reshape(bsz, t_len, LORA_COLS), s0, w["prep"],
                            c=64, first_chunk=n_pad // 64)
    y, g, bonus = flat(y), flat(g), flat(bonus)
    lf, cum = _logf_cumsum(zf, w["b_f"], t_len, n_pad)
    fox, kn, vv = _foxattn(zq, cum, w["q_norm"], w["k_norm"], bsz, t_len, n_pad, n_qtiles=4)
    x2 = _merge(x1, y, bonus, g, fox, zg, w["gn_g"], w["gn_b"], ones_bd,
                w["w_a"], w["w_b"], w["w_o"], tm=512 if last else 544,
                keep=(t_len, n_pad + N_META) if last else None)
    x3 = _ffn(x2, *w["ffn2"], tm=512 if last else 544)
    return x3, zr, kn, vv, lf, s_t


def _sample_layer(x1, w, shift_prev, wkv0, page_table, cache_k, cache_v, cache_lf):
    bsz = x1.shape[0]
    ones_bd = w["prep"]["ones_bd"]
    zr, _, zq, zf, zg = _proj(x1, w["mix_norm"], w["w_r"], w["w_l"], w["w_q"], w["w_f"], w["w_g"], tm=bsz)
    prepped = _prep(zr, shift_prev, w["prep"], tm=bsz)
    g, bonus = prepped[6], prepped[7]
    tok = jnp.stack(prepped[:6], axis=1).reshape(bsz, 6, N_HEADS, 1, HEAD_DIM)
    y, s_t = _step(tok, wkv0.astype(F32), nb=8)
    y = y.reshape(bsz, WIDTH)
    qn, kn, lf = _foxprep(zq, zf, w["q_norm"], w["k_norm"], w["b_f"], ones_bd)
    vn = zq[:, 2 * WIDTH:]
    fox = _dec(page_table, qn, kn, vn, lf, cache_k, cache_v, cache_lf, n_group=32)
    x2 = _merge(x1, y, bonus, g, fox, zg, w["gn_g"], w["gn_b"], ones_bd, w["w_a"], w["w_b"], w["w_o"],
                tm=bsz)
    x3 = _ffn(x2, *w["ffn2"], tm=bsz)
    return x3, zr, kn, vn, lf, s_t


def kernel(x_prompt, x_sample, cache_k, cache_v, cache_logf, state_wkv, state_shift, page_table, meta_tokens, ffn1_norm, ffn1_w_gate, ffn1_w_up, ffn1_w_down, mix_norm, w_in, mu_shift, w0, w_decay_up, a0, w_aaa_up, w_gate_up, k_k, k_a, r_k, gn_g, gn_b, q_norm, k_norm, b_f, w_a, w_b, w_o, ffn2_norm, ffn2_w_gate, ffn2_w_up, ffn2_w_down):
    depth = w_in.shape[0]
    bp, seq, _ = x_prompt.shape
    bs = x_sample.shape[0]
    l_tok = seq + N_META
    n_pad = (-l_tok) % LANES
    t_len = l_tok + n_pad
    head = jnp.concatenate([jnp.zeros((n_pad, D_MODEL), F32), meta_tokens.astype(F32)], axis=0)
    h_p = x_prompt.reshape(bp * seq, D_MODEL).astype(F32)
    h_s = x_sample.reshape(bs, D_MODEL).astype(F32)
    rows_p, rows_s = [], []
    for l in range(depth):
        w = _layer_weights(l, ffn1_norm, ffn1_w_gate, ffn1_w_up, ffn1_w_down, mix_norm, w_in, mu_shift, w0,
                           w_decay_up, a0, w_aaa_up, w_gate_up, k_k, k_a, r_k, gn_g, gn_b, q_norm, k_norm,
                           b_f, w_a, w_b, w_o, ffn2_norm, ffn2_w_gate, ffn2_w_up, ffn2_w_down)
        if l == 0:
            small = _ffn(jnp.concatenate([h_s, head], axis=0), *w["ffn1"], tm=bs + n_pad + N_META)
            x1_s, head1 = small[:bs], small[bs:]
            x1_p = _ffn(h_p, *w["ffn1"], tm=512, scatter=(t_len, n_pad + N_META, bp * t_len))
            x1_p = _place_rows(x1_p, head1, t_len)
        else:
            x1_s = _ffn(h_s, *w["ffn1"], tm=bs)
            x1_p = _ffn(h_p, *w["ffn1"], tm=544)
        h_p, zr, kn, vv, lf, s_t = _prompt_layer(x1_p, w, bp, t_len, n_pad, last=(l == depth - 1))
        rows_p.append((kn.reshape(bp, l_tok, N_HEADS, HEAD_DIM), vv.reshape(bp, l_tok, N_HEADS, HEAD_DIM),
                       lf.reshape(bp, t_len, LANES)[:, n_pad:, :N_HEADS],
                       _from_blockdiag(s_t),
                       zr.reshape(bp, t_len, R_COLS)[:, -1]))
        h_s, zr_s, kn_s, vn_s, lf_s, s_ts = _sample_layer(
            x1_s, w, state_shift[l].astype(F32), state_wkv[l], page_table, cache_k[l], cache_v[l],
            cache_logf[l])
        rows_s.append((kn_s.reshape(bs, 1, N_HEADS, HEAD_DIM), vn_s.reshape(bs, 1, N_HEADS, HEAD_DIM),
                       lf_s[:, :N_HEADS].reshape(bs, 1, N_HEADS), s_ts, zr_s))
    y_prompt = h_p.reshape(bp, seq, D_MODEL)
    y_sample = h_s.reshape(bs, 1, D_MODEL)
    stk = lambda rows, i: jnp.stack([r[i] for r in rows], axis=0)
    return (y_prompt, y_sample,
            stk(rows_p, 0), stk(rows_p, 1), stk(rows_p, 2), stk(rows_p, 3), stk(rows_p, 4),
            stk(rows_s, 0), stk(rows_s, 1), stk(rows_s, 2), stk(rows_s, 3), stk(rows_s, 4))
```
